```python
import jax, jax.numpy as jnp
from jax import lax
import numpy as np

D_MODEL = 1024
BATCH = 8
SEQ = 4096
DEPTH = 1

HEAD_DIM = 128
N_ATTN_HEADS = D_MODEL // HEAD_DIM
D_ATTN = N_ATTN_HEADS * HEAD_DIM
D_LRU = D_MODEL
N_LRU_BLOCKS = 8
LRU_BLOCK = D_LRU // N_LRU_BLOCKS
CONV_WIDTH = 4
LRU_C = 8.0
D_MIX = D_ATTN + D_LRU
D_PLE = 256
Q_BLOCK = 128
RMS_EPS = 1e-6
D_IN = 4 * D_ATTN + N_ATTN_HEADS + 2 * D_LRU
SPLIT_POINTS = [D_ATTN, 2 * D_ATTN, 3 * D_ATTN, 3 * D_ATTN + N_ATTN_HEADS,
                4 * D_ATTN + N_ATTN_HEADS, 4 * D_ATTN + N_ATTN_HEADS + D_LRU]

kernel_name = "hymba_fox_rglru_sandwich_ple"


def rmsnorm(x, g):
    xf = x.astype(jnp.float32)
    y = xf * lax.rsqrt(jnp.mean(xf * xf, axis=-1, keepdims=True) + RMS_EPS) * g.astype(jnp.float32)
    return y.astype(x.dtype)


def forgetting_attention(q, k, v, f_logit):
    B, S, H, Dh = q.shape
    nblk = S // Q_BLOCK
    scale = HEAD_DIM ** -0.5
    q = q.transpose(0, 2, 1, 3)
    k = k.transpose(0, 2, 1, 3)
    v = v.transpose(0, 2, 1, 3)
    c = jnp.cumsum(jax.nn.log_sigmoid(f_logit.astype(jnp.float32)), axis=1).transpose(0, 2, 1)
    qb = q.reshape(B, H, nblk, Q_BLOCK, Dh).transpose(2, 0, 1, 3, 4)
    cb = c.reshape(B, H, nblk, Q_BLOCK).transpose(2, 0, 1, 3)
    kpos = jnp.arange(S)

    def one_block(args):
        qi, ci, bi = args
        qpos = bi * Q_BLOCK + jnp.arange(Q_BLOCK)
        s = jnp.einsum('bhqd,bhkd->bhqk', qi, k, preferred_element_type=jnp.float32) * scale
        s = s + (ci[:, :, :, None] - c[:, :, None, :])
        s = jnp.where(kpos[None, :] <= qpos[:, None], s, -jnp.inf)
        w = jax.nn.softmax(s, axis=-1)
        return jnp.einsum('bhqk,bhkd->bhqd', w.astype(v.dtype), v)

    ob = lax.map(one_block, (qb, cb, jnp.arange(nblk)))
    return ob.transpose(1, 0, 3, 2, 4).reshape(B, S, H * Dh)


def causal_depthwise_conv(x, w, b):
    y = lax.conv_general_dilated(
        x, w[:, None, :].astype(x.dtype), window_strides=(1,),
        padding=[(CONV_WIDTH - 1, 0)], dimension_numbers=('NWC', 'WIO', 'NWC'),
        feature_group_count=x.shape[-1])
    return y + b


def rg_lru(xc, w_r, b_r, w_i, b_i, lam):
    B, S, _ = xc.shape
    xb = xc.reshape(B, S, N_LRU_BLOCKS, LRU_BLOCK)
    r = jax.nn.sigmoid((jnp.einsum('bsnj,njk->bsnk', xb, w_r).reshape(B, S, D_LRU) + b_r).astype(jnp.float32))
    i = jax.nn.sigmoid((jnp.einsum('bsnj,njk->bsnk', xb, w_i).reshape(B, S, D_LRU) + b_i).astype(jnp.float32))
    log_a = -LRU_C * r * jax.nn.softplus(-lam.astype(jnp.float32))
    a = jnp.exp(log_a)
    u = jnp.sqrt(-jnp.expm1(2.0 * log_a)) * (i * xc.astype(jnp.float32))

    def combine(left, right):
        a_l, b_l = left
        a_r, b_r2 = right
        return a_l * a_r, a_r * b_l + b_r2

    _, h = lax.associative_scan(combine, (a, u), axis=1)
    return h.astype(xc.dtype)


def setup_inputs(seed: int = 0) -> dict:
    key = jax.random.key(seed)
    ks = jax.random.split(key, 24)
    f32 = jnp.float32
    nrm = lambda k, shape, s: jax.random.normal(k, shape, f32) * s
    x = jax.random.normal(ks[0], (BATCH, SEQ, D_MODEL), f32)
    p = jax.random.normal(ks[1], (DEPTH, BATCH, SEQ, D_PLE), f32)
    w_in = nrm(ks[2], (DEPTH, D_MODEL, D_IN), D_MODEL ** -0.5)
    b_f = jnp.linspace(1.0, 6.0, N_ATTN_HEADS, dtype=f32)[None, :] + nrm(ks[3], (DEPTH, N_ATTN_HEADS), 0.1)
    pre_gain = 1.0 + nrm(ks[4], (DEPTH, D_MODEL), 0.05)
    post_gain = 1.0 + nrm(ks[5], (DEPTH, D_MODEL), 0.05)
    conv_w = nrm(ks[6], (DEPTH, CONV_WIDTH, D_LRU), CONV_WIDTH ** -0.5)
    conv_b = nrm(ks[7], (DEPTH, D_LRU), 0.01)
    w_rgate = nrm(ks[8], (DEPTH, N_LRU_BLOCKS, LRU_BLOCK, LRU_BLOCK), LRU_BLOCK ** -0.5)
    b_rgate = nrm(ks[9], (DEPTH, D_LRU), 0.01)
    w_igate = nrm(ks[10], (DEPTH, N_LRU_BLOCKS, LRU_BLOCK, LRU_BLOCK), LRU_BLOCK ** -0.5)
    b_igate = nrm(ks[11], (DEPTH, D_LRU), 0.01)
    a_pow = jax.random.uniform(ks[12], (DEPTH, D_LRU), f32, minval=0.9, maxval=0.999)
    a0 = a_pow ** (1.0 / LRU_C)
    lru_lambda = jnp.log(a0) - jnp.log1p(-a0)
    attn_out_gain = 1.0 + nrm(ks[13], (DEPTH, D_ATTN), 0.05)
    lru_out_gain = 1.0 + nrm(ks[14], (DEPTH, D_LRU), 0.05)
    w_out = nrm(ks[15], (DEPTH, D_MIX, D_MODEL), D_MIX ** -0.5)
    w_ple = nrm(ks[16], (DEPTH, D_PLE, D_MODEL), D_PLE ** -0.5)
    ple_gain = 1.0 + nrm(ks[17], (DEPTH, D_MODEL), 0.05)
    w_ple_gate = nrm(ks[18], (DEPTH, D_MODEL, D_MODEL), D_MODEL ** -0.5)
    b_ple_gate = nrm(ks[19], (DEPTH, D_MODEL), 0.01)
    return {"x": x, "p": p, "w_in": w_in, "b_f": b_f, "pre_gain": pre_gain,
            "post_gain": post_gain, "conv_w": conv_w, "conv_b": conv_b,
            "w_rgate": w_rgate, "b_rgate": b_rgate, "w_igate": w_igate, "b_igate": b_igate,
            "lru_lambda": lru_lambda, "attn_out_gain": attn_out_gain, "lru_out_gain": lru_out_gain,
            "w_out": w_out, "w_ple": w_ple, "ple_gain": ple_gain,
            "w_ple_gate": w_ple_gate, "b_ple_gate": b_ple_gate}


def reference(x, p, w_in, b_f, pre_gain, post_gain, conv_w, conv_b, w_rgate, b_rgate,
              w_igate, b_igate, lru_lambda, attn_out_gain, lru_out_gain, w_out,
              w_ple, ple_gain, w_ple_gate, b_ple_gate):
    B, S, _ = x.shape
    h = x
    for i in range(DEPTH):
        xn = rmsnorm(h, pre_gain[i])
        z = xn @ w_in[i]
        q, k, v, fl, g_attn, x_lru, g_lru = jnp.split(z, SPLIT_POINTS, axis=-1)
        fl = fl + b_f[i]
        o_attn = forgetting_attention(q.reshape(B, S, N_ATTN_HEADS, HEAD_DIM),
                                      k.reshape(B, S, N_ATTN_HEADS, HEAD_DIM),
                                      v.reshape(B, S, N_ATTN_HEADS, HEAD_DIM), fl)
        y_attn = rmsnorm(o_attn, attn_out_gain[i]) * jax.nn.silu(g_attn)
        xc = causal_depthwise_conv(x_lru, conv_w[i], conv_b[i])
        o_lru = rg_lru(xc, w_rgate[i], b_rgate[i], w_igate[i], b_igate[i], lru_lambda[i])
        y_lru = rmsnorm(o_lru, lru_out_gain[i]) * jax.nn.silu(g_lru)
        mix = jnp.concatenate([y_attn, y_lru], axis=-1) @ w_out[i]
        h = h + rmsnorm(mix, post_gain[i])
        e = rmsnorm(p[i] @ w_ple[i], ple_gain[i])
        gate = jax.nn.sigmoid(h @ w_ple_gate[i] + b_ple_gate[i])
        h = h + gate * e
    return h
```

```python
import functools
import math

import jax
import jax.numpy as jnp
from jax import lax
from jax.experimental import pallas as pl
from jax.experimental.pallas import tpu as pltpu

HEAD_DIM = 128
N_LRU_BLOCKS = 8
LRU_BLOCK = 128
CONV_WIDTH = 4
LRU_C = 8.0
RMS_EPS = 1e-6
N_FORGET_PAD = 128
SUBLANES = 8
VMEM_LIMIT_BYTES = 56 * 1024 * 1024
NEG_BIG = -1e30

F32 = jnp.float32
BF16 = jnp.bfloat16


def _sigmoid(x):
    return 0.5 * jnp.tanh(0.5 * x) + 0.5


def _rms_scale(x):
    return lax.rsqrt(jnp.mean(x * x, axis=-1, keepdims=True) + RMS_EPS)


def _in_proj_kernel(x_ref, gain_ref, w_ref, wf_ref,
                    qkv_ref, gattn_ref, xlru_ref, glru_ref, flt_ref,
                    *, d_attn, chunk, q_scale):
    x = x_ref[...]
    xn = (x * _rms_scale(x) * gain_ref[...]).astype(BF16)
    n_main = w_ref.shape[1]
    dests = ((qkv_ref, 0, 3 * d_attn),
             (gattn_ref, 3 * d_attn, 4 * d_attn),
             (xlru_ref, 4 * d_attn, 5 * d_attn),
             (glru_ref, 5 * d_attn, 6 * d_attn))
    assert n_main == 6 * d_attn
    for dst, lo, hi in dests:
        for c0 in range(lo, hi, chunk):
            z = jnp.dot(xn, w_ref[:, c0:c0 + chunk], preferred_element_type=F32)
            if c0 < d_attn:
                z = z * q_scale
            dst[:, c0 - lo:c0 - lo + chunk] = z.astype(dst.dtype)
    fl = jnp.dot(xn, wf_ref[...], preferred_element_type=F32)
    flt_ref[...] = fl.T[:flt_ref.shape[0], :]


def _in_proj(x2d, pre_gain, w_main, w_f, *, n_heads, tm, chunk):
    n, d = x2d.shape
    d_attn = n_heads * HEAD_DIM
    kern = functools.partial(_in_proj_kernel, d_attn=d_attn, chunk=chunk,
                             q_scale=HEAD_DIM ** -0.5)
    const = lambda i: (0, 0)
    row = lambda i: (i, 0)
    return pl.pallas_call(
        kern,
        grid=(n // tm,),
        in_specs=[
            pl.BlockSpec((tm, d), row),
            pl.BlockSpec((1, d), const),
            pl.BlockSpec(w_main.shape, const, pipeline_mode=pl.Buffered(1)),
            pl.BlockSpec(w_f.shape, const, pipeline_mode=pl.Buffered(1)),
        ],
        out_specs=[
            pl.BlockSpec((tm, 3 * d_attn), row),
            pl.BlockSpec((tm, d_attn), row),
            pl.BlockSpec((tm, d_attn), row),
            pl.BlockSpec((tm, d_attn), row),
            pl.BlockSpec((n_heads, tm), lambda i: (0, i)),
        ],
        out_shape=[
            jax.ShapeDtypeStruct((n, 3 * d_attn), BF16),
            jax.ShapeDtypeStruct((n, d_attn), BF16),
            jax.ShapeDtypeStruct((n, d_attn), BF16),
            jax.ShapeDtypeStruct((n, d_attn), BF16),
            jax.ShapeDtypeStruct((n_heads, n), F32),
        ],
        compiler_params=pltpu.CompilerParams(
            dimension_semantics=("arbitrary",),
            vmem_limit_bytes=VMEM_LIMIT_BYTES),
        name="in_proj",
    )(x2d, pre_gain, w_main, w_f)


def _forget_cumsum_kernel(flt_ref, bf_ref, c_ref):
    z = flt_ref[...] + bf_ref[...]
    ls = jnp.minimum(z, 0.0) - jnp.log1p(jnp.exp(-jnp.abs(z)))
    seq = ls.shape[1]
    lane = lax.broadcasted_iota(jnp.int32, ls.shape, 1)
    shift = 1
    while shift < seq:
        ls = ls + jnp.where(lane >= shift, pltpu.roll(ls, shift, axis=1), 0.0)
        shift *= 2
    c_ref[...] = ls


def _forget_cumsum(fl_t, b_f, *, seq):
    n_heads, n = fl_t.shape
    return pl.pallas_call(
        _forget_cumsum_kernel,
        grid=(n // seq,),
        in_specs=[pl.BlockSpec((n_heads, seq), lambda b: (0, b)),
                  pl.BlockSpec((n_heads, 1), lambda b: (0, 0))],
        out_specs=pl.BlockSpec((n_heads, seq), lambda b: (0, b)),
        out_shape=jax.ShapeDtypeStruct((n_heads, n), F32),
        compiler_params=pltpu.CompilerParams(dimension_semantics=("arbitrary",)),
        name="forget_cumsum",
    )(fl_t, b_f)


def _fox_attention_kernel(q_ref, k_ref, v_ref, c_ref, o_ref, m_scr, l_scr, acc_scr,
                          *, tile):
    h = pl.program_id(1)
    i = pl.program_id(2)
    q = q_ref[...]
    m_scr[...] = jnp.full(m_scr.shape, NEG_BIG, F32)
    l_scr[...] = jnp.zeros(l_scr.shape, F32)
    acc_scr[...] = jnp.zeros(acc_scr.shape, F32)

    def step(j, masked):
        start = pl.multiple_of(j * tile, tile)
        k = k_ref[pl.ds(start, tile), :]
        v = v_ref[pl.ds(start, tile), :]
        s = lax.dot_general(q, k, (((1,), (1,)), ((), ())),
                            preferred_element_type=F32)
        s = s - c_ref[pl.ds(h, 1), pl.ds(start, tile)]
        if masked:
            row = lax.broadcasted_iota(jnp.int32, s.shape, 0)
            col = lax.broadcasted_iota(jnp.int32, s.shape, 1)
            s = jnp.where(row >= col, s, NEG_BIG)
        m_prev = m_scr[...]
        m_new = jnp.maximum(m_prev, jnp.max(s, axis=1, keepdims=True))
        alpha = jnp.exp(m_prev - m_new)
        p = jnp.exp(s - m_new)
        l_scr[...] = alpha * l_scr[...] + jnp.sum(p, axis=1, keepdims=True)
        acc_scr[...] = alpha * acc_scr[...] + jnp.dot(
            p.astype(BF16), v, preferred_element_type=F32)
        m_scr[...] = m_new

    def body(j, carry):
        step(j, masked=False)
        return carry

    lax.fori_loop(0, i, body, 0)
    step(i, masked=True)
    o_ref[...] = (acc_scr[...] / l_scr[...]).astype(o_ref.dtype)


def _fox_attention(qkv, c, *, batch, seq, n_heads, tile):
    n = qkv.shape[0]
    nq = seq // tile
    kern = functools.partial(_fox_attention_kernel, tile=tile)
    return pl.pallas_call(
        kern,
        grid=(batch, n_heads, nq),
        in_specs=[
            pl.BlockSpec((tile, HEAD_DIM), lambda b, h, i: (b * nq + i, h)),
            pl.BlockSpec((seq, HEAD_DIM), lambda b, h, i: (b, n_heads + h)),
            pl.BlockSpec((seq, HEAD_DIM), lambda b, h, i: (b, 2 * n_heads + h)),
            pl.BlockSpec((n_heads, seq), lambda b, h, i: (0, b)),
        ],
        out_specs=pl.BlockSpec((tile, HEAD_DIM), lambda b, h, i: (b * nq + i, h)),
        out_shape=jax.ShapeDtypeStruct((n, n_heads * HEAD_DIM), BF16),
        scratch_shapes=[pltpu.VMEM((tile, 1), F32),
                        pltpu.VMEM((tile, 1), F32),
                        pltpu.VMEM((tile, HEAD_DIM), F32)],
        compiler_params=pltpu.CompilerParams(
            dimension_semantics=("arbitrary", "arbitrary", "arbitrary"),
            vmem_limit_bytes=VMEM_LIMIT_BYTES),
        name="fox_attention",
    )(qkv, qkv, qkv, c)


def _rg_lru_kernel(x_ref, g_ref, cw_ref, cb_ref, wri_ref, br_ref, bi_ref, lam_ref,
                   gain_ref, y_ref, xpad_scr, a_scr, u_scr, h_scr, carry_scr,
                   *, ts):
    @pl.when(pl.program_id(1) == 0)
    def _():
        xpad_scr[0:SUBLANES, :] = jnp.zeros((SUBLANES, xpad_scr.shape[1]), F32)
        carry_scr[...] = jnp.zeros(carry_scr.shape, F32)

    xpad_scr[SUBLANES:SUBLANES + ts, :] = x_ref[...].astype(F32)
    xc = cb_ref[...] + sum(
        cw_ref[j:j + 1, :] * xpad_scr[pl.ds(SUBLANES - (CONV_WIDTH - 1) + j, ts), :]
        for j in range(CONV_WIDTH))
    xpad_scr[0:SUBLANES, :] = xpad_scr[ts:ts + SUBLANES, :]

    xcb = xc.astype(BF16)
    neg_lam = -lam_ref[...]
    log_a_unit = -LRU_C * (jnp.maximum(neg_lam, 0.0)
                           + jnp.log1p(jnp.exp(-jnp.abs(neg_lam))))
    for n in range(N_LRU_BLOCKS):
        cols = slice(n * LRU_BLOCK, (n + 1) * LRU_BLOCK)
        gates = jnp.dot(xcb[:, cols], wri_ref[n], preferred_element_type=F32)
        r = _sigmoid(gates[:, :LRU_BLOCK] + br_ref[:, cols])
        gi = _sigmoid(gates[:, LRU_BLOCK:] + bi_ref[:, cols])
        log_a = r * log_a_unit[:, cols]
        t = jnp.tanh(log_a)
        a_scr[:, cols] = jnp.exp(log_a)
        u_scr[:, cols] = jnp.sqrt(-2.0 * t / (1.0 - t)) * (gi * xc[:, cols])

    sub = lax.broadcasted_iota(jnp.int32, (SUBLANES, a_scr.shape[1]), 0)

    def group(g, h_in):
        rows = pl.ds(pl.multiple_of(g * SUBLANES, SUBLANES), SUBLANES)
        a = a_scr[rows, :]
        u = u_scr[rows, :]
        d = 1
        while d < SUBLANES:
            a_sh = jnp.where(sub >= d, pltpu.roll(a, d, axis=0), 1.0)
            u_sh = jnp.where(sub >= d, pltpu.roll(u, d, axis=0), 0.0)
            u = a * u_sh + u
            a = a * a_sh
            d *= 2
        hcur = a * h_in + u
        h_scr[rows, :] = hcur
        return hcur[SUBLANES - 1:SUBLANES, :]

    h_last = lax.fori_loop(0, ts // SUBLANES, group, carry_scr[0:1, :])
    carry_scr[...] = jnp.broadcast_to(h_last, carry_scr.shape)

    o = h_scr[...]
    g = g_ref[...].astype(F32)
    y_ref[...] = (o * _rms_scale(o) * gain_ref[...] * (g * _sigmoid(g))).astype(y_ref.dtype)


def _rg_lru(x_lru, g_lru, conv_w, conv_b, w_ri, b_r, b_i, lam, gain, *, batch, seq, ts):
    n, d = x_lru.shape
    ns = seq // ts
    kern = functools.partial(_rg_lru_kernel, ts=ts)
    row = lambda b, s: (b * ns + s, 0)
    const2 = lambda b, s: (0, 0)
    vec = pl.BlockSpec((1, d), const2)
    return pl.pallas_call(
        kern,
        grid=(batch, ns),
        in_specs=[
            pl.BlockSpec((ts, d), row),
            pl.BlockSpec((ts, d), row),
            pl.BlockSpec((CONV_WIDTH, d), const2),
            vec,
            pl.BlockSpec(w_ri.shape, lambda b, s: (0, 0, 0)),
            vec, vec, vec, vec,
        ],
        out_specs=pl.BlockSpec((ts, d), row),
        out_shape=jax.ShapeDtypeStruct((n, d), BF16),
        scratch_shapes=[pltpu.VMEM((ts + 2 * SUBLANES, d), F32),
                        pltpu.VMEM((ts, d), F32),
                        pltpu.VMEM((ts, d), F32),
                        pltpu.VMEM((ts, d), F32),
                        pltpu.VMEM((SUBLANES, d), F32)],
        compiler_params=pltpu.CompilerParams(
            dimension_semantics=("arbitrary", "arbitrary"),
            vmem_limit_bytes=VMEM_LIMIT_BYTES),
        name="rg_lru",
    )(x_lru, g_lru, conv_w, conv_b, w_ri, b_r, b_i, lam, gain)


def _out_proj_kernel(oa_ref, ga_ref, yl_ref, x_ref, p_ref, again_ref, wout_ref,
                     pgain_ref, wple_ref, plegain_ref, wpg_ref, bpg_ref, out_ref,
                     *, d_attn):
    oa = oa_ref[...].astype(F32)
    ga = ga_ref[...].astype(F32)
    ya = oa * _rms_scale(oa) * again_ref[...] * (ga * _sigmoid(ga))
    mix = jnp.dot(ya.astype(BF16), wout_ref[0:d_attn, :], preferred_element_type=F32)
    mix = mix + jnp.dot(yl_ref[...], wout_ref[d_attn:, :], preferred_element_type=F32)
    h1 = x_ref[...] + mix * _rms_scale(mix) * pgain_ref[...]
    e = jnp.dot(p_ref[...].astype(BF16), wple_ref[...], preferred_element_type=F32)
    e = e * _rms_scale(e) * plegain_ref[...]
    gate = _sigmoid(jnp.dot(h1.astype(BF16), wpg_ref[...], preferred_element_type=F32)
                    + bpg_ref[...])
    out_ref[...] = h1 + gate * e


def _out_proj(o_attn, g_attn, y_lru, x2d, p2d, attn_gain, w_out, post_gain, w_ple,
              ple_gain, w_pg, b_pg, *, tm):
    n, d = x2d.shape
    d_attn = o_attn.shape[1]
    d_lru = y_lru.shape[1]
    d_ple = p2d.shape[1]
    kern = functools.partial(_out_proj_kernel, d_attn=d_attn)
    row = lambda i: (i, 0)
    const = lambda i: (0, 0)
    single = pl.Buffered(1)
    return pl.pallas_call(
        kern,
        grid=(n // tm,),
        in_specs=[
            pl.BlockSpec((tm, d_attn), row),
            pl.BlockSpec((tm, d_attn), row),
            pl.BlockSpec((tm, d_lru), row),
            pl.BlockSpec((tm, d), row),
            pl.BlockSpec((tm, d_ple), row),
            pl.BlockSpec((1, d_attn), const),
            pl.BlockSpec(w_out.shape, const, pipeline_mode=single),
            pl.BlockSpec((1, d), const),
            pl.BlockSpec(w_ple.shape, const, pipeline_mode=single),
            pl.BlockSpec((1, d), const),
            pl.BlockSpec(w_pg.shape, const, pipeline_mode=single),
            pl.BlockSpec((1, d), const),
        ],
        out_specs=pl.BlockSpec((tm, d), row),
        out_shape=jax.ShapeDtypeStruct((n, d), F32),
        compiler_params=pltpu.CompilerParams(
            dimension_semantics=("arbitrary",),
            vmem_limit_bytes=VMEM_LIMIT_BYTES),
        name="out_proj",
    )(o_attn, g_attn, y_lru, x2d, p2d, attn_gain, w_out, post_gain, w_ple, ple_gain,
      w_pg, b_pg)


def _layer(h2d, p2d, w_in, b_f, pre_gain, post_gain, conv_w, conv_b, w_rgate, b_rgate,
           w_igate, b_igate, lru_lambda, attn_out_gain, lru_out_gain, w_out, w_ple,
           ple_gain, w_ple_gate, b_ple_gate, *, batch, seq):
    d = h2d.shape[1]
    n_heads = b_f.shape[0]
    d_attn = n_heads * HEAD_DIM
    assert w_in.shape[1] == 4 * d_attn + n_heads + 2 * d and d == d_attn
    fl_lo = 3 * d_attn
    w_main = jnp.concatenate([w_in[:, :fl_lo], w_in[:, fl_lo + n_heads:]],
                             axis=1).astype(BF16)
    w_f = jnp.pad(w_in[:, fl_lo:fl_lo + n_heads],
                  ((0, 0), (0, N_FORGET_PAD - n_heads))).astype(BF16)
    vec = lambda v: v.reshape(1, -1)

    qkv, g_attn, x_lru, g_lru, fl_t = _in_proj(
        h2d, vec(pre_gain), w_main, w_f, n_heads=n_heads, tm=512, chunk=512)
    c = _forget_cumsum(fl_t, b_f.reshape(n_heads, 1), seq=seq)
    o_attn = _fox_attention(qkv, c, batch=batch, seq=seq, n_heads=n_heads, tile=512)
    w_ri = jnp.concatenate([w_rgate, w_igate], axis=-1).astype(BF16)
    y_lru = _rg_lru(x_lru, g_lru, conv_w, vec(conv_b), w_ri, vec(b_rgate), vec(b_igate),
                    vec(lru_lambda), vec(lru_out_gain), batch=batch, seq=seq, ts=256)
    return _out_proj(o_attn, g_attn, y_lru, h2d, p2d, vec(attn_out_gain),
                     w_out.astype(BF16), vec(post_gain), w_ple.astype(BF16),
                     vec(ple_gain), w_ple_gate.astype(BF16), vec(b_ple_gate), tm=512)


def kernel(x, p, w_in, b_f, pre_gain, post_gain, conv_w, conv_b, w_rgate, b_rgate,
           w_igate, b_igate, lru_lambda, attn_out_gain, lru_out_gain, w_out, w_ple,
           ple_gain, w_ple_gate, b_ple_gate):
    batch, seq, d = x.shape
    h = x.reshape(batch * seq, d)
    for i in range(w_in.shape[0]):
        h = _layer(h, p[i].reshape(batch * seq, -1), w_in[i], b_f[i], pre_gain[i],
                   post_gain[i], conv_w[i], conv_b[i], w_rgate[i], b_rgate[i],
                   w_igate[i], b_igate[i], lru_lambda[i], attn_out_gain[i],
                   lru_out_gain[i], w_out[i], w_ple[i], ple_gain[i], w_ple_gate[i],
                   b_ple_gate[i], batch=batch, seq=seq)
    return h.reshape(batch, seq, d)
```

```python
import functools
import math

import jax
import jax.numpy as jnp
from jax import lax
from jax.experimental import pallas as pl
from jax.experimental.pallas import tpu as pltpu

HEAD_DIM = 128
N_LRU_BLOCKS = 8
LRU_BLOCK = 128
CONV_WIDTH = 4
LRU_C = 8.0
RMS_EPS = 1e-6
N_FORGET_PAD = 128
SUBLANES = 8
VMEM_LIMIT_BYTES = 56 * 1024 * 1024
NEG_BIG = -1e30
LOG2_E = math.log2(math.e)

F32 = jnp.float32
BF16 = jnp.bfloat16


def _sigmoid(x):
    return 0.5 * jnp.tanh(0.5 * x) + 0.5


def _rms_scale(x):
    return lax.rsqrt(jnp.mean(x * x, axis=-1, keepdims=True) + RMS_EPS)


def _in_proj_kernel(x_ref, gain_ref, w_ref, wf_ref,
                    qkv_ref, gattn_ref, xlru_ref, glru_ref, flt_ref,
                    *, d_attn, chunk, q_scale):
    x = x_ref[...]
    xn = (x * _rms_scale(x) * gain_ref[...]).astype(BF16)
    n_main = w_ref.shape[1]
    dests = ((qkv_ref, 0, 3 * d_attn),
             (gattn_ref, 3 * d_attn, 4 * d_attn),
             (xlru_ref, 4 * d_attn, 5 * d_attn),
             (glru_ref, 5 * d_attn, 6 * d_attn))
    assert n_main == 6 * d_attn
    for dst, lo, hi in dests:
        for c0 in range(lo, hi, chunk):
            z = jnp.dot(xn, w_ref[:, c0:c0 + chunk], preferred_element_type=F32)
            if c0 < d_attn:
                z = z * q_scale
            dst[:, c0 - lo:c0 - lo + chunk] = z.astype(dst.dtype)
    fl = jnp.dot(xn, wf_ref[...], preferred_element_type=F32)
    flt_ref[...] = fl.T[:flt_ref.shape[0], :]


def _in_proj(x2d, pre_gain, w_main, w_f, *, n_heads, tm, chunk):
    n, d = x2d.shape
    d_attn = n_heads * HEAD_DIM
    kern = functools.partial(_in_proj_kernel, d_attn=d_attn, chunk=chunk,
                             q_scale=HEAD_DIM ** -0.5 * LOG2_E)
    const = lambda i: (0, 0)
    row = lambda i: (i, 0)
    return pl.pallas_call(
        kern,
        grid=(n // tm,),
        in_specs=[
            pl.BlockSpec((tm, d), row),
            pl.BlockSpec((1, d), const),
            pl.BlockSpec(w_main.shape, const, pipeline_mode=pl.Buffered(1)),
            pl.BlockSpec(w_f.shape, const, pipeline_mode=pl.Buffered(1)),
        ],
        out_specs=[
            pl.BlockSpec((tm, 3 * d_attn), row),
            pl.BlockSpec((tm, d_attn), row),
            pl.BlockSpec((tm, d_attn), row),
            pl.BlockSpec((tm, d_attn), row),
            pl.BlockSpec((n_heads, tm), lambda i: (0, i)),
        ],
        out_shape=[
            jax.ShapeDtypeStruct((n, 3 * d_attn), BF16),
            jax.ShapeDtypeStruct((n, d_attn), BF16),
            jax.ShapeDtypeStruct((n, d_attn), BF16),
            jax.ShapeDtypeStruct((n, d_attn), BF16),
            jax.ShapeDtypeStruct((n_heads, n), F32),
        ],
        compiler_params=pltpu.CompilerParams(
            dimension_semantics=("arbitrary",),
            vmem_limit_bytes=VMEM_LIMIT_BYTES),
        name="in_proj",
    )(x2d, pre_gain, w_main, w_f)


def _forget_cumsum_kernel(flt_ref, bf_ref, c_ref):
    z = flt_ref[...] + bf_ref[...]
    ls = jnp.minimum(z, 0.0) - jnp.log1p(jnp.exp(-jnp.abs(z)))
    seq = ls.shape[1]
    lane = lax.broadcasted_iota(jnp.int32, ls.shape, 1)
    shift = 1
    while shift < seq:
        ls = ls + jnp.where(lane >= shift, pltpu.roll(ls, shift, axis=1), 0.0)
        shift *= 2
    c_ref[...] = ls * LOG2_E


def _forget_cumsum(fl_t, b_f, *, seq):
    n_heads, n = fl_t.shape
    return pl.pallas_call(
        _forget_cumsum_kernel,
        grid=(n // seq,),
        in_specs=[pl.BlockSpec((n_heads, seq), lambda b: (0, b)),
                  pl.BlockSpec((n_heads, 1), lambda b: (0, 0))],
        out_specs=pl.BlockSpec((n_heads, seq), lambda b: (0, b)),
        out_shape=jax.ShapeDtypeStruct((n_heads, n), F32),
        compiler_params=pltpu.CompilerParams(dimension_semantics=("arbitrary",)),
        name="forget_cumsum",
    )(fl_t, b_f)


def _fox_attention_kernel(q_ref, k_ref, v_ref, c_ref, o_ref,
                          s0_scr, s1_scr, m_scr, l_scr, acc_scr, *, tile):
    h = pl.program_id(1)
    i = pl.program_id(2)
    n_chunks = tile // HEAD_DIM
    m_scr[...] = jnp.full(m_scr.shape, NEG_BIG, F32)
    l_scr[...] = jnp.zeros(l_scr.shape, F32)
    acc_scr[...] = jnp.zeros(acc_scr.shape, F32)

    def scores(j, s_scr):
        start = pl.multiple_of(j * tile, tile)
        s = lax.dot_general(q_ref[...], k_ref[pl.ds(start, tile), :],
                            (((1,), (1,)), ((), ())), preferred_element_type=F32)
        s_scr[...] = s - c_ref[pl.ds(h, 1), pl.ds(start, tile)]

    def consume(j, s_scr, masked):
        start = pl.multiple_of(j * tile, tile)
        s = s_scr[...]
        if masked:
            row = lax.broadcasted_iota(jnp.int32, s.shape, 0)
            col = lax.broadcasted_iota(jnp.int32, s.shape, 1)
            s = jnp.where(row >= col, s, NEG_BIG)
        m_prev = m_scr[...]
        m_new = jnp.maximum(m_prev, jnp.max(s, axis=1, keepdims=True))
        alpha = jnp.exp2(m_prev - m_new)
        p = [jnp.exp2(s[:, c * HEAD_DIM:(c + 1) * HEAD_DIM] - m_new)
             for c in range(n_chunks)]
        l_scr[...] = alpha * l_scr[...] + sum(p)
        pb = jnp.concatenate([pc.astype(BF16) for pc in p], axis=1)
        acc_scr[...] = alpha * acc_scr[...] + jnp.dot(
            pb, v_ref[pl.ds(start, tile), :], preferred_element_type=F32)
        m_scr[...] = m_new

    def pair(t, carry):
        j = 2 * t
        scores(j + 1, s1_scr)
        consume(j, s0_scr, masked=False)
        scores(j + 2, s0_scr)
        consume(j + 1, s1_scr, masked=False)
        return carry

    scores(0, s0_scr)
    lax.fori_loop(0, i // 2, pair, 0)

    @pl.when(i % 2 == 0)
    def _():
        consume(i, s0_scr, masked=True)

    @pl.when(i % 2 == 1)
    def _():
        scores(i, s1_scr)
        consume(i - 1, s0_scr, masked=False)
        consume(i, s1_scr, masked=True)

    l = jnp.sum(l_scr[...], axis=1, keepdims=True)
    o_ref[...] = (acc_scr[...] / l).astype(o_ref.dtype)


def _fox_attention(qkv, c, *, batch, seq, n_heads, tile):
    n = qkv.shape[0]
    nq = seq // tile
    kern = functools.partial(_fox_attention_kernel, tile=tile)
    return pl.pallas_call(
        kern,
        grid=(batch, n_heads, nq),
        in_specs=[
            pl.BlockSpec((tile, HEAD_DIM), lambda b, h, i: (b * nq + i, h)),
            pl.BlockSpec((seq, HEAD_DIM), lambda b, h, i: (b, n_heads + h)),
            pl.BlockSpec((seq, HEAD_DIM), lambda b, h, i: (b, 2 * n_heads + h)),
            pl.BlockSpec((n_heads, seq), lambda b, h, i: (0, b)),
        ],
        out_specs=pl.BlockSpec((tile, HEAD_DIM), lambda b, h, i: (b * nq + i, h)),
        out_shape=jax.ShapeDtypeStruct((n, n_heads * HEAD_DIM), BF16),
        scratch_shapes=[pltpu.VMEM((tile, tile), F32),
                        pltpu.VMEM((tile, tile), F32),
                        pltpu.VMEM((tile, HEAD_DIM), F32),
                        pltpu.VMEM((tile, HEAD_DIM), F32),
                        pltpu.VMEM((tile, HEAD_DIM), F32)],
        compiler_params=pltpu.CompilerParams(
            dimension_semantics=("arbitrary", "arbitrary", "arbitrary"),
            vmem_limit_bytes=VMEM_LIMIT_BYTES),
        name="fox_attention",
    )(qkv, qkv, qkv, c)


def _rg_lru_kernel(x_ref, g_ref, cw_ref, cb_ref, wri_ref, br_ref, bi_ref, lam_ref,
                   gain_ref, y_ref, xpad_scr, a_scr, u_scr, h_scr, carry_scr,
                   *, ts):
    @pl.when(pl.program_id(1) == 0)
    def _():
        xpad_scr[0:SUBLANES, :] = jnp.zeros((SUBLANES, xpad_scr.shape[1]), F32)
        carry_scr[...] = jnp.zeros(carry_scr.shape, F32)

    xpad_scr[SUBLANES:SUBLANES + ts, :] = x_ref[...].astype(F32)
    xc = cb_ref[...] + sum(
        cw_ref[j:j + 1, :] * xpad_scr[pl.ds(SUBLANES - (CONV_WIDTH - 1) + j, ts), :]
        for j in range(CONV_WIDTH))
    xpad_scr[0:SUBLANES, :] = xpad_scr[ts:ts + SUBLANES, :]

    xcb = xc.astype(BF16)
    neg_lam = -lam_ref[...]
    log_a_unit = -LRU_C * (jnp.maximum(neg_lam, 0.0)
                           + jnp.log1p(jnp.exp(-jnp.abs(neg_lam))))
    for n in range(N_LRU_BLOCKS):
        cols = slice(n * LRU_BLOCK, (n + 1) * LRU_BLOCK)
        gates = jnp.dot(xcb[:, cols], wri_ref[n], preferred_element_type=F32)
        r = _sigmoid(gates[:, :LRU_BLOCK] + br_ref[:, cols])
        gi = _sigmoid(gates[:, LRU_BLOCK:] + bi_ref[:, cols])
        log_a = r * log_a_unit[:, cols]
        t = jnp.tanh(log_a)
        a_scr[:, cols] = jnp.exp(log_a)
        u_scr[:, cols] = jnp.sqrt(-2.0 * t / (1.0 - t)) * (gi * xc[:, cols])

    sub = lax.broadcasted_iota(jnp.int32, (SUBLANES, a_scr.shape[1]), 0)

    def group(g, h_in):
        rows = pl.ds(pl.multiple_of(g * SUBLANES, SUBLANES), SUBLANES)
        a = a_scr[rows, :]
        u = u_scr[rows, :]
        d = 1
        while d < SUBLANES:
            a_sh = jnp.where(sub >= d, pltpu.roll(a, d, axis=0), 1.0)
            u_sh = jnp.where(sub >= d, pltpu.roll(u, d, axis=0), 0.0)
            u = a * u_sh + u
            a = a * a_sh
            d *= 2
        hcur = a * h_in + u
        h_scr[rows, :] = hcur
        return hcur[SUBLANES - 1:SUBLANES, :]

    h_last = lax.fori_loop(0, ts // SUBLANES, group, carry_scr[0:1, :])
    carry_scr[...] = jnp.broadcast_to(h_last, carry_scr.shape)

    o = h_scr[...]
    g = g_ref[...].astype(F32)
    y_ref[...] = (o * _rms_scale(o) * gain_ref[...] * (g * _sigmoid(g))).astype(y_ref.dtype)


def _rg_lru(x_lru, g_lru, conv_w, conv_b, w_ri, b_r, b_i, lam, gain, *, batch, seq, ts):
    n, d = x_lru.shape
    ns = seq // ts
    kern = functools.partial(_rg_lru_kernel, ts=ts)
    row = lambda b, s: (b * ns + s, 0)
    const2 = lambda b, s: (0, 0)
    vec = pl.BlockSpec((1, d), const2)
    return pl.pallas_call(
        kern,
        grid=(batch, ns),
        in_specs=[
            pl.BlockSpec((ts, d), row),
            pl.BlockSpec((ts, d), row),
            pl.BlockSpec((CONV_WIDTH, d), const2),
            vec,
            pl.BlockSpec(w_ri.shape, lambda b, s: (0, 0, 0)),
            vec, vec, vec, vec,
        ],
        out_specs=pl.BlockSpec((ts, d), row),
        out_shape=jax.ShapeDtypeStruct((n, d), BF16),
        scratch_shapes=[pltpu.VMEM((ts + 2 * SUBLANES, d), F32),
                        pltpu.VMEM((ts, d), F32),
                        pltpu.VMEM((ts, d), F32),
                        pltpu.VMEM((ts, d), F32),
                        pltpu.VMEM((SUBLANES, d), F32)],
        compiler_params=pltpu.CompilerParams(
            dimension_semantics=("arbitrary", "arbitrary"),
            vmem_limit_bytes=VMEM_LIMIT_BYTES),
        name="rg_lru",
    )(x_lru, g_lru, conv_w, conv_b, w_ri, b_r, b_i, lam, gain)


def _out_proj_kernel(oa_ref, ga_ref, yl_ref, x_ref, p_ref, again_ref, wout_ref,
                     pgain_ref, wple_ref, plegain_ref, wpg_ref, bpg_ref, out_ref,
                     *, d_attn):
    oa = oa_ref[...].astype(F32)
    ga = ga_ref[...].astype(F32)
    ya = oa * _rms_scale(oa) * again_ref[...] * (ga * _sigmoid(ga))
    mix = jnp.dot(ya.astype(BF16), wout_ref[0:d_attn, :], preferred_element_type=F32)
    mix = mix + jnp.dot(yl_ref[...], wout_ref[d_attn:, :], preferred_element_type=F32)
    h1 = x_ref[...] + mix * _rms_scale(mix) * pgain_ref[...]
    e = jnp.dot(p_ref[...].astype(BF16), wple_ref[...], preferred_element_type=F32)
    e = e * _rms_scale(e) * plegain_ref[...]
    gate = _sigmoid(jnp.dot(h1.astype(BF16), wpg_ref[...], preferred_element_type=F32)
                    + bpg_ref[...])
    out_ref[...] = h1 + gate * e


def _out_proj(o_attn, g_attn, y_lru, x2d, p2d, attn_gain, w_out, post_gain, w_ple,
              ple_gain, w_pg, b_pg, *, tm):
    n, d = x2d.shape
    d_attn = o_attn.shape[1]
    d_lru = y_lru.shape[1]
    d_ple = p2d.shape[1]
    kern = functools.partial(_out_proj_kernel, d_attn=d_attn)
    row = lambda i: (i, 0)
    const = lambda i: (0, 0)
    single = pl.Buffered(1)
    return pl.pallas_call(
        kern,
        grid=(n // tm,),
        in_specs=[
            pl.BlockSpec((tm, d_attn), row),
            pl.BlockSpec((tm, d_attn), row),
            pl.BlockSpec((tm, d_lru), row),
            pl.BlockSpec((tm, d), row),
            pl.BlockSpec((tm, d_ple), row),
            pl.BlockSpec((1, d_attn), const),
            pl.BlockSpec(w_out.shape, const, pipeline_mode=single),
            pl.BlockSpec((1, d), const),
            pl.BlockSpec(w_ple.shape, const, pipeline_mode=single),
            pl.BlockSpec((1, d), const),
            pl.BlockSpec(w_pg.shape, const, pipeline_mode=single),
            pl.BlockSpec((1, d), const),
        ],
        out_specs=pl.BlockSpec((tm, d), row),
        out_shape=jax.ShapeDtypeStruct((n, d), F32),
        compiler_params=pltpu.CompilerParams(
            dimension_semantics=("arbitrary",),
            vmem_limit_bytes=VMEM_LIMIT_BYTES),
        name="out_proj",
    )(o_attn, g_attn, y_lru, x2d, p2d, attn_gain, w_out, post_gain, w_ple, ple_gain,
      w_pg, b_pg)


def _layer(h2d, p2d, w_in, b_f, pre_gain, post_gain, conv_w, conv_b, w_rgate, b_rgate,
           w_igate, b_igate, lru_lambda, attn_out_gain, lru_out_gain, w_out, w_ple,
           ple_gain, w_ple_gate, b_ple_gate, *, batch, seq):
    d = h2d.shape[1]
    n_heads = b_f.shape[0]
    d_attn = n_heads * HEAD_DIM
    assert w_in.shape[1] == 4 * d_attn + n_heads + 2 * d and d == d_attn
    fl_lo = 3 * d_attn
    w_main = jnp.concatenate([w_in[:, :fl_lo], w_in[:, fl_lo + n_heads:]],
                             axis=1).astype(BF16)
    w_f = jnp.pad(w_in[:, fl_lo:fl_lo + n_heads],
                  ((0, 0), (0, N_FORGET_PAD - n_heads))).astype(BF16)
    vec = lambda v: v.reshape(1, -1)

    qkv, g_attn, x_lru, g_lru, fl_t = _in_proj(
        h2d, vec(pre_gain), w_main, w_f, n_heads=n_heads, tm=512, chunk=512)
    c = _forget_cumsum(fl_t, b_f.reshape(n_heads, 1), seq=seq)
    o_attn = _fox_attention(qkv, c, batch=batch, seq=seq, n_heads=n_heads, tile=512)
    w_ri = jnp.concatenate([w_rgate, w_igate], axis=-1).astype(BF16)
    y_lru = _rg_lru(x_lru, g_lru, conv_w, vec(conv_b), w_ri, vec(b_rgate), vec(b_igate),
                    vec(lru_lambda), vec(lru_out_gain), batch=batch, seq=seq, ts=256)
    return _out_proj(o_attn, g_attn, y_lru, h2d, p2d, vec(attn_out_gain),
                     w_out.astype(BF16), vec(post_gain), w_ple.astype(BF16),
                     vec(ple_gain), w_ple_gate.astype(BF16), vec(b_ple_gate), tm=512)


def kernel(x, p, w_in, b_f, pre_gain, post_gain, conv_w, conv_b, w_rgate, b_rgate,
           w_igate, b_igate, lru_lambda, attn_out_gain, lru_out_gain, w_out, w_ple,
           ple_gain, w_ple_gate, b_ple_gate):
    batch, seq, d = x.shape
    h = x.reshape(batch * seq, d)
    for i in range(w_in.shape[0]):
        h = _layer(h, p[i].reshape(batch * seq, -1), w_in[i], b_f[i], pre_gain[i],
                   post_gain[i], conv_w[i], conv_b[i], w_rgate[i], b_rgate[i],
                   w_igate[i], b_igate[i], lru_lambda[i], attn_out_gain[i],
                   lru_out_gain[i], w_out[i], w_ple[i], ple_gain[i], w_ple_gate[i],
                   b_ple_gate[i], batch=batch, seq=seq)
    return h.reshape(batch, seq, d)
```

```python
import functools
import math

import jax
import jax.numpy as jnp
from jax import lax
from jax.experimental import pallas as pl
from jax.experimental.pallas import tpu as pltpu

HEAD_DIM = 128
N_LRU_BLOCKS = 8
LRU_BLOCK = 128
CONV_WIDTH = 4
LRU_C = 8.0
RMS_EPS = 1e-6
N_FORGET_PAD = 128
SUBLANES = 8
VMEM_LIMIT_BYTES = 56 * 1024 * 1024
NEG_BIG = -1e30
TINY = 1e-30
LOG2_E = math.log2(math.e)

F32 = jnp.float32
BF16 = jnp.bfloat16


def _sigmoid(x):
    return 0.5 * jnp.tanh(0.5 * x) + 0.5


def _rms_scale(x):
    return lax.rsqrt(jnp.mean(x * x, axis=-1, keepdims=True) + RMS_EPS)


def _in_proj_kernel(x_ref, gain_ref, w_ref, wf_ref,
                    qkv_ref, gattn_ref, xlru_ref, glru_ref, flt_ref,
                    *, d_attn, chunk, q_scale):
    x = x_ref[...]
    xn = (x * _rms_scale(x) * gain_ref[...]).astype(BF16)
    n_main = w_ref.shape[1]
    dests = ((qkv_ref, 0, 3 * d_attn),
             (gattn_ref, 3 * d_attn, 4 * d_attn),
             (xlru_ref, 4 * d_attn, 5 * d_attn),
             (glru_ref, 5 * d_attn, 6 * d_attn))
    assert n_main == 6 * d_attn
    for dst, lo, hi in dests:
        for c0 in range(lo, hi, chunk):
            z = jnp.dot(xn, w_ref[:, c0:c0 + chunk], preferred_element_type=F32)
            if c0 < d_attn:
                z = z * q_scale
            dst[:, c0 - lo:c0 - lo + chunk] = z.astype(dst.dtype)
    fl = jnp.dot(xn, wf_ref[...], preferred_element_type=F32)
    flt_ref[...] = fl.T[:flt_ref.shape[0], :]


def _in_proj(x2d, pre_gain, w_main, w_f, *, n_heads, tm, chunk):
    n, d = x2d.shape
    d_attn = n_heads * HEAD_DIM
    kern = functools.partial(_in_proj_kernel, d_attn=d_attn, chunk=chunk,
                             q_scale=HEAD_DIM ** -0.5 * LOG2_E)
    const = lambda i: (0, 0)
    row = lambda i: (i, 0)
    return pl.pallas_call(
        kern,
        grid=(n // tm,),
        in_specs=[
            pl.BlockSpec((tm, d), row),
            pl.BlockSpec((1, d), const),
            pl.BlockSpec(w_main.shape, const, pipeline_mode=pl.Buffered(1)),
            pl.BlockSpec(w_f.shape, const, pipeline_mode=pl.Buffered(1)),
        ],
        out_specs=[
            pl.BlockSpec((tm, 3 * d_attn), row),
            pl.BlockSpec((tm, d_attn), row),
            pl.BlockSpec((tm, d_attn), row),
            pl.BlockSpec((tm, d_attn), row),
            pl.BlockSpec((n_heads, tm), lambda i: (0, i)),
        ],
        out_shape=[
            jax.ShapeDtypeStruct((n, 3 * d_attn), BF16),
            jax.ShapeDtypeStruct((n, d_attn), BF16),
            jax.ShapeDtypeStruct((n, d_attn), BF16),
            jax.ShapeDtypeStruct((n, d_attn), BF16),
            jax.ShapeDtypeStruct((n_heads, n), F32),
        ],
        compiler_params=pltpu.CompilerParams(
            dimension_semantics=("arbitrary",),
            vmem_limit_bytes=VMEM_LIMIT_BYTES),
        name="in_proj",
    )(x2d, pre_gain, w_main, w_f)


def _forget_cumsum_kernel(flt_ref, bf_ref, c_ref):
    z = flt_ref[...] + bf_ref[...]
    ls = jnp.minimum(z, 0.0) - jnp.log1p(jnp.exp(-jnp.abs(z)))
    seq = ls.shape[1]
    lane = lax.broadcasted_iota(jnp.int32, ls.shape, 1)
    shift = 1
    while shift < seq:
        ls = ls + jnp.where(lane >= shift, pltpu.roll(ls, shift, axis=1), 0.0)
        shift *= 2
    c_ref[...] = ls * LOG2_E


def _forget_cumsum(fl_t, b_f, *, seq):
    n_heads, n = fl_t.shape
    return pl.pallas_call(
        _forget_cumsum_kernel,
        grid=(n // seq,),
        in_specs=[pl.BlockSpec((n_heads, seq), lambda b: (0, b)),
                  pl.BlockSpec((n_heads, 1), lambda b: (0, 0))],
        out_specs=pl.BlockSpec((n_heads, seq), lambda b: (0, b)),
        out_shape=jax.ShapeDtypeStruct((n_heads, n), F32),
        compiler_params=pltpu.CompilerParams(dimension_semantics=("arbitrary",)),
        name="forget_cumsum",
    )(fl_t, b_f)


def _fox_attention_kernel(q_ref, k_ref, v_ref, c_ref, o_ref,
                          s0_scr, s1_scr, m_scr, l_scr, acc_scr, *, tile):
    h = pl.program_id(1)
    i = pl.program_id(2)
    n_chunks = tile // HEAD_DIM
    m_scr[...] = jnp.full(m_scr.shape, NEG_BIG, F32)
    l_scr[...] = jnp.zeros(l_scr.shape, F32)
    acc_scr[...] = jnp.zeros(acc_scr.shape, F32)

    def scores(j, s_scr):
        start = pl.multiple_of(j * tile, tile)
        s = lax.dot_general(q_ref[...], k_ref[pl.ds(start, tile), :],
                            (((1,), (1,)), ((), ())), preferred_element_type=F32)
        s_scr[...] = s - c_ref[pl.ds(h, 1), pl.ds(start, tile)]

    def consume(j, s_scr, masked):
        start = pl.multiple_of(j * tile, tile)
        s = s_scr[...]
        if masked:
            row = lax.broadcasted_iota(jnp.int32, s.shape, 0)
            col = lax.broadcasted_iota(jnp.int32, s.shape, 1)
            s = jnp.where(row >= col, s, NEG_BIG)
        m_prev = m_scr[...]
        m_new = jnp.maximum(m_prev, jnp.max(s, axis=1, keepdims=True))
        alpha = jnp.exp2(m_prev - m_new)
        p = [jnp.exp2(s[:, c * HEAD_DIM:(c + 1) * HEAD_DIM] - m_new)
             for c in range(n_chunks)]
        l_scr[...] = alpha * l_scr[...] + sum(p)
        pb = jnp.concatenate([pc.astype(BF16) for pc in p], axis=1)
        acc_scr[...] = alpha * acc_scr[...] + jnp.dot(
            pb, v_ref[pl.ds(start, tile), :], preferred_element_type=F32)
        m_scr[...] = m_new

    def pair(t, carry):
        j = 2 * t
        scores(j + 1, s1_scr)
        consume(j, s0_scr, masked=False)
        scores(j + 2, s0_scr)
        consume(j + 1, s1_scr, masked=False)
        return carry

    scores(0, s0_scr)
    lax.fori_loop(0, i // 2, pair, 0)

    @pl.when(i % 2 == 0)
    def _():
        consume(i, s0_scr, masked=True)

    @pl.when(i % 2 == 1)
    def _():
        scores(i, s1_scr)
        consume(i - 1, s0_scr, masked=False)
        consume(i, s1_scr, masked=True)

    l = jnp.sum(l_scr[...], axis=1, keepdims=True)
    o_ref[...] = (acc_scr[...] / l).astype(o_ref.dtype)


def _fox_attention(qkv, c, *, batch, seq, n_heads, tile):
    n = qkv.shape[0]
    nq = seq // tile
    kern = functools.partial(_fox_attention_kernel, tile=tile)
    return pl.pallas_call(
        kern,
        grid=(batch, n_heads, nq),
        in_specs=[
            pl.BlockSpec((tile, HEAD_DIM), lambda b, h, i: (b * nq + i, h)),
            pl.BlockSpec((seq, HEAD_DIM), lambda b, h, i: (b, n_heads + h)),
            pl.BlockSpec((seq, HEAD_DIM), lambda b, h, i: (b, 2 * n_heads + h)),
            pl.BlockSpec((n_heads, seq), lambda b, h, i: (0, b)),
        ],
        out_specs=pl.BlockSpec((tile, HEAD_DIM), lambda b, h, i: (b * nq + i, h)),
        out_shape=jax.ShapeDtypeStruct((n, n_heads * HEAD_DIM), BF16),
        scratch_shapes=[pltpu.VMEM((tile, tile), F32),
                        pltpu.VMEM((tile, tile), F32),
                        pltpu.VMEM((tile, HEAD_DIM), F32),
                        pltpu.VMEM((tile, HEAD_DIM), F32),
                        pltpu.VMEM((tile, HEAD_DIM), F32)],
        compiler_params=pltpu.CompilerParams(
            dimension_semantics=("arbitrary", "arbitrary", "arbitrary"),
            vmem_limit_bytes=VMEM_LIMIT_BYTES),
        name="fox_attention",
    )(qkv, qkv, qkv, c)


def _segment_permutation(ts):
    p = jnp.arange(ts)
    t = (p % SUBLANES) * (ts // SUBLANES) + p // SUBLANES
    return (t[:, None] == jnp.arange(ts)[None, :]).astype(BF16)


def _rg_lru_kernel(x_ref, g_ref, perm_ref, unperm_ref, cw_ref, cb_ref, wri_ref, br_ref,
                   bi_ref, lam_ref, gain_ref, y_ref,
                   xpad_scr, g_scr, a_scr, u_scr, h_scr, acum_scr, tail_scr, carry_scr,
                   *, ts):
    seg = ts // SUBLANES
    head = (CONV_WIDTH - 1) * SUBLANES
    d = x_ref.shape[1]

    @pl.when(pl.program_id(1) == 0)
    def _():
        tail_scr[...] = jnp.zeros(tail_scr.shape, F32)
        carry_scr[...] = jnp.zeros(carry_scr.shape, F32)

    perm = perm_ref[...]
    x = jnp.dot(perm, x_ref[...], preferred_element_type=F32)
    g_scr[...] = jnp.dot(perm, g_ref[...], preferred_element_type=F32)
    xpad_scr[head:head + ts, :] = x
    sub = lax.broadcasted_iota(jnp.int32, (SUBLANES, d), 0)
    for m in range(CONV_WIDTH - 1):
        grp = slice(m * SUBLANES, (m + 1) * SUBLANES)
        cur = x[ts - head + m * SUBLANES:ts - head + (m + 1) * SUBLANES, :]
        xpad_scr[grp, :] = jnp.where(sub == 0, pltpu.roll(tail_scr[grp, :], 1, axis=0),
                                     pltpu.roll(cur, 1, axis=0))
    tail_scr[...] = x[ts - head:ts, :]
    xc = cb_ref[...] + sum(
        cw_ref[j:j + 1, :] * xpad_scr[j * SUBLANES:j * SUBLANES + ts, :]
        for j in range(CONV_WIDTH))

    neg_lam = -lam_ref[...]
    k_unit = -LRU_C * (jnp.maximum(neg_lam, 0.0) + jnp.log1p(jnp.exp(-jnp.abs(neg_lam))))
    c1 = (0.5 * LOG2_E) * k_unit
    hb_r = 0.5 * br_ref[...]
    hb_i = 0.5 * bi_ref[...]
    xcb = xc.astype(BF16)
    for n in range(N_LRU_BLOCKS):
        cols = slice(n * LRU_BLOCK, (n + 1) * LRU_BLOCK)
        gates = jnp.dot(xcb[:, cols], wri_ref[n], preferred_element_type=F32)
        tr = jnp.tanh(gates[:, :LRU_BLOCK] + hb_r[:, cols])
        ti = jnp.tanh(gates[:, LRU_BLOCK:] + hb_i[:, cols])
        a = jnp.exp2(tr * c1[:, cols] + c1[:, cols])
        y2 = 1.0 - a * a
        a_scr[:, cols] = a
        u_scr[:, cols] = (y2 * lax.rsqrt(jnp.maximum(y2, TINY))) * (
            (ti + 1.0) * (0.5 * xc[:, cols]))

    def local(k, carry):
        h, acc = carry
        rows = pl.ds(pl.multiple_of(k * SUBLANES, SUBLANES), SUBLANES)
        a = a_scr[rows, :]
        h = a * h + u_scr[rows, :]
        acc = a * acc
        h_scr[rows, :] = h
        acum_scr[rows, :] = acc
        return h, acc

    h_end, a_end = lax.fori_loop(
        0, seg, local, (jnp.zeros((SUBLANES, d), F32), jnp.ones((SUBLANES, d), F32)),
        unroll=True)

    c = carry_scr[0:1, :]
    seg_in = [c]
    for r in range(SUBLANES):
        c = a_end[r:r + 1, :] * c + h_end[r:r + 1, :]
        seg_in.append(c)
    carry_scr[...] = jnp.broadcast_to(seg_in[SUBLANES], carry_scr.shape)
    c_in = jnp.concatenate(seg_in[:SUBLANES], axis=0)

    gain = gain_ref[...]

    def finish(k, carry):
        rows = pl.ds(pl.multiple_of(k * SUBLANES, SUBLANES), SUBLANES)
        h = h_scr[rows, :] + acum_scr[rows, :] * c_in
        hg = 0.5 * g_scr[rows, :]
        a_scr[rows, :] = (h * _rms_scale(h)) * gain * (hg * (jnp.tanh(hg) + 1.0))
        return carry

    lax.fori_loop(0, seg, finish, 0, unroll=True)
    y = jnp.dot(unperm_ref[...], a_scr[...].astype(BF16), preferred_element_type=F32)
    y_ref[...] = y.astype(y_ref.dtype)


def _rg_lru(x_lru, g_lru, conv_w, conv_b, w_ri, b_r, b_i, lam, gain, *, batch, seq, ts):
    n, d = x_lru.shape
    ns = seq // ts
    kern = functools.partial(_rg_lru_kernel, ts=ts)
    perm = _segment_permutation(ts)
    row = lambda b, s: (b * ns + s, 0)
    const2 = lambda b, s: (0, 0)
    vec = pl.BlockSpec((1, d), const2)
    mat = pl.BlockSpec((ts, ts), const2)
    return pl.pallas_call(
        kern,
        grid=(batch, ns),
        in_specs=[
            pl.BlockSpec((ts, d), row),
            pl.BlockSpec((ts, d), row),
            mat, mat,
            pl.BlockSpec((CONV_WIDTH, d), const2),
            vec,
            pl.BlockSpec(w_ri.shape, lambda b, s: (0, 0, 0)),
            vec, vec, vec, vec,
        ],
        out_specs=pl.BlockSpec((ts, d), row),
        out_shape=jax.ShapeDtypeStruct((n, d), BF16),
        scratch_shapes=[pltpu.VMEM((ts + (CONV_WIDTH - 1) * SUBLANES, d), F32),
                        pltpu.VMEM((ts, d), F32),
                        pltpu.VMEM((ts, d), F32),
                        pltpu.VMEM((ts, d), F32),
                        pltpu.VMEM((ts, d), F32),
                        pltpu.VMEM((ts, d), F32),
                        pltpu.VMEM(((CONV_WIDTH - 1) * SUBLANES, d), F32),
                        pltpu.VMEM((SUBLANES, d), F32)],
        compiler_params=pltpu.CompilerParams(
            dimension_semantics=("arbitrary", "arbitrary"),
            vmem_limit_bytes=VMEM_LIMIT_BYTES),
        name="rg_lru",
    )(x_lru, g_lru, perm, perm.T, conv_w, conv_b, w_ri, b_r, b_i, lam, gain)


def _out_proj_kernel(oa_ref, ga_ref, yl_ref, x_ref, p_ref, again_ref, wout_ref,
                     pgain_ref, wple_ref, plegain_ref, wpg_ref, bpg_ref, out_ref,
                     *, d_attn):
    oa = oa_ref[...].astype(F32)
    ga = ga_ref[...].astype(F32)
    ya = oa * _rms_scale(oa) * again_ref[...] * (ga * _sigmoid(ga))
    mix = jnp.dot(ya.astype(BF16), wout_ref[0:d_attn, :], preferred_element_type=F32)
    mix = mix + jnp.dot(yl_ref[...], wout_ref[d_attn:, :], preferred_element_type=F32)
    h1 = x_ref[...] + mix * _rms_scale(mix) * pgain_ref[...]
    e = jnp.dot(p_ref[...].astype(BF16), wple_ref[...], preferred_element_type=F32)
    e = e * _rms_scale(e) * plegain_ref[...]
    gate = _sigmoid(jnp.dot(h1.astype(BF16), wpg_ref[...], preferred_element_type=F32)
                    + bpg_ref[...])
    out_ref[...] = h1 + gate * e


def _out_proj(o_attn, g_attn, y_lru, x2d, p2d, attn_gain, w_out, post_gain, w_ple,
              ple_gain, w_pg, b_pg, *, tm):
    n, d = x2d.shape
    d_attn = o_attn.shape[1]
    d_lru = y_lru.shape[1]
    d_ple = p2d.shape[1]
    kern = functools.partial(_out_proj_kernel, d_attn=d_attn)
    row = lambda i: (i, 0)
    const = lambda i: (0, 0)
    single = pl.Buffered(1)
    return pl.pallas_call(
        kern,
        grid=(n // tm,),
        in_specs=[
            pl.BlockSpec((tm, d_attn), row),
            pl.BlockSpec((tm, d_attn), row),
            pl.BlockSpec((tm, d_lru), row),
            pl.BlockSpec((tm, d), row),
            pl.BlockSpec((tm, d_ple), row),
            pl.BlockSpec((1, d_attn), const),
            pl.BlockSpec(w_out.shape, const, pipeline_mode=single),
            pl.BlockSpec((1, d), const),
            pl.BlockSpec(w_ple.shape, const, pipeline_mode=single),
            pl.BlockSpec((1, d), const),
            pl.BlockSpec(w_pg.shape, const, pipeline_mode=single),
            pl.BlockSpec((1, d), const),
        ],
        out_specs=pl.BlockSpec((tm, d), row),
        out_shape=jax.ShapeDtypeStruct((n, d), F32),
        compiler_params=pltpu.CompilerParams(
            dimension_semantics=("arbitrary",),
            vmem_limit_bytes=VMEM_LIMIT_BYTES),
        name="out_proj",
    )(o_attn, g_attn, y_lru, x2d, p2d, attn_gain, w_out, post_gain, w_ple, ple_gain,
      w_pg, b_pg)


def _layer(h2d, p2d, w_in, b_f, pre_gain, post_gain, conv_w, conv_b, w_rgate, b_rgate,
           w_igate, b_igate, lru_lambda, attn_out_gain, lru_out_gain, w_out, w_ple,
           ple_gain, w_ple_gate, b_ple_gate, *, batch, seq):
    d = h2d.shape[1]
    n_heads = b_f.shape[0]
    d_attn = n_heads * HEAD_DIM
    assert w_in.shape[1] == 4 * d_attn + n_heads + 2 * d and d == d_attn
    fl_lo = 3 * d_attn
    w_main = jnp.concatenate([w_in[:, :fl_lo], w_in[:, fl_lo + n_heads:]],
                             axis=1).astype(BF16)
    w_f = jnp.pad(w_in[:, fl_lo:fl_lo + n_heads],
                  ((0, 0), (0, N_FORGET_PAD - n_heads))).astype(BF16)
    vec = lambda v: v.reshape(1, -1)

    qkv, g_attn, x_lru, g_lru, fl_t = _in_proj(
        h2d, vec(pre_gain), w_main, w_f, n_heads=n_heads, tm=512, chunk=512)
    c = _forget_cumsum(fl_t, b_f.reshape(n_heads, 1), seq=seq)
    o_attn = _fox_attention(qkv, c, batch=batch, seq=seq, n_heads=n_heads, tile=512)
    w_ri = (0.5 * jnp.concatenate([w_rgate, w_igate], axis=-1)).astype(BF16)
    y_lru = _rg_lru(x_lru, g_lru, conv_w, vec(conv_b), w_ri, vec(b_rgate), vec(b_igate),
                    vec(lru_lambda), vec(lru_out_gain), batch=batch, seq=seq, ts=256)
    return _out_proj(o_attn, g_attn, y_lru, h2d, p2d, vec(attn_out_gain),
                     w_out.astype(BF16), vec(post_gain), w_ple.astype(BF16),
                     vec(ple_gain), w_ple_gate.astype(BF16), vec(b_ple_gate), tm=512)


def kernel(x, p, w_in, b_f, pre_gain, post_gain, conv_w, conv_b, w_rgate, b_rgate,
           w_igate, b_igate, lru_lambda, attn_out_gain, lru_out_gain, w_out, w_ple,
           ple_gain, w_ple_gate, b_ple_gate):
    batch, seq, d = x.shape
    h = x.reshape(batch * seq, d)
    for i in range(w_in.shape[0]):
        h = _layer(h, p[i].reshape(batch * seq, -1), w_in[i], b_f[i], pre_gain[i],
                   post_gain[i], conv_w[i], conv_b[i], w_rgate[i], b_rgate[i],
                   w_igate[i], b_igate[i], lru_lambda[i], attn_out_gain[i],
                   lru_out_gain[i], w_out[i], w_ple[i], ple_gain[i], w_ple_gate[i],
                   b_ple_gate[i], batch=batch, seq=seq)
    return h.reshape(batch, seq, d)
```

```python
import functools
import math

import jax
import jax.numpy as jnp
from jax import lax
from jax.experimental import pallas as pl
from jax.experimental.pallas import tpu as pltpu

HEAD_DIM = 128
N_LRU_BLOCKS = 8
LRU_BLOCK = 128
CONV_WIDTH = 4
LRU_C = 8.0
RMS_EPS = 1e-6
N_FORGET_PAD = 128
SUBLANES = 8
VMEM_LIMIT_BYTES = 56 * 1024 * 1024
NEG_BIG = -1e30
TINY = 1e-30
LOG2_E = math.log2(math.e)

F32 = jnp.float32
BF16 = jnp.bfloat16


def _sigmoid(x):
    return 0.5 * jnp.tanh(0.5 * x) + 0.5


def _rms_scale(x):
    return lax.rsqrt(jnp.mean(x * x, axis=-1, keepdims=True) + RMS_EPS)


def _in_proj_kernel(x_ref, gain_ref, w_ref, wf_ref,
                    qkv_ref, gattn_ref, xlru_ref, glru_ref, flt_ref,
                    *, d_attn, chunk, q_scale):
    x = x_ref[...]
    xn = (x * _rms_scale(x) * gain_ref[...]).astype(BF16)
    n_main = w_ref.shape[1]
    dests = ((qkv_ref, 0, 3 * d_attn),
             (gattn_ref, 3 * d_attn, 4 * d_attn),
             (xlru_ref, 4 * d_attn, 5 * d_attn),
             (glru_ref, 5 * d_attn, 6 * d_attn))
    assert n_main == 6 * d_attn
    for dst, lo, hi in dests:
        for c0 in range(lo, hi, chunk):
            z = jnp.dot(xn, w_ref[:, c0:c0 + chunk], preferred_element_type=F32)
            if c0 < d_attn:
                z = z * q_scale
            dst[:, c0 - lo:c0 - lo + chunk] = z.astype(dst.dtype)
    fl = jnp.dot(xn, wf_ref[...], preferred_element_type=F32)
    flt_ref[...] = fl.T[:flt_ref.shape[0], :]


def _in_proj(x2d, pre_gain, w_main, w_f, *, n_heads, tm, chunk):
    n, d = x2d.shape
    d_attn = n_heads * HEAD_DIM
    kern = functools.partial(_in_proj_kernel, d_attn=d_attn, chunk=chunk,
                             q_scale=HEAD_DIM ** -0.5 * LOG2_E)
    const = lambda i: (0, 0)
    row = lambda i: (i, 0)
    return pl.pallas_call(
        kern,
        grid=(n // tm,),
        in_specs=[
            pl.BlockSpec((tm, d), row),
            pl.BlockSpec((1, d), const),
            pl.BlockSpec(w_main.shape, const, pipeline_mode=pl.Buffered(1)),
            pl.BlockSpec(w_f.shape, const, pipeline_mode=pl.Buffered(1)),
        ],
        out_specs=[
            pl.BlockSpec((tm, 3 * d_attn), row),
            pl.BlockSpec((tm, d_attn), row),
            pl.BlockSpec((tm, d_attn), row),
            pl.BlockSpec((tm, d_attn), row),
            pl.BlockSpec((n_heads, tm), lambda i: (0, i)),
        ],
        out_shape=[
            jax.ShapeDtypeStruct((n, 3 * d_attn), BF16),
            jax.ShapeDtypeStruct((n, d_attn), BF16),
            jax.ShapeDtypeStruct((n, d_attn), BF16),
            jax.ShapeDtypeStruct((n, d_attn), BF16),
            jax.ShapeDtypeStruct((n_heads, n), F32),
        ],
        compiler_params=pltpu.CompilerParams(
            dimension_semantics=("arbitrary",),
            vmem_limit_bytes=VMEM_LIMIT_BYTES),
        name="in_proj",
    )(x2d, pre_gain, w_main, w_f)


def _forget_cumsum_kernel(flt_ref, bf_ref, c_ref):
    z = flt_ref[...] + bf_ref[...]
    ls = jnp.minimum(z, 0.0) - jnp.log1p(jnp.exp(-jnp.abs(z)))
    seq = ls.shape[1]
    lane = lax.broadcasted_iota(jnp.int32, ls.shape, 1)
    shift = 1
    while shift < seq:
        ls = ls + jnp.where(lane >= shift, pltpu.roll(ls, shift, axis=1), 0.0)
        shift *= 2
    c_ref[...] = ls * LOG2_E


def _forget_cumsum(fl_t, b_f, *, seq):
    n_heads, n = fl_t.shape
    return pl.pallas_call(
        _forget_cumsum_kernel,
        grid=(n // seq,),
        in_specs=[pl.BlockSpec((n_heads, seq), lambda b: (0, b)),
                  pl.BlockSpec((n_heads, 1), lambda b: (0, 0))],
        out_specs=pl.BlockSpec((n_heads, seq), lambda b: (0, b)),
        out_shape=jax.ShapeDtypeStruct((n_heads, n), F32),
        compiler_params=pltpu.CompilerParams(dimension_semantics=("arbitrary",)),
        name="forget_cumsum",
    )(fl_t, b_f)


def _fox_attention_kernel(q_ref, k_ref, v_ref, c_ref, o_ref,
                          s0_scr, s1_scr, r0_scr, r1_scr, vext_scr, m_scr, acc_scr,
                          *, tile):
    h = pl.program_id(1)
    i = pl.program_id(2)
    n_chunks = tile // HEAD_DIM
    half_a = slice(0, tile)
    half_b = slice(tile, 2 * tile)
    both = slice(0, 2 * tile)

    @pl.when(i == 0)
    def _():
        vext_scr[:, :HEAD_DIM] = v_ref[...]
        vext_scr[:, HEAD_DIM:] = jnp.ones((vext_scr.shape[0], HEAD_DIM), BF16)

    m_scr[...] = jnp.full(m_scr.shape, NEG_BIG, F32)
    acc_scr[...] = jnp.zeros(acc_scr.shape, F32)

    def scores(j, s_scr, r_scr, rows):
        start = pl.multiple_of(j * tile, tile)
        s = lax.dot_general(q_ref[rows, :], k_ref[pl.ds(start, tile), :],
                            (((1,), (1,)), ((), ())), preferred_element_type=F32)
        s = s - c_ref[pl.ds(h, 1), pl.ds(start, tile)]
        s_scr[rows, :] = s
        r_scr[rows, :] = jnp.broadcast_to(jnp.max(s, axis=1, keepdims=True),
                                          (s.shape[0], HEAD_DIM))

    def consume(j, s_scr, r_scr, rows, diag):
        start = pl.multiple_of(j * tile, tile)
        m_prev = m_scr[rows, :]
        if diag:
            s = s_scr[rows, :]
            row = lax.broadcasted_iota(jnp.int32, s.shape, 0)
            col = lax.broadcasted_iota(jnp.int32, s.shape, 1)
            s = jnp.where(row >= col, s, NEG_BIG)
            m_new = jnp.maximum(m_prev, jnp.max(s, axis=1, keepdims=True))
            s_chunks = [s[:, c * HEAD_DIM:(c + 1) * HEAD_DIM] for c in range(n_chunks)]
        else:
            m_new = jnp.maximum(m_prev, r_scr[rows, :])
            s_chunks = [s_scr[rows, c * HEAD_DIM:(c + 1) * HEAD_DIM]
                        for c in range(n_chunks)]
        alpha = jnp.exp2(m_prev - m_new)
        pb = jnp.concatenate([jnp.exp2(sc - m_new).astype(BF16) for sc in s_chunks],
                             axis=1)
        pv = jnp.dot(pb, vext_scr[pl.ds(start, tile), :], preferred_element_type=F32)
        acc_scr[rows, :HEAD_DIM] = alpha * acc_scr[rows, :HEAD_DIM] + pv[:, :HEAD_DIM]
        acc_scr[rows, HEAD_DIM:] = alpha * acc_scr[rows, HEAD_DIM:] + pv[:, HEAD_DIM:]
        m_scr[rows, :] = m_new

    def pair(t, carry):
        j = 2 * t
        scores(j + 1, s1_scr, r1_scr, both)
        consume(j, s0_scr, r0_scr, both, diag=False)
        scores(j + 2, s0_scr, r0_scr, both)
        consume(j + 1, s1_scr, r1_scr, both, diag=False)
        return carry

    scores(0, s0_scr, r0_scr, both)
    lax.fori_loop(0, i, pair, 0)
    scores(2 * i + 1, s1_scr, r1_scr, half_b)
    consume(2 * i, s0_scr, r0_scr, half_a, diag=True)
    consume(2 * i, s0_scr, r0_scr, half_b, diag=False)
    consume(2 * i + 1, s1_scr, r1_scr, half_b, diag=True)

    o_ref[...] = (acc_scr[:, :HEAD_DIM] / acc_scr[:, HEAD_DIM:]).astype(o_ref.dtype)


def _fox_attention(qkv, c, *, batch, seq, n_heads, tile):
    n = qkv.shape[0]
    tq = 2 * tile
    nq = seq // tq
    kern = functools.partial(_fox_attention_kernel, tile=tile)
    return pl.pallas_call(
        kern,
        grid=(batch, n_heads, nq),
        in_specs=[
            pl.BlockSpec((tq, HEAD_DIM), lambda b, h, i: (b * nq + i, h)),
            pl.BlockSpec((seq, HEAD_DIM), lambda b, h, i: (b, n_heads + h)),
            pl.BlockSpec((seq, HEAD_DIM), lambda b, h, i: (b, 2 * n_heads + h)),
            pl.BlockSpec((n_heads, seq), lambda b, h, i: (0, b)),
        ],
        out_specs=pl.BlockSpec((tq, HEAD_DIM), lambda b, h, i: (b * nq + i, h)),
        out_shape=jax.ShapeDtypeStruct((n, n_heads * HEAD_DIM), BF16),
        scratch_shapes=[pltpu.VMEM((tq, tile), F32),
                        pltpu.VMEM((tq, tile), F32),
                        pltpu.VMEM((tq, HEAD_DIM), F32),
                        pltpu.VMEM((tq, HEAD_DIM), F32),
                        pltpu.VMEM((seq, 2 * HEAD_DIM), BF16),
                        pltpu.VMEM((tq, HEAD_DIM), F32),
                        pltpu.VMEM((tq, 2 * HEAD_DIM), F32)],
        compiler_params=pltpu.CompilerParams(
            dimension_semantics=("arbitrary", "arbitrary", "arbitrary"),
            vmem_limit_bytes=VMEM_LIMIT_BYTES),
        name="fox_attention",
    )(qkv, qkv, qkv, c)


def _segment_permutation(ts):
    p = jnp.arange(ts)
    t = (p % SUBLANES) * (ts // SUBLANES) + p // SUBLANES
    return (t[:, None] == jnp.arange(ts)[None, :]).astype(BF16)


def _rg_lru_kernel(x_ref, g_ref, perm_ref, unperm_ref, cw_ref, cb_ref, wri_ref, br_ref,
                   bi_ref, lam_ref, gain_ref, y_ref,
                   xpad_scr, g_scr, a_scr, u_scr, h_scr, acum_scr, tail_scr, carry_scr,
                   *, ts):
    seg = ts // SUBLANES
    head = (CONV_WIDTH - 1) * SUBLANES
    d = x_ref.shape[1]

    @pl.when(pl.program_id(1) == 0)
    def _():
        tail_scr[...] = jnp.zeros(tail_scr.shape, F32)
        carry_scr[...] = jnp.zeros(carry_scr.shape, F32)

    perm = perm_ref[...]
    x = jnp.dot(perm, x_ref[...], preferred_element_type=F32)
    g_scr[...] = jnp.dot(perm, g_ref[...], preferred_element_type=F32)
    xpad_scr[head:head + ts, :] = x
    sub = lax.broadcasted_iota(jnp.int32, (SUBLANES, d), 0)
    for m in range(CONV_WIDTH - 1):
        grp = slice(m * SUBLANES, (m + 1) * SUBLANES)
        cur = x[ts - head + m * SUBLANES:ts - head + (m + 1) * SUBLANES, :]
        xpad_scr[grp, :] = jnp.where(sub == 0, pltpu.roll(tail_scr[grp, :], 1, axis=0),
                                     pltpu.roll(cur, 1, axis=0))
    tail_scr[...] = x[ts - head:ts, :]
    xc = cb_ref[...] + sum(
        cw_ref[j:j + 1, :] * xpad_scr[j * SUBLANES:j * SUBLANES + ts, :]
        for j in range(CONV_WIDTH))

    neg_lam = -lam_ref[...]
    k_unit = -LRU_C * (jnp.maximum(neg_lam, 0.0) + jnp.log1p(jnp.exp(-jnp.abs(neg_lam))))
    c1 = (0.5 * LOG2_E) * k_unit
    hb_r = 0.5 * br_ref[...]
    hb_i = 0.5 * bi_ref[...]
    xcb = xc.astype(BF16)
    for n in range(N_LRU_BLOCKS):
        cols = slice(n * LRU_BLOCK, (n + 1) * LRU_BLOCK)
        gates = jnp.dot(xcb[:, cols], wri_ref[n], preferred_element_type=F32)
        tr = jnp.tanh(gates[:, :LRU_BLOCK] + hb_r[:, cols])
        ti = jnp.tanh(gates[:, LRU_BLOCK:] + hb_i[:, cols])
        a = jnp.exp2(tr * c1[:, cols] + c1[:, cols])
        y2 = 1.0 - a * a
        a_scr[:, cols] = a
        u_scr[:, cols] = (y2 * lax.rsqrt(jnp.maximum(y2, TINY))) * (
            (ti + 1.0) * (0.5 * xc[:, cols]))

    def local(k, carry):
        h, acc = carry
        rows = pl.ds(pl.multiple_of(k * SUBLANES, SUBLANES), SUBLANES)
        a = a_scr[rows, :]
        h = a * h + u_scr[rows, :]
        acc = a * acc
        h_scr[rows, :] = h
        acum_scr[rows, :] = acc
        return h, acc

    h_end, a_end = lax.fori_loop(
        0, seg, local, (jnp.zeros((SUBLANES, d), F32), jnp.ones((SUBLANES, d), F32)),
        unroll=True)

    c = carry_scr[0:1, :]
    seg_in = [c]
    for r in range(SUBLANES):
        c = a_end[r:r + 1, :] * c + h_end[r:r + 1, :]
        seg_in.append(c)
    carry_scr[...] = jnp.broadcast_to(seg_in[SUBLANES], carry_scr.shape)
    c_in = jnp.concatenate(seg_in[:SUBLANES], axis=0)

    gain = gain_ref[...]

    def finish(k, carry):
        rows = pl.ds(pl.multiple_of(k * SUBLANES, SUBLANES), SUBLANES)
        h = h_scr[rows, :] + acum_scr[rows, :] * c_in
        hg = 0.5 * g_scr[rows, :]
        a_scr[rows, :] = (h * _rms_scale(h)) * gain * (hg * (jnp.tanh(hg) + 1.0))
        return carry

    lax.fori_loop(0, seg, finish, 0, unroll=True)
    y = jnp.dot(unperm_ref[...], a_scr[...].astype(BF16), preferred_element_type=F32)
    y_ref[...] = y.astype(y_ref.dtype)


def _rg_lru(x_lru, g_lru, conv_w, conv_b, w_ri, b_r, b_i, lam, gain, *, batch, seq, ts):
    n, d = x_lru.shape
    ns = seq // ts
    kern = functools.partial(_rg_lru_kernel, ts=ts)
    perm = _segment_permutation(ts)
    row = lambda b, s: (b * ns + s, 0)
    const2 = lambda b, s: (0, 0)
    vec = pl.BlockSpec((1, d), const2)
    mat = pl.BlockSpec((ts, ts), const2)
    return pl.pallas_call(
        kern,
        grid=(batch, ns),
        in_specs=[
            pl.BlockSpec((ts, d), row),
            pl.BlockSpec((ts, d), row),
            mat, mat,
            pl.BlockSpec((CONV_WIDTH, d), const2),
            vec,
            pl.BlockSpec(w_ri.shape, lambda b, s: (0, 0, 0)),
            vec, vec, vec, vec,
        ],
        out_specs=pl.BlockSpec((ts, d), row),
        out_shape=jax.ShapeDtypeStruct((n, d), BF16),
        scratch_shapes=[pltpu.VMEM((ts + (CONV_WIDTH - 1) * SUBLANES, d), F32),
                        pltpu.VMEM((ts, d), F32),
                        pltpu.VMEM((ts, d), F32),
                        pltpu.VMEM((ts, d), F32),
                        pltpu.VMEM((ts, d), F32),
                        pltpu.VMEM((ts, d), F32),
                        pltpu.VMEM(((CONV_WIDTH - 1) * SUBLANES, d), F32),
                        pltpu.VMEM((SUBLANES, d), F32)],
        compiler_params=pltpu.CompilerParams(
            dimension_semantics=("arbitrary", "arbitrary"),
            vmem_limit_bytes=VMEM_LIMIT_BYTES),
        name="rg_lru",
    )(x_lru, g_lru, perm, perm.T, conv_w, conv_b, w_ri, b_r, b_i, lam, gain)


def _out_proj_kernel(oa_ref, ga_ref, yl_ref, x_ref, p_ref, again_ref, wout_ref,
                     pgain_ref, wple_ref, plegain_ref, wpg_ref, bpg_ref, out_ref,
                     *, d_attn):
    oa = oa_ref[...].astype(F32)
    ga = ga_ref[...].astype(F32)
    ya = oa * _rms_scale(oa) * again_ref[...] * (ga * _sigmoid(ga))
    mix = jnp.dot(ya.astype(BF16), wout_ref[0:d_attn, :], preferred_element_type=F32)
    mix = mix + jnp.dot(yl_ref[...], wout_ref[d_attn:, :], preferred_element_type=F32)
    h1 = x_ref[...] + mix * _rms_scale(mix) * pgain_ref[...]
    e = jnp.dot(p_ref[...].astype(BF16), wple_ref[...], preferred_element_type=F32)
    e = e * _rms_scale(e) * plegain_ref[...]
    gate = _sigmoid(jnp.dot(h1.astype(BF16), wpg_ref[...], preferred_element_type=F32)
                    + bpg_ref[...])
    out_ref[...] = h1 + gate * e


def _out_proj(o_attn, g_attn, y_lru, x2d, p2d, attn_gain, w_out, post_gain, w_ple,
              ple_gain, w_pg, b_pg, *, tm):
    n, d = x2d.shape
    d_attn = o_attn.shape[1]
    d_lru = y_lru.shape[1]
    d_ple = p2d.shape[1]
    kern = functools.partial(_out_proj_kernel, d_attn=d_attn)
    row = lambda i: (i, 0)
    const = lambda i: (0, 0)
    single = pl.Buffered(1)
    return pl.pallas_call(
        kern,
        grid=(n // tm,),
        in_specs=[
            pl.BlockSpec((tm, d_attn), row),
            pl.BlockSpec((tm, d_attn), row),
            pl.BlockSpec((tm, d_lru), row),
            pl.BlockSpec((tm, d), row),
            pl.BlockSpec((tm, d_ple), row),
            pl.BlockSpec((1, d_attn), const),
            pl.BlockSpec(w_out.shape, const, pipeline_mode=single),
            pl.BlockSpec((1, d), const),
            pl.BlockSpec(w_ple.shape, const, pipeline_mode=single),
            pl.BlockSpec((1, d), const),
            pl.BlockSpec(w_pg.shape, const, pipeline_mode=single),
            pl.BlockSpec((1, d), const),
        ],
        out_specs=pl.BlockSpec((tm, d), row),
        out_shape=jax.ShapeDtypeStruct((n, d), F32),
        compiler_params=pltpu.CompilerParams(
            dimension_semantics=("arbitrary",),
            vmem_limit_bytes=VMEM_LIMIT_BYTES),
        name="out_proj",
    )(o_attn, g_attn, y_lru, x2d, p2d, attn_gain, w_out, post_gain, w_ple, ple_gain,
      w_pg, b_pg)


def _layer(h2d, p2d, w_in, b_f, pre_gain, post_gain, conv_w, conv_b, w_rgate, b_rgate,
           w_igate, b_igate, lru_lambda, attn_out_gain, lru_out_gain, w_out, w_ple,
           ple_gain, w_ple_gate, b_ple_gate, *, batch, seq):
    d = h2d.shape[1]
    n_heads = b_f.shape[0]
    d_attn = n_heads * HEAD_DIM
    assert w_in.shape[1] == 4 * d_attn + n_heads + 2 * d and d == d_attn
    fl_lo = 3 * d_attn
    w_main = jnp.concatenate([w_in[:, :fl_lo], w_in[:, fl_lo + n_heads:]],
                             axis=1).astype(BF16)
    w_f = jnp.pad(w_in[:, fl_lo:fl_lo + n_heads],
                  ((0, 0), (0, N_FORGET_PAD - n_heads))).astype(BF16)
    vec = lambda v: v.reshape(1, -1)

    qkv, g_attn, x_lru, g_lru, fl_t = _in_proj(
        h2d, vec(pre_gain), w_main, w_f, n_heads=n_heads, tm=512, chunk=512)
    c = _forget_cumsum(fl_t, b_f.reshape(n_heads, 1), seq=seq)
    o_attn = _fox_attention(qkv, c, batch=batch, seq=seq, n_heads=n_heads, tile=512)
    w_ri = (0.5 * jnp.concatenate([w_rgate, w_igate], axis=-1)).astype(BF16)
    y_lru = _rg_lru(x_lru, g_lru, conv_w, vec(conv_b), w_ri, vec(b_rgate), vec(b_igate),
                    vec(lru_lambda), vec(lru_out_gain), batch=batch, seq=seq, ts=256)
    return _out_proj(o_attn, g_attn, y_lru, h2d, p2d, vec(attn_out_gain),
                     w_out.astype(BF16), vec(post_gain), w_ple.astype(BF16),
                     vec(ple_gain), w_ple_gate.astype(BF16), vec(b_ple_gate), tm=512)


def kernel(x, p, w_in, b_f, pre_gain, post_gain, conv_w, conv_b, w_rgate, b_rgate,
           w_igate, b_igate, lru_lambda, attn_out_gain, lru_out_gain, w_out, w_ple,
           ple_gain, w_ple_gate, b_ple_gate):
    batch, seq, d = x.shape
    h = x.reshape(batch * seq, d)
    for i in range(w_in.shape[0]):
        h = _layer(h, p[i].reshape(batch * seq, -1), w_in[i], b_f[i], pre_gain[i],
                   post_gain[i], conv_w[i], conv_b[i], w_rgate[i], b_rgate[i],
                   w_igate[i], b_igate[i], lru_lambda[i], attn_out_gain[i],
                   lru_out_gain[i], w_out[i], w_ple[i], ple_gain[i], w_ple_gate[i],
                   b_ple_gate[i], batch=batch, seq=seq)
    return h.reshape(batch, seq, d)
```

```python
import functools
import math

import jax
import jax.numpy as jnp
from jax import lax
from jax.experimental import pallas as pl
from jax.experimental.pallas import tpu as pltpu

HEAD_DIM = 128
N_LRU_BLOCKS = 8
LRU_BLOCK = 128
CONV_WIDTH = 4
LRU_C = 8.0
RMS_EPS = 1e-6
N_FORGET_PAD = 128
SUBLANES = 8
VMEM_LIMIT_BYTES = 56 * 1024 * 1024
NEG_BIG = -1e30
TINY = 1e-30
LOG2_E = math.log2(math.e)

F32 = jnp.float32
BF16 = jnp.bfloat16


def _sigmoid(x):
    return 0.5 * jnp.tanh(0.5 * x) + 0.5


def _rms_scale(x):
    return lax.rsqrt(jnp.mean(x * x, axis=-1, keepdims=True) + RMS_EPS)


def _in_proj_kernel(x_ref, gain_ref, w_ref, wf_ref,
                    qkv_ref, gattn_ref, xlru_ref, glru_ref, flt_ref,
                    *, d_attn, chunk, q_scale):
    x = x_ref[...]
    xn = (x * _rms_scale(x) * gain_ref[...]).astype(BF16)
    n_main = w_ref.shape[1]
    dests = ((qkv_ref, 0, 3 * d_attn),
             (gattn_ref, 3 * d_attn, 4 * d_attn),
             (xlru_ref, 4 * d_attn, 5 * d_attn),
             (glru_ref, 5 * d_attn, 6 * d_attn))
    assert n_main == 6 * d_attn
    for dst, lo, hi in dests:
        for c0 in range(lo, hi, chunk):
            z = jnp.dot(xn, w_ref[:, c0:c0 + chunk], preferred_element_type=F32)
            if c0 < d_attn:
                z = z * q_scale
            dst[:, c0 - lo:c0 - lo + chunk] = z.astype(dst.dtype)
    fl = jnp.dot(xn, wf_ref[...], preferred_element_type=F32)
    flt_ref[...] = fl.T[:flt_ref.shape[0], :]


def _in_proj(x2d, pre_gain, w_main, w_f, *, n_heads, tm, chunk):
    n, d = x2d.shape
    d_attn = n_heads * HEAD_DIM
    kern = functools.partial(_in_proj_kernel, d_attn=d_attn, chunk=chunk,
                             q_scale=HEAD_DIM ** -0.5 * LOG2_E)
    const = lambda i: (0, 0)
    row = lambda i: (i, 0)
    return pl.pallas_call(
        kern,
        grid=(n // tm,),
        in_specs=[
            pl.BlockSpec((tm, d), row),
            pl.BlockSpec((1, d), const),
            pl.BlockSpec(w_main.shape, const, pipeline_mode=pl.Buffered(1)),
            pl.BlockSpec(w_f.shape, const, pipeline_mode=pl.Buffered(1)),
        ],
        out_specs=[
            pl.BlockSpec((tm, 3 * d_attn), row),
            pl.BlockSpec((tm, d_attn), row),
            pl.BlockSpec((tm, d_attn), row),
            pl.BlockSpec((tm, d_attn), row),
            pl.BlockSpec((n_heads, tm), lambda i: (0, i)),
        ],
        out_shape=[
            jax.ShapeDtypeStruct((n, 3 * d_attn), BF16),
            jax.ShapeDtypeStruct((n, d_attn), BF16),
            jax.ShapeDtypeStruct((n, d_attn), BF16),
            jax.ShapeDtypeStruct((n, d_attn), BF16),
            jax.ShapeDtypeStruct((n_heads, n), F32),
        ],
        compiler_params=pltpu.CompilerParams(
            dimension_semantics=("arbitrary",),
            vmem_limit_bytes=VMEM_LIMIT_BYTES),
        name="in_proj",
    )(x2d, pre_gain, w_main, w_f)


def _forget_cumsum_kernel(flt_ref, bf_ref, c_ref):
    z = flt_ref[...] + bf_ref[...]
    ls = jnp.minimum(z, 0.0) - jnp.log1p(jnp.exp(-jnp.abs(z)))
    seq = ls.shape[1]
    lane = lax.broadcasted_iota(jnp.int32, ls.shape, 1)
    shift = 1
    while shift < seq:
        ls = ls + jnp.where(lane >= shift, pltpu.roll(ls, shift, axis=1), 0.0)
        shift *= 2
    c_ref[...] = ls * LOG2_E


def _forget_cumsum(fl_t, b_f, *, seq):
    n_heads, n = fl_t.shape
    return pl.pallas_call(
        _forget_cumsum_kernel,
        grid=(n // seq,),
        in_specs=[pl.BlockSpec((n_heads, seq), lambda b: (0, b)),
                  pl.BlockSpec((n_heads, 1), lambda b: (0, 0))],
        out_specs=pl.BlockSpec((n_heads, seq), lambda b: (0, b)),
        out_shape=jax.ShapeDtypeStruct((n_heads, n), F32),
        compiler_params=pltpu.CompilerParams(dimension_semantics=("arbitrary",)),
        name="forget_cumsum",
    )(fl_t, b_f)


def _fox_attention_kernel(q_ref, k_ref, v_ref, c_ref, o_ref,
                          s0_scr, s1_scr, r0_scr, r1_scr, vext_scr, m_scr, acc_scr,
                          *, tile, heads_per_step):
    hg = pl.program_id(1)
    i = pl.program_id(2)
    n_chunks = tile // HEAD_DIM
    half_a = slice(0, tile)
    half_b = slice(tile, 2 * tile)
    both = slice(0, 2 * tile)

    @pl.when(i == 0)
    def _():
        for hh in range(heads_per_step):
            vext_scr[hh, :, :HEAD_DIM] = v_ref[:, hh * HEAD_DIM:(hh + 1) * HEAD_DIM]
            vext_scr[hh, :, HEAD_DIM:] = jnp.ones((vext_scr.shape[1], HEAD_DIM), BF16)

    def scores(hh, j, s_scr, r_scr, rows):
        start = pl.multiple_of(j * tile, tile)
        cols = slice(hh * HEAD_DIM, (hh + 1) * HEAD_DIM)
        s = lax.dot_general(q_ref[rows, cols], k_ref[pl.ds(start, tile), cols],
                            (((1,), (1,)), ((), ())), preferred_element_type=F32)
        s = s - c_ref[pl.ds(hg * heads_per_step + hh, 1), pl.ds(start, tile)]
        s_scr[rows, :] = s
        r_scr[rows, :] = jnp.broadcast_to(jnp.max(s, axis=1, keepdims=True),
                                          (s.shape[0], HEAD_DIM))

    def consume(hh, j, s_scr, r_scr, rows, diag):
        start = pl.multiple_of(j * tile, tile)
        m_prev = m_scr[rows, :]
        if diag:
            s = s_scr[rows, :]
            row = lax.broadcasted_iota(jnp.int32, s.shape, 0)
            col = lax.broadcasted_iota(jnp.int32, s.shape, 1)
            s = jnp.where(row >= col, s, NEG_BIG)
            m_new = jnp.maximum(m_prev, jnp.max(s, axis=1, keepdims=True))
            s_chunks = [s[:, c * HEAD_DIM:(c + 1) * HEAD_DIM] for c in range(n_chunks)]
        else:
            m_new = jnp.maximum(m_prev, r_scr[rows, :])
            s_chunks = [s_scr[rows, c * HEAD_DIM:(c + 1) * HEAD_DIM]
                        for c in range(n_chunks)]
        alpha = jnp.exp2(m_prev - m_new)
        pb = jnp.concatenate([jnp.exp2(sc - m_new).astype(BF16) for sc in s_chunks],
                             axis=1)
        pv = jnp.dot(pb, vext_scr[hh, pl.ds(start, tile), :],
                     preferred_element_type=F32)
        acc_scr[rows, :HEAD_DIM] = alpha * acc_scr[rows, :HEAD_DIM] + pv[:, :HEAD_DIM]
        acc_scr[rows, HEAD_DIM:] = alpha * acc_scr[rows, HEAD_DIM:] + pv[:, HEAD_DIM:]
        m_scr[rows, :] = m_new

    def finish(hh):
        consume(hh, 2 * i + 1, s1_scr, r1_scr, half_b, diag=True)
        o_ref[:, hh * HEAD_DIM:(hh + 1) * HEAD_DIM] = (
            acc_scr[:, :HEAD_DIM] / acc_scr[:, HEAD_DIM:]).astype(o_ref.dtype)

    for hh in range(heads_per_step):
        def pair(t, carry, hh=hh):
            j = 2 * t
            scores(hh, j + 1, s1_scr, r1_scr, both)
            consume(hh, j, s0_scr, r0_scr, both, diag=False)
            scores(hh, j + 2, s0_scr, r0_scr, both)
            consume(hh, j + 1, s1_scr, r1_scr, both, diag=False)
            return carry

        scores(hh, 0, s0_scr, r0_scr, both)
        if hh > 0:
            finish(hh - 1)
        m_scr[...] = jnp.full(m_scr.shape, NEG_BIG, F32)
        acc_scr[...] = jnp.zeros(acc_scr.shape, F32)
        lax.fori_loop(0, i, pair, 0)
        scores(hh, 2 * i + 1, s1_scr, r1_scr, half_b)
        consume(hh, 2 * i, s0_scr, r0_scr, half_a, diag=True)
        consume(hh, 2 * i, s0_scr, r0_scr, half_b, diag=False)
    finish(heads_per_step - 1)


def _fox_attention(qkv, c, *, batch, seq, n_heads, tile, heads_per_step):
    n = qkv.shape[0]
    tq = 2 * tile
    nq = seq // tq
    hp = heads_per_step
    w = hp * HEAD_DIM
    ng = n_heads // hp
    kern = functools.partial(_fox_attention_kernel, tile=tile, heads_per_step=hp)
    return pl.pallas_call(
        kern,
        grid=(batch, ng, nq),
        in_specs=[
            pl.BlockSpec((tq, w), lambda b, g, i: (b * nq + i, g)),
            pl.BlockSpec((seq, w), lambda b, g, i: (b, ng + g)),
            pl.BlockSpec((seq, w), lambda b, g, i: (b, 2 * ng + g)),
            pl.BlockSpec((n_heads, seq), lambda b, g, i: (0, b)),
        ],
        out_specs=pl.BlockSpec((tq, w), lambda b, g, i: (b * nq + i, g)),
        out_shape=jax.ShapeDtypeStruct((n, n_heads * HEAD_DIM), BF16),
        scratch_shapes=[pltpu.VMEM((tq, tile), F32),
                        pltpu.VMEM((tq, tile), F32),
                        pltpu.VMEM((tq, HEAD_DIM), F32),
                        pltpu.VMEM((tq, HEAD_DIM), F32),
                        pltpu.VMEM((hp, seq, 2 * HEAD_DIM), BF16),
                        pltpu.VMEM((tq, HEAD_DIM), F32),
                        pltpu.VMEM((tq, 2 * HEAD_DIM), F32)],
        compiler_params=pltpu.CompilerParams(
            dimension_semantics=("arbitrary", "arbitrary", "arbitrary"),
            vmem_limit_bytes=VMEM_LIMIT_BYTES),
        name="fox_attention",
    )(qkv, qkv, qkv, c)


def _segment_permutation(ts):
    p = jnp.arange(ts)
    t = (p % SUBLANES) * (ts // SUBLANES) + p // SUBLANES
    return (t[:, None] == jnp.arange(ts)[None, :]).astype(BF16)


def _rg_lru_kernel(x_ref, g_ref, perm_ref, unperm_ref, cw_ref, cb_ref, wri_ref, br_ref,
                   bi_ref, lam_ref, gain_ref, y_ref,
                   xpad_scr, g_scr, a_scr, u_scr, h_scr, acum_scr, tail_scr, carry_scr,
                   *, ts):
    seg = ts // SUBLANES
    head = (CONV_WIDTH - 1) * SUBLANES
    d = x_ref.shape[1]

    @pl.when(pl.program_id(1) == 0)
    def _():
        tail_scr[...] = jnp.zeros(tail_scr.shape, F32)
        carry_scr[...] = jnp.zeros(carry_scr.shape, F32)

    perm = perm_ref[...]
    x = jnp.dot(perm, x_ref[...], preferred_element_type=F32)
    g_scr[...] = jnp.dot(perm, g_ref[...], preferred_element_type=F32)
    xpad_scr[head:head + ts, :] = x
    sub = lax.broadcasted_iota(jnp.int32, (SUBLANES, d), 0)
    for m in range(CONV_WIDTH - 1):
        grp = slice(m * SUBLANES, (m + 1) * SUBLANES)
        cur = x[ts - head + m * SUBLANES:ts - head + (m + 1) * SUBLANES, :]
        xpad_scr[grp, :] = jnp.where(sub == 0, pltpu.roll(tail_scr[grp, :], 1, axis=0),
                                     pltpu.roll(cur, 1, axis=0))
    tail_scr[...] = x[ts - head:ts, :]
    xc = cb_ref[...] + sum(
        cw_ref[j:j + 1, :] * xpad_scr[j * SUBLANES:j * SUBLANES + ts, :]
        for j in range(CONV_WIDTH))

    neg_lam = -lam_ref[...]
    k_unit = -LRU_C * (jnp.maximum(neg_lam, 0.0) + jnp.log1p(jnp.exp(-jnp.abs(neg_lam))))
    c1 = (0.5 * LOG2_E) * k_unit
    hb_r = 0.5 * br_ref[...]
    hb_i = 0.5 * bi_ref[...]
    xcb = xc.astype(BF16)
    for n in range(N_LRU_BLOCKS):
        cols = slice(n * LRU_BLOCK, (n + 1) * LRU_BLOCK)
        gates = jnp.dot(xcb[:, cols], wri_ref[n], preferred_element_type=F32)
        tr = jnp.tanh(gates[:, :LRU_BLOCK] + hb_r[:, cols])
        ti = jnp.tanh(gates[:, LRU_BLOCK:] + hb_i[:, cols])
        a = jnp.exp2(tr * c1[:, cols] + c1[:, cols])
        y2 = 1.0 - a * a
        a_scr[:, cols] = a
        u_scr[:, cols] = (y2 * lax.rsqrt(jnp.maximum(y2, TINY))) * (
            (ti + 1.0) * (0.5 * xc[:, cols]))

    def local(k, carry):
        h, acc = carry
        rows = pl.ds(pl.multiple_of(k * SUBLANES, SUBLANES), SUBLANES)
        a = a_scr[rows, :]
        h = a * h + u_scr[rows, :]
        acc = a * acc
        h_scr[rows, :] = h
        acum_scr[rows, :] = acc
        return h, acc

    h_end, a_end = lax.fori_loop(
        0, seg, local, (jnp.zeros((SUBLANES, d), F32), jnp.ones((SUBLANES, d), F32)),
        unroll=True)

    c = carry_scr[0:1, :]
    seg_in = [c]
    for r in range(SUBLANES):
        c = a_end[r:r + 1, :] * c + h_end[r:r + 1, :]
        seg_in.append(c)
    carry_scr[...] = jnp.broadcast_to(seg_in[SUBLANES], carry_scr.shape)
    c_in = jnp.concatenate(seg_in[:SUBLANES], axis=0)

    gain = gain_ref[...]

    def finish(k, carry):
        rows = pl.ds(pl.multiple_of(k * SUBLANES, SUBLANES), SUBLANES)
        h = h_scr[rows, :] + acum_scr[rows, :] * c_in
        hg = 0.5 * g_scr[rows, :]
        a_scr[rows, :] = (h * _rms_scale(h)) * gain * (hg * (jnp.tanh(hg) + 1.0))
        return carry

    lax.fori_loop(0, seg, finish, 0, unroll=True)
    y = jnp.dot(unperm_ref[...], a_scr[...].astype(BF16), preferred_element_type=F32)
    y_ref[...] = y.astype(y_ref.dtype)


def _rg_lru(x_lru, g_lru, conv_w, conv_b, w_ri, b_r, b_i, lam, gain, *, batch, seq, ts):
    n, d = x_lru.shape
    ns = seq // ts
    kern = functools.partial(_rg_lru_kernel, ts=ts)
    perm = _segment_permutation(ts)
    row = lambda b, s: (b * ns + s, 0)
    const2 = lambda b, s: (0, 0)
    vec = pl.BlockSpec((1, d), const2)
    mat = pl.BlockSpec((ts, ts), const2)
    return pl.pallas_call(
        kern,
        grid=(batch, ns),
        in_specs=[
            pl.BlockSpec((ts, d), row),
            pl.BlockSpec((ts, d), row),
            mat, mat,
            pl.BlockSpec((CONV_WIDTH, d), const2),
            vec,
            pl.BlockSpec(w_ri.shape, lambda b, s: (0, 0, 0)),
            vec, vec, vec, vec,
        ],
        out_specs=pl.BlockSpec((ts, d), row),
        out_shape=jax.ShapeDtypeStruct((n, d), BF16),
        scratch_shapes=[pltpu.VMEM((ts + (CONV_WIDTH - 1) * SUBLANES, d), F32),
                        pltpu.VMEM((ts, d), F32),
                        pltpu.VMEM((ts, d), F32),
                        pltpu.VMEM((ts, d), F32),
                        pltpu.VMEM((ts, d), F32),
                        pltpu.VMEM((ts, d), F32),
                        pltpu.VMEM(((CONV_WIDTH - 1) * SUBLANES, d), F32),
                        pltpu.VMEM((SUBLANES, d), F32)],
        compiler_params=pltpu.CompilerParams(
            dimension_semantics=("arbitrary", "arbitrary"),
            vmem_limit_bytes=VMEM_LIMIT_BYTES),
        name="rg_lru",
    )(x_lru, g_lru, perm, perm.T, conv_w, conv_b, w_ri, b_r, b_i, lam, gain)


def _out_proj_kernel(oa_ref, ga_ref, yl_ref, x_ref, p_ref, again_ref, wout_ref,
                     pgain_ref, wple_ref, plegain_ref, wpg_ref, bpg_ref, out_ref,
                     *, d_attn):
    oa = oa_ref[...].astype(F32)
    ga = ga_ref[...].astype(F32)
    ya = oa * _rms_scale(oa) * again_ref[...] * (ga * _sigmoid(ga))
    mix = jnp.dot(ya.astype(BF16), wout_ref[0:d_attn, :], preferred_element_type=F32)
    mix = mix + jnp.dot(yl_ref[...], wout_ref[d_attn:, :], preferred_element_type=F32)
    h1 = x_ref[...] + mix * _rms_scale(mix) * pgain_ref[...]
    e = jnp.dot(p_ref[...].astype(BF16), wple_ref[...], preferred_element_type=F32)
    e = e * _rms_scale(e) * plegain_ref[...]
    gate = _sigmoid(jnp.dot(h1.astype(BF16), wpg_ref[...], preferred_element_type=F32)
                    + bpg_ref[...])
    out_ref[...] = h1 + gate * e


def _out_proj(o_attn, g_attn, y_lru, x2d, p2d, attn_gain, w_out, post_gain, w_ple,
              ple_gain, w_pg, b_pg, *, tm):
    n, d = x2d.shape
    d_attn = o_attn.shape[1]
    d_lru = y_lru.shape[1]
    d_ple = p2d.shape[1]
    kern = functools.partial(_out_proj_kernel, d_attn=d_attn)
    row = lambda i: (i, 0)
    const = lambda i: (0, 0)
    single = pl.Buffered(1)
    return pl.pallas_call(
        kern,
        grid=(n // tm,),
        in_specs=[
            pl.BlockSpec((tm, d_attn), row),
            pl.BlockSpec((tm, d_attn), row),
            pl.BlockSpec((tm, d_lru), row),
            pl.BlockSpec((tm, d), row),
            pl.BlockSpec((tm, d_ple), row),
            pl.BlockSpec((1, d_attn), const),
            pl.BlockSpec(w_out.shape, const, pipeline_mode=single),
            pl.BlockSpec((1, d), const),
            pl.BlockSpec(w_ple.shape, const, pipeline_mode=single),
            pl.BlockSpec((1, d), const),
            pl.BlockSpec(w_pg.shape, const, pipeline_mode=single),
            pl.BlockSpec((1, d), const),
        ],
        out_specs=pl.BlockSpec((tm, d), row),
        out_shape=jax.ShapeDtypeStruct((n, d), F32),
        compiler_params=pltpu.CompilerParams(
            dimension_semantics=("arbitrary",),
            vmem_limit_bytes=VMEM_LIMIT_BYTES),
        name="out_proj",
    )(o_attn, g_attn, y_lru, x2d, p2d, attn_gain, w_out, post_gain, w_ple, ple_gain,
      w_pg, b_pg)


def _layer(h2d, p2d, w_in, b_f, pre_gain, post_gain, conv_w, conv_b, w_rgate, b_rgate,
           w_igate, b_igate, lru_lambda, attn_out_gain, lru_out_gain, w_out, w_ple,
           ple_gain, w_ple_gate, b_ple_gate, *, batch, seq):
    d = h2d.shape[1]
    n_heads = b_f.shape[0]
    d_attn = n_heads * HEAD_DIM
    assert w_in.shape[1] == 4 * d_attn + n_heads + 2 * d and d == d_attn
    fl_lo = 3 * d_attn
    w_main = jnp.concatenate([w_in[:, :fl_lo], w_in[:, fl_lo + n_heads:]],
                             axis=1).astype(BF16)
    w_f = jnp.pad(w_in[:, fl_lo:fl_lo + n_heads],
                  ((0, 0), (0, N_FORGET_PAD - n_heads))).astype(BF16)
    vec = lambda v: v.reshape(1, -1)

    qkv, g_attn, x_lru, g_lru, fl_t = _in_proj(
        h2d, vec(pre_gain), w_main, w_f, n_heads=n_heads, tm=512, chunk=512)
    c = _forget_cumsum(fl_t, b_f.reshape(n_heads, 1), seq=seq)
    o_attn = _fox_attention(qkv, c, batch=batch, seq=seq, n_heads=n_heads, tile=512,
                            heads_per_step=2)
    w_ri = (0.5 * jnp.concatenate([w_rgate, w_igate], axis=-1)).astype(BF16)
    y_lru = _rg_lru(x_lru, g_lru, conv_w, vec(conv_b), w_ri, vec(b_rgate), vec(b_igate),
                    vec(lru_lambda), vec(lru_out_gain), batch=batch, seq=seq, ts=256)
    return _out_proj(o_attn, g_attn, y_lru, h2d, p2d, vec(attn_out_gain),
                     w_out.astype(BF16), vec(post_gain), w_ple.astype(BF16),
                     vec(ple_gain), w_ple_gate.astype(BF16), vec(b_ple_gate), tm=512)


def kernel(x, p, w_in, b_f, pre_gain, post_gain, conv_w, conv_b, w_rgate, b_rgate,
           w_igate, b_igate, lru_lambda, attn_out_gain, lru_out_gain, w_out, w_ple,
           ple_gain, w_ple_gate, b_ple_gate):
    batch, seq, d = x.shape
    h = x.reshape(batch * seq, d)
    for i in range(w_in.shape[0]):
        h = _layer(h, p[i].reshape(batch * seq, -1), w_in[i], b_f[i], pre_gain[i],
                   post_gain[i], conv_w[i], conv_b[i], w_rgate[i], b_rgate[i],
                   w_igate[i], b_igate[i], lru_lambda[i], attn_out_gain[i],
                   lru_out_gain[i], w_out[i], w_ple[i], ple_gain[i], w_ple_gate[i],
                   b_ple_gate[i], batch=batch, seq=seq)
    return h.reshape(batch, seq, d)
```

```python
import functools
import math

import jax
import jax.numpy as jnp
from jax import lax
from jax.experimental import pallas as pl
from jax.experimental.pallas import tpu as pltpu

HEAD_DIM = 128
N_LRU_BLOCKS = 8
LRU_BLOCK = 128
CONV_WIDTH = 4
LRU_C = 8.0
RMS_EPS = 1e-6
N_FORGET_PAD = 128
SUBLANES = 8
VMEM_LIMIT_BYTES = 56 * 1024 * 1024
NEG_BIG = -1e30
TINY = 1e-30
LOG2_E = math.log2(math.e)

F32 = jnp.float32
BF16 = jnp.bfloat16


def _sigmoid(x):
    return 0.5 * jnp.tanh(0.5 * x) + 0.5


def _rms_scale(x):
    return lax.rsqrt(jnp.mean(x * x, axis=-1, keepdims=True) + RMS_EPS)


def _in_proj_kernel(x_ref, gain_ref, wqkv_ref, wg_ref, wf_ref,
                    qkv_ref, gattn_ref, xlru_ref, glru_ref, flt_ref,
                    *, d_attn, chunk, q_scale):
    x = x_ref[...]
    xn = (x * _rms_scale(x) * gain_ref[...]).astype(BF16)
    dests = ((qkv_ref, wqkv_ref, 0, 3 * d_attn),
             (gattn_ref, wg_ref, 0, d_attn),
             (xlru_ref, wg_ref, d_attn, 2 * d_attn),
             (glru_ref, wg_ref, 2 * d_attn, 3 * d_attn))
    for dst, w_ref, lo, hi in dests:
        for c0 in range(lo, hi, chunk):
            z = jnp.dot(xn, w_ref[:, c0:c0 + chunk], preferred_element_type=F32)
            if w_ref is wqkv_ref and c0 < d_attn:
                z = z * q_scale
            dst[:, c0 - lo:c0 - lo + chunk] = z.astype(dst.dtype)
    fl = jnp.dot(xn, wf_ref[...], preferred_element_type=F32)
    flt_ref[...] = fl.T[:flt_ref.shape[0], :]


def _in_proj(x2d, pre_gain, w_qkv, w_g, w_f, *, n_heads, tm, chunk):
    n, d = x2d.shape
    d_attn = n_heads * HEAD_DIM
    kern = functools.partial(_in_proj_kernel, d_attn=d_attn, chunk=chunk,
                             q_scale=HEAD_DIM ** -0.5 * LOG2_E)
    const = lambda i: (0, 0)
    row = lambda i: (i, 0)
    single = pl.Buffered(1)
    return pl.pallas_call(
        kern,
        grid=(n // tm,),
        in_specs=[
            pl.BlockSpec((tm, d), row),
            pl.BlockSpec((1, d), const),
            pl.BlockSpec(w_qkv.shape, const, pipeline_mode=single),
            pl.BlockSpec(w_g.shape, const, pipeline_mode=single),
            pl.BlockSpec(w_f.shape, const, pipeline_mode=single),
        ],
        out_specs=[
            pl.BlockSpec((tm, 3 * d_attn), row),
            pl.BlockSpec((tm, d_attn), row),
            pl.BlockSpec((tm, d_attn), row),
            pl.BlockSpec((tm, d_attn), row),
            pl.BlockSpec((n_heads, tm), lambda i: (0, i)),
        ],
        out_shape=[
            jax.ShapeDtypeStruct((n, 3 * d_attn), BF16),
            jax.ShapeDtypeStruct((n, d_attn), BF16),
            jax.ShapeDtypeStruct((n, d_attn), BF16),
            jax.ShapeDtypeStruct((n, d_attn), BF16),
            jax.ShapeDtypeStruct((n_heads, n), F32),
        ],
        compiler_params=pltpu.CompilerParams(
            dimension_semantics=("arbitrary",),
            vmem_limit_bytes=VMEM_LIMIT_BYTES),
        name="in_proj",
    )(x2d, pre_gain, w_qkv, w_g, w_f)


def _forget_cumsum_kernel(flt_ref, bf_ref, c_ref):
    z = flt_ref[...] + bf_ref[...]
    ls = jnp.minimum(z, 0.0) - jnp.log1p(jnp.exp(-jnp.abs(z)))
    seq = ls.shape[1]
    lane = lax.broadcasted_iota(jnp.int32, ls.shape, 1)
    shift = 1
    while shift < seq:
        ls = ls + jnp.where(lane >= shift, pltpu.roll(ls, shift, axis=1), 0.0)
        shift *= 2
    c_ref[...] = ls * LOG2_E


def _forget_cumsum(fl_t, b_f, *, seq):
    n_heads, n = fl_t.shape
    return pl.pallas_call(
        _forget_cumsum_kernel,
        grid=(n // seq,),
        in_specs=[pl.BlockSpec((n_heads, seq), lambda b: (0, b)),
                  pl.BlockSpec((n_heads, 1), lambda b: (0, 0))],
        out_specs=pl.BlockSpec((n_heads, seq), lambda b: (0, b)),
        out_shape=jax.ShapeDtypeStruct((n_heads, n), F32),
        compiler_params=pltpu.CompilerParams(dimension_semantics=("arbitrary",)),
        name="forget_cumsum",
    )(fl_t, b_f)


def _fox_attention_kernel(q_ref, k_ref, v_ref, c_ref, o_ref,
                          s0_scr, s1_scr, r0_scr, r1_scr, vext_scr, m_scr, acc_scr,
                          *, tile, heads_per_step):
    hg = pl.program_id(1)
    i = pl.program_id(2)
    n_chunks = tile // HEAD_DIM
    half_a = slice(0, tile)
    half_b = slice(tile, 2 * tile)
    both = slice(0, 2 * tile)

    @pl.when(i == 0)
    def _():
        for hh in range(heads_per_step):
            vext_scr[hh, :, :HEAD_DIM] = v_ref[:, hh * HEAD_DIM:(hh + 1) * HEAD_DIM]
            vext_scr[hh, :, HEAD_DIM:] = jnp.ones((vext_scr.shape[1], HEAD_DIM), BF16)

    def scores(hh, j, s_scr, r_scr, rows):
        start = pl.multiple_of(j * tile, tile)
        cols = slice(hh * HEAD_DIM, (hh + 1) * HEAD_DIM)
        s = lax.dot_general(q_ref[rows, cols], k_ref[pl.ds(start, tile), cols],
                            (((1,), (1,)), ((), ())), preferred_element_type=F32)
        s = s - c_ref[pl.ds(hg * heads_per_step + hh, 1), pl.ds(start, tile)]
        s_scr[rows, :] = s
        r_scr[rows, :] = jnp.broadcast_to(jnp.max(s, axis=1, keepdims=True),
                                          (s.shape[0], HEAD_DIM))

    def consume(hh, j, s_scr, r_scr, rows, diag):
        start = pl.multiple_of(j * tile, tile)
        if diag:
            half = tile // 2
            parts = [(slice(rows.start, rows.start + half), half),
                     (slice(rows.start + half, rows.stop), tile)]
            row = lax.broadcasted_iota(jnp.int32, (half, HEAD_DIM), 0)
            lane = lax.broadcasted_iota(jnp.int32, (half, HEAD_DIM), 1)
        else:
            parts = [(rows, tile)]
        for prows, n_keys in parts:
            m_prev = m_scr[prows, :]
            s_chunks = [s_scr[prows, c * HEAD_DIM:(c + 1) * HEAD_DIM]
                        for c in range(n_keys // HEAD_DIM)]
            if diag:
                first = (n_keys - half) // HEAD_DIM
                for c in range(first, len(s_chunks)):
                    keep = row >= lane + (c - first) * HEAD_DIM
                    s_chunks[c] = jnp.where(keep, s_chunks[c], NEG_BIG)
                blk_max = functools.reduce(jnp.maximum, s_chunks)
                m_new = jnp.maximum(m_prev, jnp.max(blk_max, axis=1, keepdims=True))
            else:
                m_new = jnp.maximum(m_prev, r_scr[prows, :])
            alpha = jnp.exp2(m_prev - m_new)
            pb = jnp.concatenate([jnp.exp2(sc - m_new).astype(BF16) for sc in s_chunks],
                                 axis=1)
            pv = jnp.dot(pb, vext_scr[hh, pl.ds(start, n_keys), :],
                         preferred_element_type=F32)
            acc_scr[prows, :HEAD_DIM] = (alpha * acc_scr[prows, :HEAD_DIM]
                                         + pv[:, :HEAD_DIM])
            acc_scr[prows, HEAD_DIM:] = (alpha * acc_scr[prows, HEAD_DIM:]
                                         + pv[:, HEAD_DIM:])
            m_scr[prows, :] = m_new

    def finish(hh):
        consume(hh, 2 * i + 1, s1_scr, r1_scr, half_b, diag=True)
        o_ref[:, hh * HEAD_DIM:(hh + 1) * HEAD_DIM] = (
            acc_scr[:, :HEAD_DIM] / acc_scr[:, HEAD_DIM:]).astype(o_ref.dtype)

    for hh in range(heads_per_step):
        def pair(t, carry, hh=hh):
            j = 2 * t
            scores(hh, j + 1, s1_scr, r1_scr, both)
            consume(hh, j, s0_scr, r0_scr, both, diag=False)
            scores(hh, j + 2, s0_scr, r0_scr, both)
            consume(hh, j + 1, s1_scr, r1_scr, both, diag=False)
            return carry

        if hh == 0:
            m_scr[...] = jnp.full(m_scr.shape, NEG_BIG, F32)
            acc_scr[...] = jnp.zeros(acc_scr.shape, F32)
        scores(hh, 0, s0_scr, r0_scr, both)
        if hh > 0:
            finish(hh - 1)
            m_scr[...] = jnp.full(m_scr.shape, NEG_BIG, F32)
            acc_scr[...] = jnp.zeros(acc_scr.shape, F32)
        lax.fori_loop(0, i, pair, 0)
        scores(hh, 2 * i + 1, s1_scr, r1_scr, half_b)
        consume(hh, 2 * i, s0_scr, r0_scr, half_a, diag=True)
        consume(hh, 2 * i, s0_scr, r0_scr, half_b, diag=False)
    finish(heads_per_step - 1)


def _fox_attention(qkv, c, *, batch, seq, n_heads, tile, heads_per_step):
    n = qkv.shape[0]
    tq = 2 * tile
    nq = seq // tq
    hp = heads_per_step
    w = hp * HEAD_DIM
    ng = n_heads // hp
    kern = functools.partial(_fox_attention_kernel, tile=tile, heads_per_step=hp)
    return pl.pallas_call(
        kern,
        grid=(batch, ng, nq),
        in_specs=[
            pl.BlockSpec((tq, w), lambda b, g, i: (b * nq + i, g)),
            pl.BlockSpec((seq, w), lambda b, g, i: (b, ng + g)),
            pl.BlockSpec((seq, w), lambda b, g, i: (b, 2 * ng + g)),
            pl.BlockSpec((n_heads, seq), lambda b, g, i: (0, b)),
        ],
        out_specs=pl.BlockSpec((tq, w), lambda b, g, i: (b * nq + i, g)),
        out_shape=jax.ShapeDtypeStruct((n, n_heads * HEAD_DIM), BF16),
        scratch_shapes=[pltpu.VMEM((tq, tile), F32),
                        pltpu.VMEM((tq, tile), F32),
                        pltpu.VMEM((tq, HEAD_DIM), F32),
                        pltpu.VMEM((tq, HEAD_DIM), F32),
                        pltpu.VMEM((hp, seq, 2 * HEAD_DIM), BF16),
                        pltpu.VMEM((tq, HEAD_DIM), F32),
                        pltpu.VMEM((tq, 2 * HEAD_DIM), F32)],
        compiler_params=pltpu.CompilerParams(
            dimension_semantics=("arbitrary", "arbitrary", "arbitrary"),
            vmem_limit_bytes=VMEM_LIMIT_BYTES),
        name="fox_attention",
    )(qkv, qkv, qkv, c)


def _segment_permutation(ts):
    p = jnp.arange(ts)
    t = (p % SUBLANES) * (ts // SUBLANES) + p // SUBLANES
    return (t[:, None] == jnp.arange(ts)[None, :]).astype(BF16)


def _rg_lru_kernel(x_ref, g_ref, perm_ref, unperm_ref, cw_ref, cb_ref, wri_ref, br_ref,
                   bi_ref, lam_ref, gain_ref, y_ref,
                   xpad_scr, g_scr, a_scr, u_scr, h_scr, acum_scr, tail_scr, carry_scr,
                   *, ts):
    seg = ts // SUBLANES
    head = (CONV_WIDTH - 1) * SUBLANES
    d = x_ref.shape[1]

    @pl.when(pl.program_id(1) == 0)
    def _():
        tail_scr[...] = jnp.zeros(tail_scr.shape, F32)
        carry_scr[...] = jnp.zeros(carry_scr.shape, F32)

    perm = perm_ref[...]
    x = jnp.dot(perm, x_ref[...], preferred_element_type=F32)
    g_scr[...] = jnp.dot(perm, g_ref[...], preferred_element_type=F32)
    xpad_scr[head:head + ts, :] = x
    sub = lax.broadcasted_iota(jnp.int32, (SUBLANES, d), 0)
    for m in range(CONV_WIDTH - 1):
        grp = slice(m * SUBLANES, (m + 1) * SUBLANES)
        cur = x[ts - head + m * SUBLANES:ts - head + (m + 1) * SUBLANES, :]
        xpad_scr[grp, :] = jnp.where(sub == 0, pltpu.roll(tail_scr[grp, :], 1, axis=0),
                                     pltpu.roll(cur, 1, axis=0))
    tail_scr[...] = x[ts - head:ts, :]
    xc = cb_ref[...] + sum(
        cw_ref[j:j + 1, :] * xpad_scr[j * SUBLANES:j * SUBLANES + ts, :]
        for j in range(CONV_WIDTH))

    neg_lam = -lam_ref[...]
    k_unit = -LRU_C * (jnp.maximum(neg_lam, 0.0) + jnp.log1p(jnp.exp(-jnp.abs(neg_lam))))
    c1 = (0.5 * LOG2_E) * k_unit
    hb_r = 0.5 * br_ref[...]
    hb_i = 0.5 * bi_ref[...]
    xcb = xc.astype(BF16)
    for n in range(N_LRU_BLOCKS):
        cols = slice(n * LRU_BLOCK, (n + 1) * LRU_BLOCK)
        gates = jnp.dot(xcb[:, cols], wri_ref[n], preferred_element_type=F32)
        tr = jnp.tanh(gates[:, :LRU_BLOCK] + hb_r[:, cols])
        ti = jnp.tanh(gates[:, LRU_BLOCK:] + hb_i[:, cols])
        a = jnp.exp2(tr * c1[:, cols] + c1[:, cols])
        y2 = 1.0 - a * a
        a_scr[:, cols] = a
        u_scr[:, cols] = (y2 * lax.rsqrt(jnp.maximum(y2, TINY))) * (
            (ti + 1.0) * (0.5 * xc[:, cols]))

    def local(k, carry):
        h, acc = carry
        rows = pl.ds(pl.multiple_of(k * SUBLANES, SUBLANES), SUBLANES)
        a = a_scr[rows, :]
        h = a * h + u_scr[rows, :]
        acc = a * acc
        h_scr[rows, :] = h
        acum_scr[rows, :] = acc
        return h, acc

    h_end, a_end = lax.fori_loop(
        0, seg, local, (jnp.zeros((SUBLANES, d), F32), jnp.ones((SUBLANES, d), F32)),
        unroll=True)

    c = carry_scr[0:1, :]
    seg_in = [c]
    for r in range(SUBLANES):
        c = a_end[r:r + 1, :] * c + h_end[r:r + 1, :]
        seg_in.append(c)
    carry_scr[...] = jnp.broadcast_to(seg_in[SUBLANES], carry_scr.shape)
    c_in = jnp.concatenate(seg_in[:SUBLANES], axis=0)

    gain = gain_ref[...]

    def finish(k, carry):
        rows = pl.ds(pl.multiple_of(k * SUBLANES, SUBLANES), SUBLANES)
        h = h_scr[rows, :] + acum_scr[rows, :] * c_in
        hg = 0.5 * g_scr[rows, :]
        a_scr[rows, :] = (h * _rms_scale(h)) * gain * (hg * (jnp.tanh(hg) + 1.0))
        return carry

    lax.fori_loop(0, seg, finish, 0, unroll=True)
    y = jnp.dot(unperm_ref[...], a_scr[...].astype(BF16), preferred_element_type=F32)
    y_ref[...] = y.astype(y_ref.dtype)


def _rg_lru(x_lru, g_lru, conv_w, conv_b, w_ri, b_r, b_i, lam, gain, *, batch, seq, ts):
    n, d = x_lru.shape
    ns = seq // ts
    kern = functools.partial(_rg_lru_kernel, ts=ts)
    perm = _segment_permutation(ts)
    row = lambda b, s: (b * ns + s, 0)
    const2 = lambda b, s: (0, 0)
    vec = pl.BlockSpec((1, d), const2)
    mat = pl.BlockSpec((ts, ts), const2)
    return pl.pallas_call(
        kern,
        grid=(batch, ns),
        in_specs=[
            pl.BlockSpec((ts, d), row),
            pl.BlockSpec((ts, d), row),
            mat, mat,
            pl.BlockSpec((CONV_WIDTH, d), const2),
            vec,
            pl.BlockSpec(w_ri.shape, lambda b, s: (0, 0, 0)),
            vec, vec, vec, vec,
        ],
        out_specs=pl.BlockSpec((ts, d), row),
        out_shape=jax.ShapeDtypeStruct((n, d), BF16),
        scratch_shapes=[pltpu.VMEM((ts + (CONV_WIDTH - 1) * SUBLANES, d), F32),
                        pltpu.VMEM((ts, d), F32),
                        pltpu.VMEM((ts, d), F32),
                        pltpu.VMEM((ts, d), F32),
                        pltpu.VMEM((ts, d), F32),
                        pltpu.VMEM((ts, d), F32),
                        pltpu.VMEM(((CONV_WIDTH - 1) * SUBLANES, d), F32),
                        pltpu.VMEM((SUBLANES, d), F32)],
        compiler_params=pltpu.CompilerParams(
            dimension_semantics=("arbitrary", "arbitrary"),
            vmem_limit_bytes=VMEM_LIMIT_BYTES),
        name="rg_lru",
    )(x_lru, g_lru, perm, perm.T, conv_w, conv_b, w_ri, b_r, b_i, lam, gain)


def _out_proj_kernel(oa_ref, ga_ref, yl_ref, x_ref, p_ref, again_ref, wout_ref,
                     pgain_ref, wple_ref, plegain_ref, wpg_ref, bpg_ref, out_ref,
                     *, d_attn, sub_rows):
    subs = [slice(r0, r0 + sub_rows) for r0 in range(0, out_ref.shape[0], sub_rows)]

    def gated_attn(rows):
        oa = oa_ref[rows, :].astype(F32)
        hg = 0.5 * ga_ref[rows, :].astype(F32)
        ya = oa * _rms_scale(oa) * again_ref[...] * (hg * (jnp.tanh(hg) + 1.0))
        return ya.astype(BF16)

    def mixed(rows, ya):
        mix = jnp.dot(ya, wout_ref[0:d_attn, :], preferred_element_type=F32)
        return mix + jnp.dot(yl_ref[rows, :], wout_ref[d_attn:, :],
                             preferred_element_type=F32)

    def embed(rows):
        e = jnp.dot(p_ref[rows, :].astype(BF16), wple_ref[...], preferred_element_type=F32)
        return e * _rms_scale(e) * plegain_ref[...]

    def residual(rows, mix):
        return x_ref[rows, :] + mix * _rms_scale(mix) * pgain_ref[...]

    def gate_logits(h1):
        return jnp.dot(h1.astype(BF16), wpg_ref[...], preferred_element_type=F32)

    ya = [gated_attn(r) for r in subs]
    mix = [mixed(r, y) for r, y in zip(subs, ya)]
    e = [embed(r) for r in subs]
    h1 = [residual(r, m) for r, m in zip(subs, mix)]
    z = [gate_logits(h) for h in h1]
    for r, h, zz, ee in zip(subs, h1, z, e):
        out_ref[r, :] = h + _sigmoid(zz + bpg_ref[...]) * ee


def _out_proj(o_attn, g_attn, y_lru, x2d, p2d, attn_gain, w_out, post_gain, w_ple,
              ple_gain, w_pg, b_pg, *, tm, sub_rows):
    n, d = x2d.shape
    d_attn = o_attn.shape[1]
    d_lru = y_lru.shape[1]
    d_ple = p2d.shape[1]
    kern = functools.partial(_out_proj_kernel, d_attn=d_attn, sub_rows=sub_rows)
    row = lambda i: (i, 0)
    const = lambda i: (0, 0)
    single = pl.Buffered(1)
    return pl.pallas_call(
        kern,
        grid=(n // tm,),
        in_specs=[
            pl.BlockSpec((tm, d_attn), row),
            pl.BlockSpec((tm, d_attn), row),
            pl.BlockSpec((tm, d_lru), row),
            pl.BlockSpec((tm, d), row),
            pl.BlockSpec((tm, d_ple), row),
            pl.BlockSpec((1, d_attn), const),
            pl.BlockSpec(w_out.shape, const, pipeline_mode=single),
            pl.BlockSpec((1, d), const),
            pl.BlockSpec(w_ple.shape, const, pipeline_mode=single),
            pl.BlockSpec((1, d), const),
            pl.BlockSpec(w_pg.shape, const, pipeline_mode=single),
            pl.BlockSpec((1, d), const),
        ],
        out_specs=pl.BlockSpec((tm, d), row),
        out_shape=jax.ShapeDtypeStruct((n, d), F32),
        compiler_params=pltpu.CompilerParams(
            dimension_semantics=("arbitrary",),
            vmem_limit_bytes=VMEM_LIMIT_BYTES),
        name="out_proj",
    )(o_attn, g_attn, y_lru, x2d, p2d, attn_gain, w_out, post_gain, w_ple, ple_gain,
      w_pg, b_pg)


def _layer(h2d, p2d, w_in, b_f, pre_gain, post_gain, conv_w, conv_b, w_rgate, b_rgate,
           w_igate, b_igate, lru_lambda, attn_out_gain, lru_out_gain, w_out, w_ple,
           ple_gain, w_ple_gate, b_ple_gate, *, batch, seq):
    d = h2d.shape[1]
    n_heads = b_f.shape[0]
    d_attn = n_heads * HEAD_DIM
    assert w_in.shape[1] == 4 * d_attn + n_heads + 2 * d and d == d_attn
    fl_lo = 3 * d_attn
    w_qkv = w_in[:, :fl_lo].astype(BF16)
    w_g = w_in[:, fl_lo + n_heads:].astype(BF16)
    w_f = jnp.pad(w_in[:, fl_lo:fl_lo + n_heads],
                  ((0, 0), (0, N_FORGET_PAD - n_heads))).astype(BF16)
    vec = lambda v: v.reshape(1, -1)

    qkv, g_attn, x_lru, g_lru, fl_t = _in_proj(
        h2d, vec(pre_gain), w_qkv, w_g, w_f, n_heads=n_heads, tm=512, chunk=512)
    c = _forget_cumsum(fl_t, b_f.reshape(n_heads, 1), seq=seq)
    o_attn = _fox_attention(qkv, c, batch=batch, seq=seq, n_heads=n_heads, tile=512,
                            heads_per_step=2)
    w_ri = (0.5 * jnp.concatenate([w_rgate, w_igate], axis=-1)).astype(BF16)
    y_lru = _rg_lru(x_lru, g_lru, conv_w, vec(conv_b), w_ri, vec(b_rgate), vec(b_igate),
                    vec(lru_lambda), vec(lru_out_gain), batch=batch, seq=seq, ts=256)
    return _out_proj(o_attn, g_attn, y_lru, h2d, p2d, vec(attn_out_gain),
                     w_out.astype(BF16), vec(post_gain), w_ple.astype(BF16),
                     vec(ple_gain), w_ple_gate.astype(BF16), vec(b_ple_gate), tm=1024,
                     sub_rows=512)


def kernel(x, p, w_in, b_f, pre_gain, post_gain, conv_w, conv_b, w_rgate, b_rgate,
           w_igate, b_igate, lru_lambda, attn_out_gain, lru_out_gain, w_out, w_ple,
           ple_gain, w_ple_gate, b_ple_gate):
    batch, seq, d = x.shape
    h = x.reshape(batch * seq, d)
    for i in range(w_in.shape[0]):
        h = _layer(h, p[i].reshape(batch * seq, -1), w_in[i], b_f[i], pre_gain[i],
                   post_gain[i], conv_w[i], conv_b[i], w_rgate[i], b_rgate[i],
                   w_igate[i], b_igate[i], lru_lambda[i], attn_out_gain[i],
                   lru_out_gain[i], w_out[i], w_ple[i], ple_gain[i], w_ple_gate[i],
                   b_ple_gate[i], batch=batch, seq=seq)
    return h.reshape(batch, seq, d)
```

```python
import functools
import math

import jax
import jax.numpy as jnp
from jax import lax
from jax.experimental import pallas as pl
from jax.experimental.pallas import tpu as pltpu

HEAD_DIM = 128
N_LRU_BLOCKS = 8
LRU_BLOCK = 128
CONV_WIDTH = 4
LRU_C = 8.0
RMS_EPS = 1e-6
N_FORGET_PAD = 128
SUBLANES = 8
VMEM_LIMIT_BYTES = 56 * 1024 * 1024
NEG_BIG = -1e30
TINY = 1e-30
LOG2_E = math.log2(math.e)

F32 = jnp.float32
BF16 = jnp.bfloat16


def _sigmoid(x):
    return 0.5 * jnp.tanh(0.5 * x) + 0.5


def _rms_scale(x):
    return lax.rsqrt(jnp.mean(x * x, axis=-1, keepdims=True) + RMS_EPS)


def _segment_permutation(ts):
    p = jnp.arange(ts)
    t = (p % SUBLANES) * (ts // SUBLANES) + p // SUBLANES
    return (t[:, None] == jnp.arange(ts)[None, :]).astype(BF16)


def _in_proj_lru_kernel(x_ref, gain_ref, wqkv_ref, wg_ref, wf_ref, perm_ref, unperm_ref,
                        cw_ref, cb_ref, wri_ref, br_ref, bi_ref, lam_ref, lgain_ref,
                        qkv_ref, gattn_ref, ylru_ref, flt_ref,
                        xlru_scr, glru_scr, xpad_scr, a_scr, u_scr, h_scr, acum_scr,
                        tail_scr, carry_scr,
                        *, d_attn, chunk, q_scale, ts, tiles_per_seq):
    tm, d = x_ref.shape
    n_sub = tm // ts
    seg = ts // SUBLANES
    head = (CONV_WIDTH - 1) * SUBLANES

    @pl.when(pl.program_id(0) % tiles_per_seq == 0)
    def _():
        tail_scr[...] = jnp.zeros(tail_scr.shape, F32)
        carry_scr[...] = jnp.zeros(carry_scr.shape, F32)

    x = x_ref[...]
    xn = (x * _rms_scale(x) * gain_ref[...]).astype(BF16)
    perm = perm_ref[...]
    xn_seg = jnp.concatenate(
        [jnp.dot(perm, xn[s * ts:(s + 1) * ts, :], preferred_element_type=F32).astype(BF16)
         for s in range(n_sub)], axis=0)

    for dst, lo in ((xlru_scr, d_attn), (glru_scr, 2 * d_attn)):
        for c0 in range(0, d_attn, chunk):
            dst[:, c0:c0 + chunk] = jnp.dot(xn_seg, wg_ref[:, lo + c0:lo + c0 + chunk],
                                            preferred_element_type=F32)

    neg_lam = -lam_ref[...]
    k_unit = -LRU_C * (jnp.maximum(neg_lam, 0.0) + jnp.log1p(jnp.exp(-jnp.abs(neg_lam))))
    c1 = (0.5 * LOG2_E) * k_unit
    hb_r = 0.5 * br_ref[...]
    hb_i = 0.5 * bi_ref[...]
    sub = lax.broadcasted_iota(jnp.int32, (SUBLANES, d), 0)

    def lru_gates(s):
        base = s * ts
        x_s = xlru_scr[base:base + ts, :]
        pad0 = s * (ts + head)
        xpad_scr[pad0 + head:pad0 + head + ts, :] = x_s
        for m in range(CONV_WIDTH - 1):
            grp = slice(m * SUBLANES, (m + 1) * SUBLANES)
            cur = x_s[ts - head + m * SUBLANES:ts - head + (m + 1) * SUBLANES, :]
            xpad_scr[pad0 + m * SUBLANES:pad0 + (m + 1) * SUBLANES, :] = jnp.where(
                sub == 0, pltpu.roll(tail_scr[grp, :], 1, axis=0),
                pltpu.roll(cur, 1, axis=0))
        tail_scr[...] = x_s[ts - head:ts, :]
        xc = cb_ref[...] + sum(
            cw_ref[j:j + 1, :]
            * xpad_scr[pad0 + j * SUBLANES:pad0 + j * SUBLANES + ts, :]
            for j in range(CONV_WIDTH))
        xcb = xc.astype(BF16)
        for n in range(N_LRU_BLOCKS):
            cols = slice(n * LRU_BLOCK, (n + 1) * LRU_BLOCK)
            gates = jnp.dot(xcb[:, cols], wri_ref[n], preferred_element_type=F32)
            tr = jnp.tanh(gates[:, :LRU_BLOCK] + hb_r[:, cols])
            ti = jnp.tanh(gates[:, LRU_BLOCK:] + hb_i[:, cols])
            a = jnp.exp2(tr * c1[:, cols] + c1[:, cols])
            y2 = 1.0 - a * a
            a_scr[base:base + ts, cols] = a
            u_scr[base:base + ts, cols] = (y2 * lax.rsqrt(jnp.maximum(y2, TINY))) * (
                (ti + 1.0) * (0.5 * xc[:, cols]))

    def lru_scan(s):
        base = s * ts
        h = jnp.zeros((SUBLANES, d), F32)
        acc = jnp.ones((SUBLANES, d), F32)
        for k in range(seg):
            rows = slice(base + k * SUBLANES, base + (k + 1) * SUBLANES)
            a = a_scr[rows, :]
            h = a * h + u_scr[rows, :]
            acc = a * acc
            h_scr[rows, :] = h
            acum_scr[rows, :] = acc
        c = carry_scr[0:1, :]
        seg_in = [c]
        for r in range(SUBLANES):
            c = acc[r:r + 1, :] * c + h[r:r + 1, :]
            seg_in.append(c)
        carry_scr[...] = jnp.broadcast_to(seg_in[SUBLANES], carry_scr.shape)
        return jnp.concatenate(seg_in[:SUBLANES], axis=0)

    def lru_finish(s, c_in):
        base = s * ts
        gain = lgain_ref[...]
        for k in range(seg):
            rows = slice(base + k * SUBLANES, base + (k + 1) * SUBLANES)
            h = h_scr[rows, :] + acum_scr[rows, :] * c_in
            hg = 0.5 * glru_scr[rows, :]
            a_scr[rows, :] = (h * _rms_scale(h)) * gain * (hg * (jnp.tanh(hg) + 1.0))
        y = jnp.dot(unperm_ref[...], a_scr[base:base + ts, :].astype(BF16),
                    preferred_element_type=F32)
        ylru_ref[base:base + ts, :] = y.astype(ylru_ref.dtype)

    def project(dst, w_ref, w_lo, c0, scale):
        z = jnp.dot(xn, w_ref[:, w_lo + c0:w_lo + c0 + chunk], preferred_element_type=F32)
        if scale is not None:
            z = z * scale
        dst[:, c0:c0 + chunk] = z.astype(dst.dtype)

    chunks = [(qkv_ref, wqkv_ref, 0, c0, q_scale if c0 < d_attn else None)
              for c0 in range(0, 3 * d_attn, chunk)]
    chunks += [(gattn_ref, wg_ref, 0, c0, None) for c0 in range(0, d_attn, chunk)]
    lru_steps = []
    carried = {}
    for s in range(n_sub):
        lru_steps.append(lambda s=s: lru_gates(s))
    for s in range(n_sub):
        lru_steps.append(lambda s=s: carried.__setitem__(s, lru_scan(s)))
        lru_steps.append(lambda s=s: lru_finish(s, carried[s]))
    for n, args in enumerate(chunks):
        project(*args)
        if n < len(lru_steps):
            lru_steps[n]()
    for step in lru_steps[len(chunks):]:
        step()
    fl = jnp.dot(xn, wf_ref[...], preferred_element_type=F32)
    flt_ref[...] = fl.T[:flt_ref.shape[0], :]


def _in_proj_lru(x2d, pre_gain, w_qkv, w_g, w_f, conv_w, conv_b, w_ri, b_r, b_i, lam,
                 lru_gain, *, n_heads, seq, tm, chunk, ts):
    n, d = x2d.shape
    d_attn = n_heads * HEAD_DIM
    n_sub = tm // ts
    kern = functools.partial(_in_proj_lru_kernel, d_attn=d_attn, chunk=chunk,
                             q_scale=HEAD_DIM ** -0.5 * LOG2_E, ts=ts,
                             tiles_per_seq=seq // tm)
    perm = _segment_permutation(ts)
    const = lambda i: (0, 0)
    row = lambda i: (i, 0)
    single = pl.Buffered(1)
    vec = pl.BlockSpec((1, d), const)
    mat = pl.BlockSpec((ts, ts), const)
    head = (CONV_WIDTH - 1) * SUBLANES
    return pl.pallas_call(
        kern,
        grid=(n // tm,),
        in_specs=[
            pl.BlockSpec((tm, d), row),
            vec,
            pl.BlockSpec(w_qkv.shape, const, pipeline_mode=single),
            pl.BlockSpec(w_g.shape, const, pipeline_mode=single),
            pl.BlockSpec(w_f.shape, const, pipeline_mode=single),
            mat, mat,
            pl.BlockSpec((CONV_WIDTH, d), const),
            vec,
            pl.BlockSpec(w_ri.shape, lambda i: (0, 0, 0)),
            vec, vec, vec, vec,
        ],
        out_specs=[
            pl.BlockSpec((tm, 3 * d_attn), row),
            pl.BlockSpec((tm, d_attn), row),
            pl.BlockSpec((tm, d), row),
            pl.BlockSpec((n_heads, tm), lambda i: (0, i)),
        ],
        out_shape=[
            jax.ShapeDtypeStruct((n, 3 * d_attn), BF16),
            jax.ShapeDtypeStruct((n, d_attn), BF16),
            jax.ShapeDtypeStruct((n, d), BF16),
            jax.ShapeDtypeStruct((n_heads, n), F32),
        ],
        scratch_shapes=[pltpu.VMEM((tm, d), F32),
                        pltpu.VMEM((tm, d), F32),
                        pltpu.VMEM((n_sub * (ts + head), d), F32),
                        pltpu.VMEM((tm, d), F32),
                        pltpu.VMEM((tm, d), F32),
                        pltpu.VMEM((tm, d), F32),
                        pltpu.VMEM((tm, d), F32),
                        pltpu.VMEM((head, d), F32),
                        pltpu.VMEM((SUBLANES, d), F32)],
        compiler_params=pltpu.CompilerParams(
            dimension_semantics=("arbitrary",),
            vmem_limit_bytes=VMEM_LIMIT_BYTES),
        name="in_proj_lru",
    )(x2d, pre_gain, w_qkv, w_g, w_f, perm, perm.T, conv_w, conv_b, w_ri, b_r, b_i, lam,
      lru_gain)


def _forget_cumsum_kernel(flt_ref, bf_ref, c_ref):
    z = flt_ref[...] + bf_ref[...]
    ls = jnp.minimum(z, 0.0) - jnp.log1p(jnp.exp(-jnp.abs(z)))
    seq = ls.shape[1]
    lane = lax.broadcasted_iota(jnp.int32, ls.shape, 1)
    shift = 1
    while shift < seq:
        ls = ls + jnp.where(lane >= shift, pltpu.roll(ls, shift, axis=1), 0.0)
        shift *= 2
    c_ref[...] = ls * LOG2_E


def _forget_cumsum(fl_t, b_f, *, seq):
    n_heads, n = fl_t.shape
    return pl.pallas_call(
        _forget_cumsum_kernel,
        grid=(n // seq,),
        in_specs=[pl.BlockSpec((n_heads, seq), lambda b: (0, b)),
                  pl.BlockSpec((n_heads, 1), lambda b: (0, 0))],
        out_specs=pl.BlockSpec((n_heads, seq), lambda b: (0, b)),
        out_shape=jax.ShapeDtypeStruct((n_heads, n), F32),
        compiler_params=pltpu.CompilerParams(dimension_semantics=("arbitrary",)),
        name="forget_cumsum",
    )(fl_t, b_f)


def _fox_attention_kernel(q_ref, k_ref, v_ref, c_ref, o_ref,
                          s0_scr, s1_scr, r0_scr, r1_scr, vext_scr, m_scr, acc_scr,
                          *, tile, heads_per_step):
    hg = pl.program_id(1)
    i = pl.program_id(2)
    half_a = slice(0, tile)
    half_b = slice(tile, 2 * tile)
    both = slice(0, 2 * tile)

    @pl.when(i == 0)
    def _():
        for hh in range(heads_per_step):
            vext_scr[hh, :, :HEAD_DIM] = v_ref[:, hh * HEAD_DIM:(hh + 1) * HEAD_DIM]
            vext_scr[hh, :, HEAD_DIM:] = jnp.ones((vext_scr.shape[1], HEAD_DIM), BF16)

    def scores(hh, j, s_scr, r_scr, rows):
        start = pl.multiple_of(j * tile, tile)
        cols = slice(hh * HEAD_DIM, (hh + 1) * HEAD_DIM)
        s = lax.dot_general(q_ref[rows, cols], k_ref[pl.ds(start, tile), cols],
                            (((1,), (1,)), ((), ())), preferred_element_type=F32)
        s = s - c_ref[pl.ds(hg * heads_per_step + hh, 1), pl.ds(start, tile)]
        s_scr[rows, :] = s
        r_scr[rows, :] = jnp.broadcast_to(jnp.max(s, axis=1, keepdims=True),
                                          (s.shape[0], HEAD_DIM))

    def consume(hh, j, s_scr, r_scr, rows, diag):
        start = pl.multiple_of(j * tile, tile)
        if diag:
            half = tile // 2
            parts = [(slice(rows.start, rows.start + half), half),
                     (slice(rows.start + half, rows.stop), tile)]
            row = lax.broadcasted_iota(jnp.int32, (half, HEAD_DIM), 0)
            lane = lax.broadcasted_iota(jnp.int32, (half, HEAD_DIM), 1)
        else:
            parts = [(rows, tile)]
        for prows, n_keys in parts:
            m_prev = m_scr[prows, :]
            s_chunks = [s_scr[prows, c * HEAD_DIM:(c + 1) * HEAD_DIM]
                        for c in range(n_keys // HEAD_DIM)]
            if diag:
                first = (n_keys - half) // HEAD_DIM
                for c in range(first, len(s_chunks)):
                    keep = row >= lane + (c - first) * HEAD_DIM
                    s_chunks[c] = jnp.where(keep, s_chunks[c], NEG_BIG)
                blk_max = functools.reduce(jnp.maximum, s_chunks)
                m_new = jnp.maximum(m_prev, jnp.max(blk_max, axis=1, keepdims=True))
            else:
                m_new = jnp.maximum(m_prev, r_scr[prows, :])
            alpha = jnp.exp2(m_prev - m_new)
            pb = jnp.concatenate([jnp.exp2(sc - m_new).astype(BF16) for sc in s_chunks],
                                 axis=1)
            pv = jnp.dot(pb, vext_scr[hh, pl.ds(start, n_keys), :],
                         preferred_element_type=F32)
            acc_scr[prows, :HEAD_DIM] = (alpha * acc_scr[prows, :HEAD_DIM]
                                         + pv[:, :HEAD_DIM])
            acc_scr[prows, HEAD_DIM:] = (alpha * acc_scr[prows, HEAD_DIM:]
                                         + pv[:, HEAD_DIM:])
            m_scr[prows, :] = m_new

    def finish(hh):
        consume(hh, 2 * i + 1, s1_scr, r1_scr, half_b, diag=True)
        o_ref[:, hh * HEAD_DIM:(hh + 1) * HEAD_DIM] = (
            acc_scr[:, :HEAD_DIM] / acc_scr[:, HEAD_DIM:]).astype(o_ref.dtype)

    for hh in range(heads_per_step):
        def pair(t, carry, hh=hh):
            j = 2 * t
            scores(hh, j + 1, s1_scr, r1_scr, both)
            consume(hh, j, s0_scr, r0_scr, both, diag=False)
            scores(hh, j + 2, s0_scr, r0_scr, both)
            consume(hh, j + 1, s1_scr, r1_scr, both, diag=False)
            return carry

        if hh == 0:
            m_scr[...] = jnp.full(m_scr.shape, NEG_BIG, F32)
            acc_scr[...] = jnp.zeros(acc_scr.shape, F32)
        scores(hh, 0, s0_scr, r0_scr, both)
        if hh > 0:
            finish(hh - 1)
            m_scr[...] = jnp.full(m_scr.shape, NEG_BIG, F32)
            acc_scr[...] = jnp.zeros(acc_scr.shape, F32)
        lax.fori_loop(0, i, pair, 0)
        scores(hh, 2 * i + 1, s1_scr, r1_scr, half_b)
        consume(hh, 2 * i, s0_scr, r0_scr, half_a, diag=True)
        consume(hh, 2 * i, s0_scr, r0_scr, half_b, diag=False)
    finish(heads_per_step - 1)


def _fox_attention(qkv, c, *, batch, seq, n_heads, tile, heads_per_step):
    n = qkv.shape[0]
    tq = 2 * tile
    nq = seq // tq
    hp = heads_per_step
    w = hp * HEAD_DIM
    ng = n_heads // hp
    kern = functools.partial(_fox_attention_kernel, tile=tile, heads_per_step=hp)
    return pl.pallas_call(
        kern,
        grid=(batch, ng, nq),
        in_specs=[
            pl.BlockSpec((tq, w), lambda b, g, i: (b * nq + i, g)),
            pl.BlockSpec((seq, w), lambda b, g, i: (b, ng + g)),
            pl.BlockSpec((seq, w), lambda b, g, i: (b, 2 * ng + g)),
            pl.BlockSpec((n_heads, seq), lambda b, g, i: (0, b)),
        ],
        out_specs=pl.BlockSpec((tq, w), lambda b, g, i: (b * nq + i, g)),
        out_shape=jax.ShapeDtypeStruct((n, n_heads * HEAD_DIM), BF16),
        scratch_shapes=[pltpu.VMEM((tq, tile), F32),
                        pltpu.VMEM((tq, tile), F32),
                        pltpu.VMEM((tq, HEAD_DIM), F32),
                        pltpu.VMEM((tq, HEAD_DIM), F32),
                        pltpu.VMEM((hp, seq, 2 * HEAD_DIM), BF16),
                        pltpu.VMEM((tq, HEAD_DIM), F32),
                        pltpu.VMEM((tq, 2 * HEAD_DIM), F32)],
        compiler_params=pltpu.CompilerParams(
            dimension_semantics=("arbitrary", "arbitrary", "arbitrary"),
            vmem_limit_bytes=VMEM_LIMIT_BYTES),
        name="fox_attention",
    )(qkv, qkv, qkv, c)


def _out_proj_kernel(oa_ref, ga_ref, yl_ref, x_ref, p_ref, again_ref, wout_ref,
                     pgain_ref, wple_ref, plegain_ref, wpg_ref, bpg_ref, out_ref,
                     *, d_attn, sub_rows):
    subs = [slice(r0, r0 + sub_rows) for r0 in range(0, out_ref.shape[0], sub_rows)]

    def gated_attn(rows):
        oa = oa_ref[rows, :].astype(F32)
        hg = 0.5 * ga_ref[rows, :].astype(F32)
        ya = oa * _rms_scale(oa) * again_ref[...] * (hg * (jnp.tanh(hg) + 1.0))
        return ya.astype(BF16)

    def mixed(rows, ya):
        mix = jnp.dot(ya, wout_ref[0:d_attn, :], preferred_element_type=F32)
        return mix + jnp.dot(yl_ref[rows, :], wout_ref[d_attn:, :],
                             preferred_element_type=F32)

    def embed(rows):
        e = jnp.dot(p_ref[rows, :].astype(BF16), wple_ref[...], preferred_element_type=F32)
        return e * _rms_scale(e) * plegain_ref[...]

    def residual(rows, mix):
        return x_ref[rows, :] + mix * _rms_scale(mix) * pgain_ref[...]

    def gate_logits(h1):
        return jnp.dot(h1.astype(BF16), wpg_ref[...], preferred_element_type=F32)

    ya = [gated_attn(r) for r in subs]
    mix = [mixed(r, y) for r, y in zip(subs, ya)]
    e = [embed(r) for r in subs]
    h1 = [residual(r, m) for r, m in zip(subs, mix)]
    z = [gate_logits(h) for h in h1]
    for r, h, zz, ee in zip(subs, h1, z, e):
        out_ref[r, :] = h + _sigmoid(zz + bpg_ref[...]) * ee


def _out_proj(o_attn, g_attn, y_lru, x2d, p2d, attn_gain, w_out, post_gain, w_ple,
              ple_gain, w_pg, b_pg, *, tm, sub_rows):
    n, d = x2d.shape
    d_attn = o_attn.shape[1]
    d_lru = y_lru.shape[1]
    d_ple = p2d.shape[1]
    kern = functools.partial(_out_proj_kernel, d_attn=d_attn, sub_rows=sub_rows)
    row = lambda i: (i, 0)
    const = lambda i: (0, 0)
    single = pl.Buffered(1)
    return pl.pallas_call(
        kern,
        grid=(n // tm,),
        in_specs=[
            pl.BlockSpec((tm, d_attn), row),
            pl.BlockSpec((tm, d_attn), row),
            pl.BlockSpec((tm, d_lru), row),
            pl.BlockSpec((tm, d), row),
            pl.BlockSpec((tm, d_ple), row),
            pl.BlockSpec((1, d_attn), const),
            pl.BlockSpec(w_out.shape, const, pipeline_mode=single),
            pl.BlockSpec((1, d), const),
            pl.BlockSpec(w_ple.shape, const, pipeline_mode=single),
            pl.BlockSpec((1, d), const),
            pl.BlockSpec(w_pg.shape, const, pipeline_mode=single),
            pl.BlockSpec((1, d), const),
        ],
        out_specs=pl.BlockSpec((tm, d), row),
        out_shape=jax.ShapeDtypeStruct((n, d), F32),
        compiler_params=pltpu.CompilerParams(
            dimension_semantics=("arbitrary",),
            vmem_limit_bytes=VMEM_LIMIT_BYTES),
        name="out_proj",
    )(o_attn, g_attn, y_lru, x2d, p2d, attn_gain, w_out, post_gain, w_ple, ple_gain,
      w_pg, b_pg)


def _layer(h2d, p2d, w_in, b_f, pre_gain, post_gain, conv_w, conv_b, w_rgate, b_rgate,
           w_igate, b_igate, lru_lambda, attn_out_gain, lru_out_gain, w_out, w_ple,
           ple_gain, w_ple_gate, b_ple_gate, *, batch, seq):
    d = h2d.shape[1]
    n_heads = b_f.shape[0]
    d_attn = n_heads * HEAD_DIM
    assert w_in.shape[1] == 4 * d_attn + n_heads + 2 * d and d == d_attn
    fl_lo = 3 * d_attn
    w_qkv = w_in[:, :fl_lo].astype(BF16)
    w_g = w_in[:, fl_lo + n_heads:].astype(BF16)
    w_f = jnp.pad(w_in[:, fl_lo:fl_lo + n_heads],
                  ((0, 0), (0, N_FORGET_PAD - n_heads))).astype(BF16)
    vec = lambda v: v.reshape(1, -1)

    w_ri = (0.5 * jnp.concatenate([w_rgate, w_igate], axis=-1)).astype(BF16)
    qkv, g_attn, y_lru, fl_t = _in_proj_lru(
        h2d, vec(pre_gain), w_qkv, w_g, w_f, conv_w, vec(conv_b), w_ri, vec(b_rgate),
        vec(b_igate), vec(lru_lambda), vec(lru_out_gain), n_heads=n_heads, seq=seq,
        tm=512, chunk=512, ts=256)
    c = _forget_cumsum(fl_t, b_f.reshape(n_heads, 1), seq=seq)
    o_attn = _fox_attention(qkv, c, batch=batch, seq=seq, n_heads=n_heads, tile=512,
                            heads_per_step=2)
    return _out_proj(o_attn, g_attn, y_lru, h2d, p2d, vec(attn_out_gain),
                     w_out.astype(BF16), vec(post_gain), w_ple.astype(BF16),
                     vec(ple_gain), w_ple_gate.astype(BF16), vec(b_ple_gate), tm=1024,
                     sub_rows=512)


def kernel(x, p, w_in, b_f, pre_gain, post_gain, conv_w, conv_b, w_rgate, b_rgate,
           w_igate, b_igate, lru_lambda, attn_out_gain, lru_out_gain, w_out, w_ple,
           ple_gain, w_ple_gate, b_ple_gate):
    batch, seq, d = x.shape
    h = x.reshape(batch * seq, d)
    for i in range(w_in.shape[0]):
        h = _layer(h, p[i].reshape(batch * seq, -1), w_in[i], b_f[i], pre_gain[i],
                   post_gain[i], conv_w[i], conv_b[i], w_rgate[i], b_rgate[i],
                   w_igate[i], b_igate[i], lru_lambda[i], attn_out_gain[i],
                   lru_out_gain[i], w_out[i], w_ple[i], ple_gain[i], w_ple_gate[i],
                   b_ple_gate[i], batch=batch, seq=seq)
    return h.reshape(batch, seq, d)
```

```python
import functools
import math

import jax
import jax.numpy as jnp
from jax import lax
from jax.experimental import pallas as pl
from jax.experimental.pallas import tpu as pltpu

HEAD_DIM = 128
N_LRU_BLOCKS = 8
LRU_BLOCK = 128
CONV_WIDTH = 4
LRU_C = 8.0
RMS_EPS = 1e-6
N_FORGET_PAD = 128
SUBLANES = 8
VMEM_LIMIT_BYTES = 56 * 1024 * 1024
NEG_BIG = -1e30
TINY = 1e-30
LOG2_E = math.log2(math.e)

F32 = jnp.float32
BF16 = jnp.bfloat16


def _sigmoid(x):
    return 0.5 * jnp.tanh(0.5 * x) + 0.5


def _rms_scale(x):
    return lax.rsqrt(jnp.mean(x * x, axis=-1, keepdims=True) + RMS_EPS)


def _segment_permutation(ts):
    p = jnp.arange(ts)
    t = (p % SUBLANES) * (ts // SUBLANES) + p // SUBLANES
    return (t[:, None] == jnp.arange(ts)[None, :]).astype(BF16)


def _in_proj_lru_kernel(x_ref, gain_ref, wqkv_ref, wg_ref, wf_ref, perm_ref, unperm_ref,
                        cw_ref, cb_ref, wri_ref, br_ref, bi_ref, lam_ref, lgain_ref,
                        qkv_ref, gattn_ref, ylru_ref, flt_ref,
                        xlru_scr, glru_scr, xpad_scr, a_scr, u_scr, h_scr, acum_scr,
                        tail_scr, carry_scr,
                        *, d_attn, chunk, q_scale, ts, tiles_per_seq):
    tm, d = x_ref.shape
    n_sub = tm // ts
    seg = ts // SUBLANES
    head = (CONV_WIDTH - 1) * SUBLANES

    @pl.when(pl.program_id(0) % tiles_per_seq == 0)
    def _():
        tail_scr[...] = jnp.zeros(tail_scr.shape, F32)
        carry_scr[...] = jnp.zeros(carry_scr.shape, F32)

    x = x_ref[...]
    xn = (x * _rms_scale(x) * gain_ref[...]).astype(BF16)
    perm = perm_ref[...]
    xn_seg = jnp.concatenate(
        [jnp.dot(perm, xn[s * ts:(s + 1) * ts, :], preferred_element_type=F32).astype(BF16)
         for s in range(n_sub)], axis=0)

    for dst, lo in ((xlru_scr, d_attn), (glru_scr, 2 * d_attn)):
        for c0 in range(0, d_attn, chunk):
            dst[:, c0:c0 + chunk] = jnp.dot(xn_seg, wg_ref[:, lo + c0:lo + c0 + chunk],
                                            preferred_element_type=F32)

    neg_lam = -lam_ref[...]
    k_unit = -LRU_C * (jnp.maximum(neg_lam, 0.0) + jnp.log1p(jnp.exp(-jnp.abs(neg_lam))))
    c1 = (0.5 * LOG2_E) * k_unit
    hb_r = 0.5 * br_ref[...]
    hb_i = 0.5 * bi_ref[...]
    sub = lax.broadcasted_iota(jnp.int32, (SUBLANES, d), 0)

    def lru_gates(s):
        base = s * ts
        x_s = xlru_scr[base:base + ts, :]
        pad0 = s * (ts + head)
        xpad_scr[pad0 + head:pad0 + head + ts, :] = x_s
        for m in range(CONV_WIDTH - 1):
            grp = slice(m * SUBLANES, (m + 1) * SUBLANES)
            cur = x_s[ts - head + m * SUBLANES:ts - head + (m + 1) * SUBLANES, :]
            xpad_scr[pad0 + m * SUBLANES:pad0 + (m + 1) * SUBLANES, :] = jnp.where(
                sub == 0, pltpu.roll(tail_scr[grp, :], 1, axis=0),
                pltpu.roll(cur, 1, axis=0))
        tail_scr[...] = x_s[ts - head:ts, :]
        xc = cb_ref[...] + sum(
            cw_ref[j:j + 1, :]
            * xpad_scr[pad0 + j * SUBLANES:pad0 + j * SUBLANES + ts, :]
            for j in range(CONV_WIDTH))
        xcb = xc.astype(BF16)
        for n in range(N_LRU_BLOCKS):
            cols = slice(n * LRU_BLOCK, (n + 1) * LRU_BLOCK)
            gates = jnp.dot(xcb[:, cols], wri_ref[n], preferred_element_type=F32)
            tr = jnp.tanh(gates[:, :LRU_BLOCK] + hb_r[:, cols])
            ti = jnp.tanh(gates[:, LRU_BLOCK:] + hb_i[:, cols])
            a = jnp.exp2(tr * c1[:, cols] + c1[:, cols])
            y2 = 1.0 - a * a
            a_scr[base:base + ts, cols] = a
            u_scr[base:base + ts, cols] = (y2 * lax.rsqrt(jnp.maximum(y2, TINY))) * (
                (ti + 1.0) * (0.5 * xc[:, cols]))

    def lru_scan(s):
        base = s * ts
        h = jnp.zeros((SUBLANES, d), F32)
        acc = jnp.ones((SUBLANES, d), F32)
        for k in range(seg):
            rows = slice(base + k * SUBLANES, base + (k + 1) * SUBLANES)
            a = a_scr[rows, :]
            h = a * h + u_scr[rows, :]
            acc = a * acc
            h_scr[rows, :] = h
            acum_scr[rows, :] = acc
        c = carry_scr[0:1, :]
        seg_in = [c]
        for r in range(SUBLANES):
            c = acc[r:r + 1, :] * c + h[r:r + 1, :]
            seg_in.append(c)
        carry_scr[...] = jnp.broadcast_to(seg_in[SUBLANES], carry_scr.shape)
        return jnp.concatenate(seg_in[:SUBLANES], axis=0)

    def lru_finish(s, c_in):
        base = s * ts
        gain = lgain_ref[...]
        for k in range(seg):
            rows = slice(base + k * SUBLANES, base + (k + 1) * SUBLANES)
            h = h_scr[rows, :] + acum_scr[rows, :] * c_in
            hg = 0.5 * glru_scr[rows, :]
            a_scr[rows, :] = (h * _rms_scale(h)) * gain * (hg * (jnp.tanh(hg) + 1.0))
        y = jnp.dot(unperm_ref[...], a_scr[base:base + ts, :].astype(BF16),
                    preferred_element_type=F32)
        ylru_ref[base:base + ts, :] = y.astype(ylru_ref.dtype)

    def project(dst, w_ref, w_lo, c0, scale):
        z = jnp.dot(xn, w_ref[:, w_lo + c0:w_lo + c0 + chunk], preferred_element_type=F32)
        if scale is not None:
            z = z * scale
        dst[:, c0:c0 + chunk] = z.astype(dst.dtype)

    chunks = [(qkv_ref, wqkv_ref, 0, c0, q_scale if c0 < d_attn else None)
              for c0 in range(0, 3 * d_attn, chunk)]
    chunks += [(gattn_ref, wg_ref, 0, c0, None) for c0 in range(0, d_attn, chunk)]
    lru_steps = []
    carried = {}
    for s in range(n_sub):
        lru_steps.append(lambda s=s: lru_gates(s))
    for s in range(n_sub):
        lru_steps.append(lambda s=s: carried.__setitem__(s, lru_scan(s)))
        lru_steps.append(lambda s=s: lru_finish(s, carried[s]))
    for n, args in enumerate(chunks):
        project(*args)
        if n < len(lru_steps):
            lru_steps[n]()
    for step in lru_steps[len(chunks):]:
        step()
    fl = jnp.dot(xn, wf_ref[...], preferred_element_type=F32)
    flt_ref[...] = fl.T[:flt_ref.shape[0], :]


def _in_proj_lru(x2d, pre_gain, w_qkv, w_g, w_f, conv_w, conv_b, w_ri, b_r, b_i, lam,
                 lru_gain, *, n_heads, seq, tm, chunk, ts):
    n, d = x2d.shape
    d_attn = n_heads * HEAD_DIM
    n_sub = tm // ts
    kern = functools.partial(_in_proj_lru_kernel, d_attn=d_attn, chunk=chunk,
                             q_scale=HEAD_DIM ** -0.5 * LOG2_E, ts=ts,
                             tiles_per_seq=seq // tm)
    perm = _segment_permutation(ts)
    const = lambda i: (0, 0)
    row = lambda i: (i, 0)
    single = pl.Buffered(1)
    vec = pl.BlockSpec((1, d), const)
    mat = pl.BlockSpec((ts, ts), const)
    head = (CONV_WIDTH - 1) * SUBLANES
    return pl.pallas_call(
        kern,
        grid=(n // tm,),
        in_specs=[
            pl.BlockSpec((tm, d), row),
            vec,
            pl.BlockSpec(w_qkv.shape, const, pipeline_mode=single),
            pl.BlockSpec(w_g.shape, const, pipeline_mode=single),
            pl.BlockSpec(w_f.shape, const, pipeline_mode=single),
            mat, mat,
            pl.BlockSpec((CONV_WIDTH, d), const),
            vec,
            pl.BlockSpec(w_ri.shape, lambda i: (0, 0, 0)),
            vec, vec, vec, vec,
        ],
        out_specs=[
            pl.BlockSpec((tm, 3 * d_attn), row),
            pl.BlockSpec((tm, d_attn), row),
            pl.BlockSpec((tm, d), row),
            pl.BlockSpec((n_heads, tm), lambda i: (0, i)),
        ],
        out_shape=[
            jax.ShapeDtypeStruct((n, 3 * d_attn), BF16),
            jax.ShapeDtypeStruct((n, d_attn), BF16),
            jax.ShapeDtypeStruct((n, d), BF16),
            jax.ShapeDtypeStruct((n_heads, n), F32),
        ],
        scratch_shapes=[pltpu.VMEM((tm, d), F32),
                        pltpu.VMEM((tm, d), F32),
                        pltpu.VMEM((n_sub * (ts + head), d), F32),
                        pltpu.VMEM((tm, d), F32),
                        pltpu.VMEM((tm, d), F32),
                        pltpu.VMEM((tm, d), F32),
                        pltpu.VMEM((tm, d), F32),
                        pltpu.VMEM((head, d), F32),
                        pltpu.VMEM((SUBLANES, d), F32)],
        compiler_params=pltpu.CompilerParams(
            dimension_semantics=("arbitrary",),
            vmem_limit_bytes=VMEM_LIMIT_BYTES),
        name="in_proj_lru",
    )(x2d, pre_gain, w_qkv, w_g, w_f, perm, perm.T, conv_w, conv_b, w_ri, b_r, b_i, lam,
      lru_gain)


def _forget_cumsum_kernel(flt_ref, bf_ref, c_ref):
    z = flt_ref[...] + bf_ref[...]
    ls = jnp.minimum(z, 0.0) - jnp.log1p(jnp.exp(-jnp.abs(z)))
    seq = ls.shape[1]
    lane = lax.broadcasted_iota(jnp.int32, ls.shape, 1)
    shift = 1
    while shift < seq:
        ls = ls + jnp.where(lane >= shift, pltpu.roll(ls, shift, axis=1), 0.0)
        shift *= 2
    c_ref[...] = ls * LOG2_E


def _forget_cumsum(fl_t, b_f, *, seq):
    n_heads, n = fl_t.shape
    return pl.pallas_call(
        _forget_cumsum_kernel,
        grid=(n // seq,),
        in_specs=[pl.BlockSpec((n_heads, seq), lambda b: (0, b)),
                  pl.BlockSpec((n_heads, 1), lambda b: (0, 0))],
        out_specs=pl.BlockSpec((n_heads, seq), lambda b: (0, b)),
        out_shape=jax.ShapeDtypeStruct((n_heads, n), F32),
        compiler_params=pltpu.CompilerParams(dimension_semantics=("arbitrary",)),
        name="forget_cumsum",
    )(fl_t, b_f)


def _fox_attention_kernel(q_ref, k_ref, v_ref, c_ref, o_ref,
                          s0_scr, s1_scr, r0_scr, r1_scr, vext_scr, m_scr, acc_scr,
                          *, tile, heads_per_step):
    hg = pl.program_id(1)
    tq = 2 * tile
    n_super = q_ref.shape[0] // tq
    half_a = slice(0, tile)
    half_b = slice(tile, tq)
    both = slice(0, tq)

    for hh in range(heads_per_step):
        vext_scr[hh, :, :HEAD_DIM] = v_ref[:, hh * HEAD_DIM:(hh + 1) * HEAD_DIM]
        vext_scr[hh, :, HEAD_DIM:] = jnp.ones((vext_scr.shape[1], HEAD_DIM), BF16)

    def scores(hh, i, j, s_scr, r_scr, rows):
        start = pl.multiple_of(j * tile, tile)
        q0 = pl.multiple_of(i * tq + rows.start, tile)
        n_rows = rows.stop - rows.start
        cols = slice(hh * HEAD_DIM, (hh + 1) * HEAD_DIM)
        s = lax.dot_general(q_ref[pl.ds(q0, n_rows), cols],
                            k_ref[pl.ds(start, tile), cols],
                            (((1,), (1,)), ((), ())), preferred_element_type=F32)
        s = s - c_ref[pl.ds(hg * heads_per_step + hh, 1), pl.ds(start, tile)]
        s_scr[rows, :] = s
        r_scr[rows, :] = jnp.broadcast_to(jnp.max(s, axis=1, keepdims=True),
                                          (n_rows, HEAD_DIM))

    def consume(hh, j, s_scr, r_scr, rows, diag):
        start = pl.multiple_of(j * tile, tile)
        if diag:
            half = tile // 2
            parts = [(slice(rows.start, rows.start + half), half),
                     (slice(rows.start + half, rows.stop), tile)]
            row = lax.broadcasted_iota(jnp.int32, (half, HEAD_DIM), 0)
            lane = lax.broadcasted_iota(jnp.int32, (half, HEAD_DIM), 1)
        else:
            parts = [(rows, tile)]
        for prows, n_keys in parts:
            m_prev = m_scr[prows, :]
            s_chunks = [s_scr[prows, c * HEAD_DIM:(c + 1) * HEAD_DIM]
                        for c in range(n_keys // HEAD_DIM)]
            if diag:
                first = (n_keys - half) // HEAD_DIM
                for c in range(first, len(s_chunks)):
                    keep = row >= lane + (c - first) * HEAD_DIM
                    s_chunks[c] = jnp.where(keep, s_chunks[c], NEG_BIG)
                blk_max = functools.reduce(jnp.maximum, s_chunks)
                m_new = jnp.maximum(m_prev, jnp.max(blk_max, axis=1, keepdims=True))
            else:
                m_new = jnp.maximum(m_prev, r_scr[prows, :])
            alpha = jnp.exp2(m_prev - m_new)
            pb = jnp.concatenate([jnp.exp2(sc - m_new).astype(BF16) for sc in s_chunks],
                                 axis=1)
            pv = jnp.dot(pb, vext_scr[hh, pl.ds(start, n_keys), :],
                         preferred_element_type=F32)
            acc_scr[prows, :HEAD_DIM] = (alpha * acc_scr[prows, :HEAD_DIM]
                                         + pv[:, :HEAD_DIM])
            acc_scr[prows, HEAD_DIM:] = (alpha * acc_scr[prows, HEAD_DIM:]
                                         + pv[:, HEAD_DIM:])
            m_scr[prows, :] = m_new

    def finish(hh, i):
        consume(hh, 2 * i + 1, s1_scr, r1_scr, half_b, diag=True)
        o_ref[pl.ds(pl.multiple_of(i * tq, tq), tq), hh * HEAD_DIM:(hh + 1) * HEAD_DIM] = (
            acc_scr[:, :HEAD_DIM] / acc_scr[:, HEAD_DIM:]).astype(o_ref.dtype)

    def reset():
        m_scr[...] = jnp.full(m_scr.shape, NEG_BIG, F32)
        acc_scr[...] = jnp.zeros(acc_scr.shape, F32)

    def super_tile(i, carry):
        for hh in range(heads_per_step):
            def pair(t, c, hh=hh):
                j = 2 * t
                scores(hh, i, j + 1, s1_scr, r1_scr, both)
                consume(hh, j, s0_scr, r0_scr, both, diag=False)
                scores(hh, i, j + 2, s0_scr, r0_scr, both)
                consume(hh, j + 1, s1_scr, r1_scr, both, diag=False)
                return c

            if hh > 0:
                scores(hh, i, 0, s0_scr, r0_scr, both)
                finish(hh - 1, i)
            reset()
            lax.fori_loop(0, i, pair, 0)
            scores(hh, i, 2 * i + 1, s1_scr, r1_scr, half_b)
            consume(hh, 2 * i, s0_scr, r0_scr, half_a, diag=True)
            consume(hh, 2 * i, s0_scr, r0_scr, half_b, diag=False)
        scores(0, jnp.minimum(i + 1, n_super - 1), 0, s0_scr, r0_scr, both)
        finish(heads_per_step - 1, i)
        return carry

    scores(0, 0, 0, s0_scr, r0_scr, both)
    lax.fori_loop(0, n_super, super_tile, 0)


def _fox_attention(qkv, c, *, batch, seq, n_heads, tile, heads_per_step):
    n = qkv.shape[0]
    tq = 2 * tile
    hp = heads_per_step
    w = hp * HEAD_DIM
    ng = n_heads // hp
    kern = functools.partial(_fox_attention_kernel, tile=tile, heads_per_step=hp)
    return pl.pallas_call(
        kern,
        grid=(batch, ng),
        in_specs=[
            pl.BlockSpec((seq, w), lambda b, g: (b, g)),
            pl.BlockSpec((seq, w), lambda b, g: (b, ng + g)),
            pl.BlockSpec((seq, w), lambda b, g: (b, 2 * ng + g)),
            pl.BlockSpec((n_heads, seq), lambda b, g: (0, b)),
        ],
        out_specs=pl.BlockSpec((seq, w), lambda b, g: (b, g)),
        out_shape=jax.ShapeDtypeStruct((n, n_heads * HEAD_DIM), BF16),
        scratch_shapes=[pltpu.VMEM((tq, tile), F32),
                        pltpu.VMEM((tq, tile), F32),
                        pltpu.VMEM((tq, HEAD_DIM), F32),
                        pltpu.VMEM((tq, HEAD_DIM), F32),
                        pltpu.VMEM((hp, seq, 2 * HEAD_DIM), BF16),
                        pltpu.VMEM((tq, HEAD_DIM), F32),
                        pltpu.VMEM((tq, 2 * HEAD_DIM), F32)],
        compiler_params=pltpu.CompilerParams(
            dimension_semantics=("arbitrary", "arbitrary"),
            vmem_limit_bytes=VMEM_LIMIT_BYTES),
        name="fox_attention",
    )(qkv, qkv, qkv, c)


def _out_proj_kernel(oa_ref, ga_ref, yl_ref, x_ref, p_ref, again_ref, wout_ref,
                     pgain_ref, wple_ref, plegain_ref, wpg_ref, bpg_ref, out_ref,
                     *, d_attn, sub_rows):
    subs = [slice(r0, r0 + sub_rows) for r0 in range(0, out_ref.shape[0], sub_rows)]

    def gated_attn(rows):
        oa = oa_ref[rows, :].astype(F32)
        hg = 0.5 * ga_ref[rows, :].astype(F32)
        ya = oa * _rms_scale(oa) * again_ref[...] * (hg * (jnp.tanh(hg) + 1.0))
        return ya.astype(BF16)

    def mixed(rows, ya):
        mix = jnp.dot(ya, wout_ref[0:d_attn, :], preferred_element_type=F32)
        return mix + jnp.dot(yl_ref[rows, :], wout_ref[d_attn:, :],
                             preferred_element_type=F32)

    def embed(rows):
        e = jnp.dot(p_ref[rows, :].astype(BF16), wple_ref[...], preferred_element_type=F32)
        return e * _rms_scale(e) * plegain_ref[...]

    def residual(rows, mix):
        return x_ref[rows, :] + mix * _rms_scale(mix) * pgain_ref[...]

    def gate_logits(h1):
        return jnp.dot(h1.astype(BF16), wpg_ref[...], preferred_element_type=F32)

    ya = [gated_attn(r) for r in subs]
    mix = [mixed(r, y) for r, y in zip(subs, ya)]
    e = [embed(r) for r in subs]
    h1 = [residual(r, m) for r, m in zip(subs, mix)]
    z = [gate_logits(h) for h in h1]
    for r, h, zz, ee in zip(subs, h1, z, e):
        out_ref[r, :] = h + _sigmoid(zz + bpg_ref[...]) * ee


def _out_proj(o_attn, g_attn, y_lru, x2d, p2d, attn_gain, w_out, post_gain, w_ple,
              ple_gain, w_pg, b_pg, *, tm, sub_rows):
    n, d = x2d.shape
    d_attn = o_attn.shape[1]
    d_lru = y_lru.shape[1]
    d_ple = p2d.shape[1]
    kern = functools.partial(_out_proj_kernel, d_attn=d_attn, sub_rows=sub_rows)
    row = lambda i: (i, 0)
    const = lambda i: (0, 0)
    single = pl.Buffered(1)
    return pl.pallas_call(
        kern,
        grid=(n // tm,),
        in_specs=[
            pl.BlockSpec((tm, d_attn), row),
            pl.BlockSpec((tm, d_attn), row),
            pl.BlockSpec((tm, d_lru), row),
            pl.BlockSpec((tm, d), row),
            pl.BlockSpec((tm, d_ple), row),
            pl.BlockSpec((1, d_attn), const),
            pl.BlockSpec(w_out.shape, const, pipeline_mode=single),
            pl.BlockSpec((1, d), const),
            pl.BlockSpec(w_ple.shape, const, pipeline_mode=single),
            pl.BlockSpec((1, d), const),
            pl.BlockSpec(w_pg.shape, const, pipeline_mode=single),
            pl.BlockSpec((1, d), const),
        ],
        out_specs=pl.BlockSpec((tm, d), row),
        out_shape=jax.ShapeDtypeStruct((n, d), F32),
        compiler_params=pltpu.CompilerParams(
            dimension_semantics=("arbitrary",),
            vmem_limit_bytes=VMEM_LIMIT_BYTES),
        name="out_proj",
    )(o_attn, g_attn, y_lru, x2d, p2d, attn_gain, w_out, post_gain, w_ple, ple_gain,
      w_pg, b_pg)


def _layer(h2d, p2d, w_in, b_f, pre_gain, post_gain, conv_w, conv_b, w_rgate, b_rgate,
           w_igate, b_igate, lru_lambda, attn_out_gain, lru_out_gain, w_out, w_ple,
           ple_gain, w_ple_gate, b_ple_gate, *, batch, seq):
    d = h2d.shape[1]
    n_heads = b_f.shape[0]
    d_attn = n_heads * HEAD_DIM
    assert w_in.shape[1] == 4 * d_attn + n_heads + 2 * d and d == d_attn
    fl_lo = 3 * d_attn
    w_qkv = w_in[:, :fl_lo].astype(BF16)
    w_g = w_in[:, fl_lo + n_heads:].astype(BF16)
    w_f = jnp.pad(w_in[:, fl_lo:fl_lo + n_heads],
                  ((0, 0), (0, N_FORGET_PAD - n_heads))).astype(BF16)
    vec = lambda v: v.reshape(1, -1)

    w_ri = (0.5 * jnp.concatenate([w_rgate, w_igate], axis=-1)).astype(BF16)
    qkv, g_attn, y_lru, fl_t = _in_proj_lru(
        h2d, vec(pre_gain), w_qkv, w_g, w_f, conv_w, vec(conv_b), w_ri, vec(b_rgate),
        vec(b_igate), vec(lru_lambda), vec(lru_out_gain), n_heads=n_heads, seq=seq,
        tm=512, chunk=512, ts=256)
    c = _forget_cumsum(fl_t, b_f.reshape(n_heads, 1), seq=seq)
    o_attn = _fox_attention(qkv, c, batch=batch, seq=seq, n_heads=n_heads, tile=512,
                            heads_per_step=2)
    return _out_proj(o_attn, g_attn, y_lru, h2d, p2d, vec(attn_out_gain),
                     w_out.astype(BF16), vec(post_gain), w_ple.astype(BF16),
                     vec(ple_gain), w_ple_gate.astype(BF16), vec(b_ple_gate), tm=1024,
                     sub_rows=512)


def kernel(x, p, w_in, b_f, pre_gain, post_gain, conv_w, conv_b, w_rgate, b_rgate,
           w_igate, b_igate, lru_lambda, attn_out_gain, lru_out_gain, w_out, w_ple,
           ple_gain, w_ple_gate, b_ple_gate):
    batch, seq, d = x.shape
    h = x.reshape(batch * seq, d)
    for i in range(w_in.shape[0]):
        h = _layer(h, p[i].reshape(batch * seq, -1), w_in[i], b_f[i], pre_gain[i],
                   post_gain[i], conv_w[i], conv_b[i], w_rgate[i], b_rgate[i],
                   w_igate[i], b_igate[i], lru_lambda[i], attn_out_gain[i],
                   lru_out_gain[i], w_out[i], w_ple[i], ple_gain[i], w_ple_gate[i],
                   b_ple_gate[i], batch=batch, seq=seq)
    return h.reshape(batch, seq, d)
```

```python
import functools
import math

import jax
import jax.numpy as jnp
from jax import lax
from jax.experimental import pallas as pl
from jax.experimental.pallas import tpu as pltpu

HEAD_DIM = 128
N_LRU_BLOCKS = 8
LRU_BLOCK = 128
CONV_WIDTH = 4
LRU_C = 8.0
RMS_EPS = 1e-6
N_FORGET_PAD = 128
SUBLANES = 8
VMEM_LIMIT_BYTES = 56 * 1024 * 1024
NEG_BIG = -1e30
TINY = 1e-30
LOG2_E = math.log2(math.e)

F32 = jnp.float32
BF16 = jnp.bfloat16


def _sigmoid(x):
    return 0.5 * jnp.tanh(0.5 * x) + 0.5


def _rms_scale(x):
    return lax.rsqrt(jnp.mean(x * x, axis=-1, keepdims=True) + RMS_EPS)


def _segment_permutation(ts):
    p = jnp.arange(ts)
    t = (p % SUBLANES) * (ts // SUBLANES) + p // SUBLANES
    return (t[:, None] == jnp.arange(ts)[None, :]).astype(BF16)


def _in_proj_lru_kernel(x_ref, gain_ref, wqkv_ref, wg_ref, wf_ref, perm_ref, unperm_ref,
                        cw_ref, cb_ref, wri_ref, br_ref, bi_ref, lam_ref, lgain_ref,
                        qkv_ref, gattn_ref, ylru_ref, flt_ref,
                        xlru_scr, glru_scr, xpad_scr, a_scr, u_scr, h_scr, acum_scr,
                        tail_scr, carry_scr,
                        *, d_attn, chunk, q_scale, ts, tiles_per_seq):
    tm, d = x_ref.shape
    n_sub = tm // ts
    seg = ts // SUBLANES
    head = (CONV_WIDTH - 1) * SUBLANES

    @pl.when(pl.program_id(0) % tiles_per_seq == 0)
    def _():
        tail_scr[...] = jnp.zeros(tail_scr.shape, F32)
        carry_scr[...] = jnp.zeros(carry_scr.shape, F32)

    x = x_ref[...]
    xn = (x * _rms_scale(x) * gain_ref[...]).astype(BF16)
    perm = perm_ref[...]
    xn_seg = jnp.concatenate(
        [jnp.dot(perm, xn[s * ts:(s + 1) * ts, :], preferred_element_type=F32).astype(BF16)
         for s in range(n_sub)], axis=0)

    for dst, lo in ((xlru_scr, d_attn), (glru_scr, 2 * d_attn)):
        for c0 in range(0, d_attn, chunk):
            dst[:, c0:c0 + chunk] = jnp.dot(xn_seg, wg_ref[:, lo + c0:lo + c0 + chunk],
                                            preferred_element_type=F32)

    neg_lam = -lam_ref[...]
    k_unit = -LRU_C * (jnp.maximum(neg_lam, 0.0) + jnp.log1p(jnp.exp(-jnp.abs(neg_lam))))
    c1 = (0.5 * LOG2_E) * k_unit
    hb_r = 0.5 * br_ref[...]
    hb_i = 0.5 * bi_ref[...]
    sub = lax.broadcasted_iota(jnp.int32, (SUBLANES, d), 0)

    def lru_gates(s):
        base = s * ts
        x_s = xlru_scr[base:base + ts, :]
        pad0 = s * (ts + head)
        xpad_scr[pad0 + head:pad0 + head + ts, :] = x_s
        for m in range(CONV_WIDTH - 1):
            grp = slice(m * SUBLANES, (m + 1) * SUBLANES)
            cur = x_s[ts - head + m * SUBLANES:ts - head + (m + 1) * SUBLANES, :]
            xpad_scr[pad0 + m * SUBLANES:pad0 + (m + 1) * SUBLANES, :] = jnp.where(
                sub == 0, pltpu.roll(tail_scr[grp, :], 1, axis=0),
                pltpu.roll(cur, 1, axis=0))
        tail_scr[...] = x_s[ts - head:ts, :]
        xc = cb_ref[...] + sum(
            cw_ref[j:j + 1, :]
            * xpad_scr[pad0 + j * SUBLANES:pad0 + j * SUBLANES + ts, :]
            for j in range(CONV_WIDTH))
        xcb = xc.astype(BF16)
        for n in range(N_LRU_BLOCKS):
            cols = slice(n * LRU_BLOCK, (n + 1) * LRU_BLOCK)
            gates = jnp.dot(xcb[:, cols], wri_ref[n], preferred_element_type=F32)
            tr = jnp.tanh(gates[:, :LRU_BLOCK] + hb_r[:, cols])
            ti = jnp.tanh(gates[:, LRU_BLOCK:] + hb_i[:, cols])
            a = jnp.exp2(tr * c1[:, cols] + c1[:, cols])
            y2 = 1.0 - a * a
            a_scr[base:base + ts, cols] = a
            u_scr[base:base + ts, cols] = (y2 * lax.rsqrt(jnp.maximum(y2, TINY))) * (
                (ti + 1.0) * (0.5 * xc[:, cols]))

    def lru_scan(s):
        base = s * ts
        h = jnp.zeros((SUBLANES, d), F32)
        acc = jnp.ones((SUBLANES, d), F32)
        for k in range(seg):
            rows = slice(base + k * SUBLANES, base + (k + 1) * SUBLANES)
            a = a_scr[rows, :]
            h = a * h + u_scr[rows, :]
            acc = a * acc
            h_scr[rows, :] = h
            acum_scr[rows, :] = acc
        c = carry_scr[0:1, :]
        seg_in = [c]
        for r in range(SUBLANES):
            c = acc[r:r + 1, :] * c + h[r:r + 1, :]
            seg_in.append(c)
        carry_scr[...] = jnp.broadcast_to(seg_in[SUBLANES], carry_scr.shape)
        return jnp.concatenate(seg_in[:SUBLANES], axis=0)

    def lru_finish(s, c_in):
        base = s * ts
        gain = lgain_ref[...]
        for k in range(seg):
            rows = slice(base + k * SUBLANES, base + (k + 1) * SUBLANES)
            h = h_scr[rows, :] + acum_scr[rows, :] * c_in
            hg = 0.5 * glru_scr[rows, :]
            a_scr[rows, :] = (h * _rms_scale(h)) * gain * (hg * (jnp.tanh(hg) + 1.0))
        y = jnp.dot(unperm_ref[...], a_scr[base:base + ts, :].astype(BF16),
                    preferred_element_type=F32)
        ylru_ref[base:base + ts, :] = y.astype(ylru_ref.dtype)

    def project(dst, w_ref, w_lo, c0, scale):
        z = jnp.dot(xn, w_ref[:, w_lo + c0:w_lo + c0 + chunk], preferred_element_type=F32)
        if scale is not None:
            z = z * scale
        dst[:, c0:c0 + chunk] = z.astype(dst.dtype)

    chunks = [(qkv_ref, wqkv_ref, 0, c0, q_scale if c0 < d_attn else None)
              for c0 in range(0, 3 * d_attn, chunk)]
    chunks += [(gattn_ref, wg_ref, 0, c0, None) for c0 in range(0, d_attn, chunk)]
    lru_steps = []
    carried = {}
    for s in range(n_sub):
        lru_steps.append(lambda s=s: lru_gates(s))
    for s in range(n_sub):
        lru_steps.append(lambda s=s: carried.__setitem__(s, lru_scan(s)))
        lru_steps.append(lambda s=s: lru_finish(s, carried[s]))
    for n, args in enumerate(chunks):
        project(*args)
        if n < len(lru_steps):
            lru_steps[n]()
    for step in lru_steps[len(chunks):]:
        step()
    fl = jnp.dot(xn, wf_ref[...], preferred_element_type=F32)
    flt_ref[...] = fl.T[:flt_ref.shape[0], :]


def _in_proj_lru(x2d, pre_gain, w_qkv, w_g, w_f, conv_w, conv_b, w_ri, b_r, b_i, lam,
                 lru_gain, *, n_heads, seq, tm, chunk, ts):
    n, d = x2d.shape
    d_attn = n_heads * HEAD_DIM
    n_sub = tm // ts
    kern = functools.partial(_in_proj_lru_kernel, d_attn=d_attn, chunk=chunk,
                             q_scale=HEAD_DIM ** -0.5 * LOG2_E, ts=ts,
                             tiles_per_seq=seq // tm)
    perm = _segment_permutation(ts)
    const = lambda i: (0, 0)
    row = lambda i: (i, 0)
    single = pl.Buffered(1)
    vec = pl.BlockSpec((1, d), const)
    mat = pl.BlockSpec((ts, ts), const)
    head = (CONV_WIDTH - 1) * SUBLANES
    return pl.pallas_call(
        kern,
        grid=(n // tm,),
        in_specs=[
            pl.BlockSpec((tm, d), row),
            vec,
            pl.BlockSpec(w_qkv.shape, const, pipeline_mode=single),
            pl.BlockSpec(w_g.shape, const, pipeline_mode=single),
            pl.BlockSpec(w_f.shape, const, pipeline_mode=single),
            mat, mat,
            pl.BlockSpec((CONV_WIDTH, d), const),
            vec,
            pl.BlockSpec(w_ri.shape, lambda i: (0, 0, 0)),
            vec, vec, vec, vec,
        ],
        out_specs=[
            pl.BlockSpec((tm, 3 * d_attn), row),
            pl.BlockSpec((tm, d_attn), row),
            pl.BlockSpec((tm, d), row),
            pl.BlockSpec((n_heads, tm), lambda i: (0, i)),
        ],
        out_shape=[
            jax.ShapeDtypeStruct((n, 3 * d_attn), BF16),
            jax.ShapeDtypeStruct((n, d_attn), BF16),
            jax.ShapeDtypeStruct((n, d), BF16),
            jax.ShapeDtypeStruct((n_heads, n), F32),
        ],
        scratch_shapes=[pltpu.VMEM((tm, d), F32),
                        pltpu.VMEM((tm, d), F32),
                        pltpu.VMEM((n_sub * (ts + head), d), F32),
                        pltpu.VMEM((tm, d), F32),
                        pltpu.VMEM((tm, d), F32),
                        pltpu.VMEM((tm, d), F32),
                        pltpu.VMEM((tm, d), F32),
                        pltpu.VMEM((head, d), F32),
                        pltpu.VMEM((SUBLANES, d), F32)],
        compiler_params=pltpu.CompilerParams(
            dimension_semantics=("arbitrary",),
            vmem_limit_bytes=VMEM_LIMIT_BYTES),
        name="in_proj_lru",
    )(x2d, pre_gain, w_qkv, w_g, w_f, perm, perm.T, conv_w, conv_b, w_ri, b_r, b_i, lam,
      lru_gain)


def _forget_cumsum_kernel(flt_ref, bf_ref, c_ref):
    z = flt_ref[...] + bf_ref[...]
    ls = jnp.minimum(z, 0.0) - jnp.log1p(jnp.exp(-jnp.abs(z)))
    seq = ls.shape[1]
    lane = lax.broadcasted_iota(jnp.int32, ls.shape, 1)
    shift = 1
    while shift < seq:
        ls = ls + jnp.where(lane >= shift, pltpu.roll(ls, shift, axis=1), 0.0)
        shift *= 2
    c_ref[...] = ls * LOG2_E


def _forget_cumsum(fl_t, b_f, *, seq):
    n_heads, n = fl_t.shape
    return pl.pallas_call(
        _forget_cumsum_kernel,
        grid=(n // seq,),
        in_specs=[pl.BlockSpec((n_heads, seq), lambda b: (0, b)),
                  pl.BlockSpec((n_heads, 1), lambda b: (0, 0))],
        out_specs=pl.BlockSpec((n_heads, seq), lambda b: (0, b)),
        out_shape=jax.ShapeDtypeStruct((n_heads, n), F32),
        compiler_params=pltpu.CompilerParams(dimension_semantics=("arbitrary",)),
        name="forget_cumsum",
    )(fl_t, b_f)


def _fox_attention_kernel(q_ref, k_ref, v_ref, c_ref, o_ref,
                          s0_scr, s1_scr, s2_scr, s3_scr, r0_scr, r1_scr, r2_scr, r3_scr,
                          vext_scr, m_scr, acc_scr, *, tile, heads_per_step):
    hg = pl.program_id(1)
    tq = 2 * tile
    n_super = q_ref.shape[0] // tq
    half_a = slice(0, tile)
    half_b = slice(tile, tq)
    both = slice(0, tq)

    for hh in range(heads_per_step):
        vext_scr[hh, :, :HEAD_DIM] = v_ref[:, hh * HEAD_DIM:(hh + 1) * HEAD_DIM]
        vext_scr[hh, :, HEAD_DIM:] = jnp.ones((vext_scr.shape[1], HEAD_DIM), BF16)

    def scores(hh, i, j, s_scr, r_scr, rows):
        start = pl.multiple_of(j * tile, tile)
        q0 = pl.multiple_of(i * tq + rows.start, tile)
        n_rows = rows.stop - rows.start
        cols = slice(hh * HEAD_DIM, (hh + 1) * HEAD_DIM)
        s = lax.dot_general(q_ref[pl.ds(q0, n_rows), cols],
                            k_ref[pl.ds(start, tile), cols],
                            (((1,), (1,)), ((), ())), preferred_element_type=F32)
        s = s - c_ref[pl.ds(hg * heads_per_step + hh, 1), pl.ds(start, tile)]
        s_scr[rows, :] = s
        r_scr[rows, :] = jnp.broadcast_to(jnp.max(s, axis=1, keepdims=True),
                                          (n_rows, HEAD_DIM))

    def consume(hh, j, s_scr, r_scr, rows, diag):
        start = pl.multiple_of(j * tile, tile)
        if diag:
            half = tile // 2
            parts = [(slice(rows.start, rows.start + half), half),
                     (slice(rows.start + half, rows.stop), tile)]
            row = lax.broadcasted_iota(jnp.int32, (half, HEAD_DIM), 0)
            lane = lax.broadcasted_iota(jnp.int32, (half, HEAD_DIM), 1)
        else:
            parts = [(rows, tile)]
        for prows, n_keys in parts:
            m_prev = m_scr[prows, :]
            s_chunks = [s_scr[prows, c * HEAD_DIM:(c + 1) * HEAD_DIM]
                        for c in range(n_keys // HEAD_DIM)]
            if diag:
                first = (n_keys - half) // HEAD_DIM
                for c in range(first, len(s_chunks)):
                    keep = row >= lane + (c - first) * HEAD_DIM
                    s_chunks[c] = jnp.where(keep, s_chunks[c], NEG_BIG)
                blk_max = functools.reduce(jnp.maximum, s_chunks)
                m_new = jnp.maximum(m_prev, jnp.max(blk_max, axis=1, keepdims=True))
            else:
                m_new = jnp.maximum(m_prev, r_scr[prows, :])
            alpha = jnp.exp2(m_prev - m_new)
            pb = jnp.concatenate([jnp.exp2(sc - m_new).astype(BF16) for sc in s_chunks],
                                 axis=1)
            pv = jnp.dot(pb, vext_scr[hh, pl.ds(start, n_keys), :],
                         preferred_element_type=F32)
            acc_scr[prows, :HEAD_DIM] = (alpha * acc_scr[prows, :HEAD_DIM]
                                         + pv[:, :HEAD_DIM])
            acc_scr[prows, HEAD_DIM:] = (alpha * acc_scr[prows, HEAD_DIM:]
                                         + pv[:, HEAD_DIM:])
            m_scr[prows, :] = m_new

    def reset():
        m_scr[...] = jnp.full(m_scr.shape, NEG_BIG, F32)
        acc_scr[...] = jnp.zeros(acc_scr.shape, F32)

    assert heads_per_step % 2 == 0
    bufs = ((s0_scr, s1_scr, r0_scr, r1_scr), (s2_scr, s3_scr, r2_scr, r3_scr))

    def super_tile(i, carry):
        for hh in range(heads_per_step):
            sa, sb, ra, rb = bufs[hh % 2]
            nxt_s, _, nxt_r, _ = bufs[(hh + 1) % 2]

            def pair(t, c, hh=hh, sa=sa, sb=sb, ra=ra, rb=rb):
                j = 2 * t
                scores(hh, i, j + 1, sb, rb, both)
                consume(hh, j, sa, ra, both, diag=False)
                scores(hh, i, j + 2, sa, ra, both)
                consume(hh, j + 1, sb, rb, both, diag=False)
                return c

            reset()
            lax.fori_loop(0, i, pair, 0)
            scores(hh, i, 2 * i + 1, sb, rb, half_b)
            if hh + 1 < heads_per_step:
                scores(hh + 1, i, 0, nxt_s, nxt_r, both)
            else:
                scores(0, jnp.minimum(i + 1, n_super - 1), 0, nxt_s, nxt_r, both)
            consume(hh, 2 * i, sa, ra, half_a, diag=True)
            consume(hh, 2 * i, sa, ra, half_b, diag=False)
            consume(hh, 2 * i + 1, sb, rb, half_b, diag=True)
            o_ref[pl.ds(pl.multiple_of(i * tq, tq), tq),
                  hh * HEAD_DIM:(hh + 1) * HEAD_DIM] = (
                acc_scr[:, :HEAD_DIM] / acc_scr[:, HEAD_DIM:]).astype(o_ref.dtype)
        return carry

    scores(0, 0, 0, s0_scr, r0_scr, both)
    lax.fori_loop(0, n_super, super_tile, 0)


def _fox_attention(qkv, c, *, batch, seq, n_heads, tile, heads_per_step):
    n = qkv.shape[0]
    tq = 2 * tile
    hp = heads_per_step
    w = hp * HEAD_DIM
    ng = n_heads // hp
    kern = functools.partial(_fox_attention_kernel, tile=tile, heads_per_step=hp)
    return pl.pallas_call(
        kern,
        grid=(batch, ng),
        in_specs=[
            pl.BlockSpec((seq, w), lambda b, g: (b, g)),
            pl.BlockSpec((seq, w), lambda b, g: (b, ng + g)),
            pl.BlockSpec((seq, w), lambda b, g: (b, 2 * ng + g)),
            pl.BlockSpec((n_heads, seq), lambda b, g: (0, b)),
        ],
        out_specs=pl.BlockSpec((seq, w), lambda b, g: (b, g)),
        out_shape=jax.ShapeDtypeStruct((n, n_heads * HEAD_DIM), BF16),
        scratch_shapes=[pltpu.VMEM((tq, tile), F32)] * 4
                       + [pltpu.VMEM((tq, HEAD_DIM), F32)] * 4
                       + [pltpu.VMEM((hp, seq, 2 * HEAD_DIM), BF16),
                        pltpu.VMEM((tq, HEAD_DIM), F32),
                        pltpu.VMEM((tq, 2 * HEAD_DIM), F32)],
        compiler_params=pltpu.CompilerParams(
            dimension_semantics=("arbitrary", "arbitrary"),
            vmem_limit_bytes=VMEM_LIMIT_BYTES),
        name="fox_attention",
    )(qkv, qkv, qkv, c)


def _out_proj_kernel(oa_ref, ga_ref, yl_ref, x_ref, p_ref, again_ref, wout_ref,
                     pgain_ref, wple_ref, plegain_ref, wpg_ref, bpg_ref, out_ref,
                     *, d_attn, sub_rows):
    subs = [slice(r0, r0 + sub_rows) for r0 in range(0, out_ref.shape[0], sub_rows)]

    def gated_attn(rows):
        oa = oa_ref[rows, :].astype(F32)
        hg = 0.5 * ga_ref[rows, :].astype(F32)
        ya = oa * _rms_scale(oa) * again_ref[...] * (hg * (jnp.tanh(hg) + 1.0))
        return ya.astype(BF16)

    def mixed(rows, ya):
        mix = jnp.dot(ya, wout_ref[0:d_attn, :], preferred_element_type=F32)
        return mix + jnp.dot(yl_ref[rows, :], wout_ref[d_attn:, :],
                             preferred_element_type=F32)

    def embed(rows):
        e = jnp.dot(p_ref[rows, :].astype(BF16), wple_ref[...], preferred_element_type=F32)
        return e * _rms_scale(e) * plegain_ref[...]

    def residual(rows, mix):
        return x_ref[rows, :] + mix * _rms_scale(mix) * pgain_ref[...]

    def gate_logits(h1):
        return jnp.dot(h1.astype(BF16), wpg_ref[...], preferred_element_type=F32)

    ya = [gated_attn(r) for r in subs]
    mix = [mixed(r, y) for r, y in zip(subs, ya)]
    e = [embed(r) for r in subs]
    h1 = [residual(r, m) for r, m in zip(subs, mix)]
    z = [gate_logits(h) for h in h1]
    for r, h, zz, ee in zip(subs, h1, z, e):
        out_ref[r, :] = h + _sigmoid(zz + bpg_ref[...]) * ee


def _out_proj(o_attn, g_attn, y_lru, x2d, p2d, attn_gain, w_out, post_gain, w_ple,
              ple_gain, w_pg, b_pg, *, tm, sub_rows):
    n, d = x2d.shape
    d_attn = o_attn.shape[1]
    d_lru = y_lru.shape[1]
    d_ple = p2d.shape[1]
    kern = functools.partial(_out_proj_kernel, d_attn=d_attn, sub_rows=sub_rows)
    row = lambda i: (i, 0)
    const = lambda i: (0, 0)
    single = pl.Buffered(1)
    return pl.pallas_call(
        kern,
        grid=(n // tm,),
        in_specs=[
            pl.BlockSpec((tm, d_attn), row),
            pl.BlockSpec((tm, d_attn), row),
            pl.BlockSpec((tm, d_lru), row),
            pl.BlockSpec((tm, d), row),
            pl.BlockSpec((tm, d_ple), row),
            pl.BlockSpec((1, d_attn), const),
            pl.BlockSpec(w_out.shape, const, pipeline_mode=single),
            pl.BlockSpec((1, d), const),
            pl.BlockSpec(w_ple.shape, const, pipeline_mode=single),
            pl.BlockSpec((1, d), const),
            pl.BlockSpec(w_pg.shape, const, pipeline_mode=single),
            pl.BlockSpec((1, d), const),
        ],
        out_specs=pl.BlockSpec((tm, d), row),
        out_shape=jax.ShapeDtypeStruct((n, d), F32),
        compiler_params=pltpu.CompilerParams(
            dimension_semantics=("arbitrary",),
            vmem_limit_bytes=VMEM_LIMIT_BYTES),
        name="out_proj",
    )(o_attn, g_attn, y_lru, x2d, p2d, attn_gain, w_out, post_gain, w_ple, ple_gain,
      w_pg, b_pg)


def _layer(h2d, p2d, w_in, b_f, pre_gain, post_gain, conv_w, conv_b, w_rgate, b_rgate,
           w_igate, b_igate, lru_lambda, attn_out_gain, lru_out_gain, w_out, w_ple,
           ple_gain, w_ple_gate, b_ple_gate, *, batch, seq):
    d = h2d.shape[1]
    n_heads = b_f.shape[0]
    d_attn = n_heads * HEAD_DIM
    assert w_in.shape[1] == 4 * d_attn + n_heads + 2 * d and d == d_attn
    fl_lo = 3 * d_attn
    w_qkv = w_in[:, :fl_lo].astype(BF16)
    w_g = w_in[:, fl_lo + n_heads:].astype(BF16)
    w_f = jnp.pad(w_in[:, fl_lo:fl_lo + n_heads],
                  ((0, 0), (0, N_FORGET_PAD - n_heads))).astype(BF16)
    vec = lambda v: v.reshape(1, -1)

    w_ri = (0.5 * jnp.concatenate([w_rgate, w_igate], axis=-1)).astype(BF16)
    qkv, g_attn, y_lru, fl_t = _in_proj_lru(
        h2d, vec(pre_gain), w_qkv, w_g, w_f, conv_w, vec(conv_b), w_ri, vec(b_rgate),
        vec(b_igate), vec(lru_lambda), vec(lru_out_gain), n_heads=n_heads, seq=seq,
        tm=512, chunk=512, ts=256)
    c = _forget_cumsum(fl_t, b_f.reshape(n_heads, 1), seq=seq)
    o_attn = _fox_attention(qkv, c, batch=batch, seq=seq, n_heads=n_heads, tile=512,
                            heads_per_step=2)
    return _out_proj(o_attn, g_attn, y_lru, h2d, p2d, vec(attn_out_gain),
                     w_out.astype(BF16), vec(post_gain), w_ple.astype(BF16),
                     vec(ple_gain), w_ple_gate.astype(BF16), vec(b_ple_gate), tm=1024,
                     sub_rows=512)


def kernel(x, p, w_in, b_f, pre_gain, post_gain, conv_w, conv_b, w_rgate, b_rgate,
           w_igate, b_igate, lru_lambda, attn_out_gain, lru_out_gain, w_out, w_ple,
           ple_gain, w_ple_gate, b_ple_gate):
    batch, seq, d = x.shape
    h = x.reshape(batch * seq, d)
    for i in range(w_in.shape[0]):
        h = _layer(h, p[i].reshape(batch * seq, -1), w_in[i], b_f[i], pre_gain[i],
                   post_gain[i], conv_w[i], conv_b[i], w_rgate[i], b_rgate[i],
                   w_igate[i], b_igate[i], lru_lambda[i], attn_out_gain[i],
                   lru_out_gain[i], w_out[i], w_ple[i], ple_gain[i], w_ple_gate[i],
                   b_ple_gate[i], batch=batch, seq=seq)
    return h.reshape(batch, seq, d)
```

```python
import functools
import math

import jax
import jax.numpy as jnp
from jax import lax
from jax.experimental import pallas as pl
from jax.experimental.pallas import tpu as pltpu

HEAD_DIM = 128
N_LRU_BLOCKS = 8
LRU_BLOCK = 128
CONV_WIDTH = 4
LRU_C = 8.0
RMS_EPS = 1e-6
N_FORGET_PAD = 128
SUBLANES = 8
VMEM_LIMIT_BYTES = 56 * 1024 * 1024
NEG_BIG = -1e30
TINY = 1e-30
LOG2_E = math.log2(math.e)

F32 = jnp.float32
BF16 = jnp.bfloat16


def _sigmoid(x):
    return 0.5 * jnp.tanh(0.5 * x) + 0.5


def _rms_scale(x):
    return lax.rsqrt(jnp.mean(x * x, axis=-1, keepdims=True) + RMS_EPS)


def _split_w_in_kernel(w_ref, wqkv_ref, wg_ref, wf_ref, *, fl_lo, n_forget):
    wqkv_ref[...] = w_ref[:, :fl_lo].astype(BF16)
    wg_ref[...] = w_ref[:, fl_lo + n_forget:].astype(BF16)
    f = w_ref[:, fl_lo:fl_lo + N_FORGET_PAD]
    lane = lax.broadcasted_iota(jnp.int32, f.shape, 1)
    wf_ref[...] = jnp.where(lane < n_forget, f, 0.0).astype(BF16)


def _split_w_in(w_in, *, fl_lo, n_forget, rows):
    d, n = w_in.shape
    n_g = n - fl_lo - n_forget
    kern = functools.partial(_split_w_in_kernel, fl_lo=fl_lo, n_forget=n_forget)
    row = lambda i: (i, 0)
    return pl.pallas_call(
        kern,
        grid=(d // rows,),
        in_specs=[pl.BlockSpec((rows, n), row)],
        out_specs=[pl.BlockSpec((rows, fl_lo), row),
                   pl.BlockSpec((rows, n_g), row),
                   pl.BlockSpec((rows, N_FORGET_PAD), row)],
        out_shape=[jax.ShapeDtypeStruct((d, fl_lo), BF16),
                   jax.ShapeDtypeStruct((d, n_g), BF16),
                   jax.ShapeDtypeStruct((d, N_FORGET_PAD), BF16)],
        compiler_params=pltpu.CompilerParams(dimension_semantics=("arbitrary",)),
        name="split_w_in",
    )(w_in)


def _segment_permutation(ts):
    p = jnp.arange(ts)
    t = (p % SUBLANES) * (ts // SUBLANES) + p // SUBLANES
    return (t[:, None] == jnp.arange(ts)[None, :]).astype(BF16)


def _in_proj_lru_kernel(x_ref, gain_ref, wqkv_ref, wg_ref, wf_ref, perm_ref, unperm_ref,
                        cw_ref, cb_ref, wri_ref, br_ref, bi_ref, lam_ref, lgain_ref,
                        qkv_ref, gattn_ref, ylru_ref, flt_ref,
                        xlru_scr, glru_scr, xpad_scr, a_scr, u_scr, h_scr, acum_scr,
                        tail_scr, carry_scr,
                        *, d_attn, chunk, q_scale, ts, tiles_per_seq):
    tm, d = x_ref.shape
    n_sub = tm // ts
    seg = ts // SUBLANES
    head = (CONV_WIDTH - 1) * SUBLANES

    @pl.when(pl.program_id(0) % tiles_per_seq == 0)
    def _():
        tail_scr[...] = jnp.zeros(tail_scr.shape, F32)
        carry_scr[...] = jnp.zeros(carry_scr.shape, F32)

    x = x_ref[...]
    xn = (x * _rms_scale(x) * gain_ref[...]).astype(BF16)
    perm = perm_ref[...]
    xn_seg = jnp.concatenate(
        [jnp.dot(perm, xn[s * ts:(s + 1) * ts, :], preferred_element_type=F32).astype(BF16)
         for s in range(n_sub)], axis=0)

    for dst, lo in ((xlru_scr, d_attn), (glru_scr, 2 * d_attn)):
        for c0 in range(0, d_attn, chunk):
            dst[:, c0:c0 + chunk] = jnp.dot(xn_seg, wg_ref[:, lo + c0:lo + c0 + chunk],
                                            preferred_element_type=F32)

    neg_lam = -lam_ref[...]
    k_unit = -LRU_C * (jnp.maximum(neg_lam, 0.0) + jnp.log1p(jnp.exp(-jnp.abs(neg_lam))))
    c1 = (0.5 * LOG2_E) * k_unit
    hb_r = 0.5 * br_ref[...]
    hb_i = 0.5 * bi_ref[...]
    sub = lax.broadcasted_iota(jnp.int32, (SUBLANES, d), 0)

    def lru_gates(s):
        base = s * ts
        x_s = xlru_scr[base:base + ts, :]
        pad0 = s * (ts + head)
        xpad_scr[pad0 + head:pad0 + head + ts, :] = x_s
        for m in range(CONV_WIDTH - 1):
            grp = slice(m * SUBLANES, (m + 1) * SUBLANES)
            cur = x_s[ts - head + m * SUBLANES:ts - head + (m + 1) * SUBLANES, :]
            xpad_scr[pad0 + m * SUBLANES:pad0 + (m + 1) * SUBLANES, :] = jnp.where(
                sub == 0, pltpu.roll(tail_scr[grp, :], 1, axis=0),
                pltpu.roll(cur, 1, axis=0))
        tail_scr[...] = x_s[ts - head:ts, :]
        xc = cb_ref[...] + sum(
            cw_ref[j:j + 1, :]
            * xpad_scr[pad0 + j * SUBLANES:pad0 + j * SUBLANES + ts, :]
            for j in range(CONV_WIDTH))
        xcb = xc.astype(BF16)
        for n in range(N_LRU_BLOCKS):
            cols = slice(n * LRU_BLOCK, (n + 1) * LRU_BLOCK)
            gates = jnp.dot(xcb[:, cols], wri_ref[n], preferred_element_type=F32)
            tr = jnp.tanh(gates[:, :LRU_BLOCK] + hb_r[:, cols])
            ti = jnp.tanh(gates[:, LRU_BLOCK:] + hb_i[:, cols])
            a = jnp.exp2(tr * c1[:, cols] + c1[:, cols])
            y2 = 1.0 - a * a
            a_scr[base:base + ts, cols] = a
            u_scr[base:base + ts, cols] = (y2 * lax.rsqrt(jnp.maximum(y2, TINY))) * (
                (ti + 1.0) * (0.5 * xc[:, cols]))

    def lru_scan(s):
        base = s * ts
        h = jnp.zeros((SUBLANES, d), F32)
        acc = jnp.ones((SUBLANES, d), F32)
        for k in range(seg):
            rows = slice(base + k * SUBLANES, base + (k + 1) * SUBLANES)
            a = a_scr[rows, :]
            h = a * h + u_scr[rows, :]
            acc = a * acc
            h_scr[rows, :] = h
            acum_scr[rows, :] = acc
        c = carry_scr[0:1, :]
        seg_in = [c]
        for r in range(SUBLANES):
            c = acc[r:r + 1, :] * c + h[r:r + 1, :]
            seg_in.append(c)
        carry_scr[...] = jnp.broadcast_to(seg_in[SUBLANES], carry_scr.shape)
        return jnp.concatenate(seg_in[:SUBLANES], axis=0)

    def lru_finish(s, c_in):
        base = s * ts
        gain = lgain_ref[...]
        for k in range(seg):
            rows = slice(base + k * SUBLANES, base + (k + 1) * SUBLANES)
            h = h_scr[rows, :] + acum_scr[rows, :] * c_in
            hg = 0.5 * glru_scr[rows, :]
            a_scr[rows, :] = (h * _rms_scale(h)) * gain * (hg * (jnp.tanh(hg) + 1.0))
        y = jnp.dot(unperm_ref[...], a_scr[base:base + ts, :].astype(BF16),
                    preferred_element_type=F32)
        ylru_ref[base:base + ts, :] = y.astype(ylru_ref.dtype)

    def project(dst, w_ref, w_lo, c0, scale):
        z = jnp.dot(xn, w_ref[:, w_lo + c0:w_lo + c0 + chunk], preferred_element_type=F32)
        if scale is not None:
            z = z * scale
        dst[:, c0:c0 + chunk] = z.astype(dst.dtype)

    chunks = [(qkv_ref, wqkv_ref, 0, c0, q_scale if c0 < d_attn else None)
              for c0 in range(0, 3 * d_attn, chunk)]
    chunks += [(gattn_ref, wg_ref, 0, c0, None) for c0 in range(0, d_attn, chunk)]
    lru_steps = []
    carried = {}
    for s in range(n_sub):
        lru_steps.append(lambda s=s: lru_gates(s))
    for s in range(n_sub):
        lru_steps.append(lambda s=s: carried.__setitem__(s, lru_scan(s)))
        lru_steps.append(lambda s=s: lru_finish(s, carried[s]))
    for n, args in enumerate(chunks):
        project(*args)
        if n < len(lru_steps):
            lru_steps[n]()
    for step in lru_steps[len(chunks):]:
        step()
    fl = jnp.dot(xn, wf_ref[...], preferred_element_type=F32)
    flt_ref[...] = fl.T[:flt_ref.shape[0], :]


def _in_proj_lru(x2d, pre_gain, w_qkv, w_g, w_f, conv_w, conv_b, w_ri, b_r, b_i, lam,
                 lru_gain, *, n_heads, seq, tm, chunk, ts):
    n, d = x2d.shape
    d_attn = n_heads * HEAD_DIM
    n_sub = tm // ts
    kern = functools.partial(_in_proj_lru_kernel, d_attn=d_attn, chunk=chunk,
                             q_scale=HEAD_DIM ** -0.5 * LOG2_E, ts=ts,
                             tiles_per_seq=seq // tm)
    perm = _segment_permutation(ts)
    const = lambda i: (0, 0)
    row = lambda i: (i, 0)
    single = pl.Buffered(1)
    vec = pl.BlockSpec((1, d), const)
    mat = pl.BlockSpec((ts, ts), const)
    head = (CONV_WIDTH - 1) * SUBLANES
    return pl.pallas_call(
        kern,
        grid=(n // tm,),
        in_specs=[
            pl.BlockSpec((tm, d), row),
            vec,
            pl.BlockSpec(w_qkv.shape, const, pipeline_mode=single),
            pl.BlockSpec(w_g.shape, const, pipeline_mode=single),
            pl.BlockSpec(w_f.shape, const, pipeline_mode=single),
            mat, mat,
            pl.BlockSpec((CONV_WIDTH, d), const),
            vec,
            pl.BlockSpec(w_ri.shape, lambda i: (0, 0, 0)),
            vec, vec, vec, vec,
        ],
        out_specs=[
            pl.BlockSpec((tm, 3 * d_attn), row),
            pl.BlockSpec((tm, d_attn), row),
            pl.BlockSpec((tm, d), row),
            pl.BlockSpec((n_heads, tm), lambda i: (0, i)),
        ],
        out_shape=[
            jax.ShapeDtypeStruct((n, 3 * d_attn), BF16),
            jax.ShapeDtypeStruct((n, d_attn), BF16),
            jax.ShapeDtypeStruct((n, d), BF16),
            jax.ShapeDtypeStruct((n_heads, n), F32),
        ],
        scratch_shapes=[pltpu.VMEM((tm, d), F32),
                        pltpu.VMEM((tm, d), F32),
                        pltpu.VMEM((n_sub * (ts + head), d), F32),
                        pltpu.VMEM((tm, d), F32),
                        pltpu.VMEM((tm, d), F32),
                        pltpu.VMEM((tm, d), F32),
                        pltpu.VMEM((tm, d), F32),
                        pltpu.VMEM((head, d), F32),
                        pltpu.VMEM((SUBLANES, d), F32)],
        compiler_params=pltpu.CompilerParams(
            dimension_semantics=("arbitrary",),
            vmem_limit_bytes=VMEM_LIMIT_BYTES),
        name="in_proj_lru",
    )(x2d, pre_gain, w_qkv, w_g, w_f, perm, perm.T, conv_w, conv_b, w_ri, b_r, b_i, lam,
      lru_gain)


def _forget_cumsum_kernel(flt_ref, bf_ref, c_ref):
    z = flt_ref[...] + bf_ref[...]
    ls = jnp.minimum(z, 0.0) - jnp.log1p(jnp.exp(-jnp.abs(z)))
    seq = ls.shape[1]
    lane = lax.broadcasted_iota(jnp.int32, ls.shape, 1)
    shift = 1
    while shift < seq:
        ls = ls + jnp.where(lane >= shift, pltpu.roll(ls, shift, axis=1), 0.0)
        shift *= 2
    c_ref[...] = ls * LOG2_E


def _forget_cumsum(fl_t, b_f, *, seq):
    n_heads, n = fl_t.shape
    return pl.pallas_call(
        _forget_cumsum_kernel,
        grid=(n // seq,),
        in_specs=[pl.BlockSpec((n_heads, seq), lambda b: (0, b)),
                  pl.BlockSpec((n_heads, 1), lambda b: (0, 0))],
        out_specs=pl.BlockSpec((n_heads, seq), lambda b: (0, b)),
        out_shape=jax.ShapeDtypeStruct((n_heads, n), F32),
        compiler_params=pltpu.CompilerParams(dimension_semantics=("arbitrary",)),
        name="forget_cumsum",
    )(fl_t, b_f)


def _fox_attention_kernel(q_ref, k_ref, v_ref, c_ref, o_ref,
                          s0_scr, s1_scr, s2_scr, s3_scr, r0_scr, r1_scr, r2_scr, r3_scr,
                          vext_scr, m_scr, acc_scr, *, tile, heads_per_step):
    hg = pl.program_id(1)
    tq = 2 * tile
    n_super = q_ref.shape[0] // tq
    half_a = slice(0, tile)
    half_b = slice(tile, tq)
    both = slice(0, tq)

    for hh in range(heads_per_step):
        vext_scr[hh, :, :HEAD_DIM] = v_ref[:, hh * HEAD_DIM:(hh + 1) * HEAD_DIM]
        vext_scr[hh, :, HEAD_DIM:] = jnp.ones((vext_scr.shape[1], HEAD_DIM), BF16)

    def scores(hh, i, j, s_scr, r_scr, rows):
        start = pl.multiple_of(j * tile, tile)
        q0 = pl.multiple_of(i * tq + rows.start, tile)
        n_rows = rows.stop - rows.start
        cols = slice(hh * HEAD_DIM, (hh + 1) * HEAD_DIM)
        s = lax.dot_general(q_ref[pl.ds(q0, n_rows), cols],
                            k_ref[pl.ds(start, tile), cols],
                            (((1,), (1,)), ((), ())), preferred_element_type=F32)
        s = s - c_ref[pl.ds(hg * heads_per_step + hh, 1), pl.ds(start, tile)]
        s_scr[rows, :tile] = s
        r_scr[rows, :] = jnp.broadcast_to(jnp.max(s, axis=1, keepdims=True),
                                          (n_rows, HEAD_DIM))

    def consume(hh, j, s_scr, r_scr, rows, diag):
        start = pl.multiple_of(j * tile, tile)
        if diag:
            half = tile // 2
            parts = [(slice(rows.start, rows.start + half), half),
                     (slice(rows.start + half, rows.stop), tile)]
            row = lax.broadcasted_iota(jnp.int32, (half, HEAD_DIM), 0)
            lane = lax.broadcasted_iota(jnp.int32, (half, HEAD_DIM), 1)
        else:
            parts = [(rows, tile)]
        for prows, n_keys in parts:
            m_prev = m_scr[prows, :]
            s_chunks = [s_scr[prows, c * HEAD_DIM:(c + 1) * HEAD_DIM]
                        for c in range(n_keys // HEAD_DIM)]
            if diag:
                first = (n_keys - half) // HEAD_DIM
                for c in range(first, len(s_chunks)):
                    keep = row >= lane + (c - first) * HEAD_DIM
                    s_chunks[c] = jnp.where(keep, s_chunks[c], NEG_BIG)
                blk_max = functools.reduce(jnp.maximum, s_chunks)
                m_new = jnp.maximum(m_prev, jnp.max(blk_max, axis=1, keepdims=True))
            else:
                m_new = jnp.maximum(m_prev, r_scr[prows, :])
            alpha = jnp.exp2(m_prev - m_new)
            pb = jnp.concatenate([jnp.exp2(sc - m_new).astype(BF16) for sc in s_chunks],
                                 axis=1)
            pv = jnp.dot(pb, vext_scr[hh, pl.ds(start, n_keys), :],
                         preferred_element_type=F32)
            acc_scr[prows, :HEAD_DIM] = (alpha * acc_scr[prows, :HEAD_DIM]
                                         + pv[:, :HEAD_DIM])
            acc_scr[prows, HEAD_DIM:] = (alpha * acc_scr[prows, HEAD_DIM:]
                                         + pv[:, HEAD_DIM:])
            m_scr[prows, :] = m_new

    def reset():
        m_scr[...] = jnp.full(m_scr.shape, NEG_BIG, F32)
        acc_scr[...] = jnp.zeros(acc_scr.shape, F32)

    assert heads_per_step % 2 == 0
    bufs = ((s0_scr, s1_scr, r0_scr, r1_scr), (s2_scr, s3_scr, r2_scr, r3_scr))

    def super_tile(i, carry):
        for hh in range(heads_per_step):
            sa, sb, ra, rb = bufs[hh % 2]
            nxt_s, _, nxt_r, _ = bufs[(hh + 1) % 2]

            def pair(t, c, hh=hh, sa=sa, sb=sb, ra=ra, rb=rb):
                j = 2 * t
                scores(hh, i, j + 1, sb, rb, both)
                consume(hh, j, sa, ra, both, diag=False)
                scores(hh, i, j + 2, sa, ra, both)
                consume(hh, j + 1, sb, rb, both, diag=False)
                return c

            reset()
            lax.fori_loop(0, i, pair, 0)
            scores(hh, i, 2 * i + 1, sb, rb, half_b)
            if hh + 1 < heads_per_step:
                scores(hh + 1, i, 0, nxt_s, nxt_r, both)
            else:
                scores(0, jnp.minimum(i + 1, n_super - 1), 0, nxt_s, nxt_r, both)
            consume(hh, 2 * i, sa, ra, half_a, diag=True)
            consume(hh, 2 * i, sa, ra, half_b, diag=False)
            consume(hh, 2 * i + 1, sb, rb, half_b, diag=True)
            o_ref[pl.ds(pl.multiple_of(i * tq, tq), tq),
                  hh * HEAD_DIM:(hh + 1) * HEAD_DIM] = (
                acc_scr[:, :HEAD_DIM] / acc_scr[:, HEAD_DIM:]).astype(o_ref.dtype)
        return carry

    scores(0, 0, 0, s0_scr, r0_scr, both)
    lax.fori_loop(0, n_super, super_tile, 0)


def _fox_attention(qkv, c, *, batch, seq, n_heads, tile, heads_per_step):
    n = qkv.shape[0]
    tq = 2 * tile
    hp = heads_per_step
    w = hp * HEAD_DIM
    ng = n_heads // hp
    kern = functools.partial(_fox_attention_kernel, tile=tile, heads_per_step=hp)
    return pl.pallas_call(
        kern,
        grid=(batch, ng),
        in_specs=[
            pl.BlockSpec((seq, w), lambda b, g: (b, g)),
            pl.BlockSpec((seq, w), lambda b, g: (b, ng + g)),
            pl.BlockSpec((seq, w), lambda b, g: (b, 2 * ng + g)),
            pl.BlockSpec((n_heads, seq), lambda b, g: (0, b)),
        ],
        out_specs=pl.BlockSpec((seq, w), lambda b, g: (b, g)),
        out_shape=jax.ShapeDtypeStruct((n, n_heads * HEAD_DIM), BF16),
        scratch_shapes=[pltpu.VMEM((tq, tile + HEAD_DIM), F32)] * 4
                       + [pltpu.VMEM((tq, HEAD_DIM), F32)] * 4
                       + [pltpu.VMEM((hp, seq, 2 * HEAD_DIM), BF16),
                        pltpu.VMEM((tq, HEAD_DIM), F32),
                        pltpu.VMEM((tq, 2 * HEAD_DIM), F32)],
        compiler_params=pltpu.CompilerParams(
            dimension_semantics=("arbitrary", "arbitrary"),
            vmem_limit_bytes=VMEM_LIMIT_BYTES),
        name="fox_attention",
    )(qkv, qkv, qkv, c)


def _out_proj_kernel(oa_ref, ga_ref, yl_ref, x_ref, p_ref, again_ref, wout_ref,
                     pgain_ref, wple_ref, plegain_ref, wpg_ref, bpg_ref, out_ref,
                     *, d_attn, sub_rows):
    subs = [slice(r0, r0 + sub_rows) for r0 in range(0, out_ref.shape[0], sub_rows)]

    def gated_attn(rows):
        oa = oa_ref[rows, :].astype(F32)
        hg = 0.5 * ga_ref[rows, :].astype(F32)
        ya = oa * _rms_scale(oa) * again_ref[...] * (hg * (jnp.tanh(hg) + 1.0))
        return ya.astype(BF16)

    def mixed(rows, ya):
        mix = jnp.dot(ya, wout_ref[0:d_attn, :], preferred_element_type=F32)
        return mix + jnp.dot(yl_ref[rows, :], wout_ref[d_attn:, :],
                             preferred_element_type=F32)

    def embed(rows):
        e = jnp.dot(p_ref[rows, :].astype(BF16), wple_ref[...], preferred_element_type=F32)
        return e * _rms_scale(e) * plegain_ref[...]

    def residual(rows, mix):
        return x_ref[rows, :] + mix * _rms_scale(mix) * pgain_ref[...]

    def gate_logits(h1):
        return jnp.dot(h1.astype(BF16), wpg_ref[...], preferred_element_type=F32)

    ya = [gated_attn(r) for r in subs]
    mix = [mixed(r, y) for r, y in zip(subs, ya)]
    e = [embed(r) for r in subs]
    h1 = [residual(r, m) for r, m in zip(subs, mix)]
    z = [gate_logits(h) for h in h1]
    for r, h, zz, ee in zip(subs, h1, z, e):
        out_ref[r, :] = h + _sigmoid(zz + bpg_ref[...]) * ee


def _out_proj(o_attn, g_attn, y_lru, x2d, p2d, attn_gain, w_out, post_gain, w_ple,
              ple_gain, w_pg, b_pg, *, tm, sub_rows):
    n, d = x2d.shape
    d_attn = o_attn.shape[1]
    d_lru = y_lru.shape[1]
    d_ple = p2d.shape[1]
    kern = functools.partial(_out_proj_kernel, d_attn=d_attn, sub_rows=sub_rows)
    row = lambda i: (i, 0)
    const = lambda i: (0, 0)
    single = pl.Buffered(1)
    return pl.pallas_call(
        kern,
        grid=(n // tm,),
        in_specs=[
            pl.BlockSpec((tm, d_attn), row),
            pl.BlockSpec((tm, d_attn), row),
            pl.BlockSpec((tm, d_lru), row),
            pl.BlockSpec((tm, d), row),
            pl.BlockSpec((tm, d_ple), row),
            pl.BlockSpec((1, d_attn), const),
            pl.BlockSpec(w_out.shape, const, pipeline_mode=single),
            pl.BlockSpec((1, d), const),
            pl.BlockSpec(w_ple.shape, const, pipeline_mode=single),
            pl.BlockSpec((1, d), const),
            pl.BlockSpec(w_pg.shape, const, pipeline_mode=single),
            pl.BlockSpec((1, d), const),
        ],
        out_specs=pl.BlockSpec((tm, d), row),
        out_shape=jax.ShapeDtypeStruct((n, d), F32),
        compiler_params=pltpu.CompilerParams(
            dimension_semantics=("arbitrary",),
            vmem_limit_bytes=VMEM_LIMIT_BYTES),
        name="out_proj",
    )(o_attn, g_attn, y_lru, x2d, p2d, attn_gain, w_out, post_gain, w_ple, ple_gain,
      w_pg, b_pg)


def _layer(h2d, p2d, w_in, b_f, pre_gain, post_gain, conv_w, conv_b, w_rgate, b_rgate,
           w_igate, b_igate, lru_lambda, attn_out_gain, lru_out_gain, w_out, w_ple,
           ple_gain, w_ple_gate, b_ple_gate, *, batch, seq):
    d = h2d.shape[1]
    n_heads = b_f.shape[0]
    d_attn = n_heads * HEAD_DIM
    assert w_in.shape[1] == 4 * d_attn + n_heads + 2 * d and d == d_attn
    fl_lo = 3 * d_attn
    w_qkv, w_g, w_f = _split_w_in(w_in, fl_lo=fl_lo, n_forget=n_heads, rows=128)
    vec = lambda v: v.reshape(1, -1)

    w_ri = (0.5 * jnp.concatenate([w_rgate, w_igate], axis=-1)).astype(BF16)
    qkv, g_attn, y_lru, fl_t = _in_proj_lru(
        h2d, vec(pre_gain), w_qkv, w_g, w_f, conv_w, vec(conv_b), w_ri, vec(b_rgate),
        vec(b_igate), vec(lru_lambda), vec(lru_out_gain), n_heads=n_heads, seq=seq,
        tm=512, chunk=512, ts=256)
    c = _forget_cumsum(fl_t, b_f.reshape(n_heads, 1), seq=seq)
    o_attn = _fox_attention(qkv, c, batch=batch, seq=seq, n_heads=n_heads, tile=512,
                            heads_per_step=2)
    return _out_proj(o_attn, g_attn, y_lru, h2d, p2d, vec(attn_out_gain),
                     w_out.astype(BF16), vec(post_gain), w_ple.astype(BF16),
                     vec(ple_gain), w_ple_gate.astype(BF16), vec(b_ple_gate), tm=1024,
                     sub_rows=512)


def kernel(x, p, w_in, b_f, pre_gain, post_gain, conv_w, conv_b, w_rgate, b_rgate,
           w_igate, b_igate, lru_lambda, attn_out_gain, lru_out_gain, w_out, w_ple,
           ple_gain, w_ple_gate, b_ple_gate):
    batch, seq, d = x.shape
    h = x.reshape(batch * seq, d)
    for i in range(w_in.shape[0]):
        h = _layer(h, p[i].reshape(batch * seq, -1), w_in[i], b_f[i], pre_gain[i],
                   post_gain[i], conv_w[i], conv_b[i], w_rgate[i], b_rgate[i],
                   w_igate[i], b_igate[i], lru_lambda[i], attn_out_gain[i],
                   lru_out_gain[i], w_out[i], w_ple[i], ple_gain[i], w_ple_gate[i],
                   b_ple_gate[i], batch=batch, seq=seq)
    return h.reshape(batch, seq, d)
```

```python
import functools
import math

import jax
import jax.numpy as jnp
from jax import lax
from jax.experimental import pallas as pl
from jax.experimental.pallas import tpu as pltpu

HEAD_DIM = 128
N_LRU_BLOCKS = 8
LRU_BLOCK = 128
CONV_WIDTH = 4
LRU_C = 8.0
RMS_EPS = 1e-6
N_FORGET_PAD = 128
SUBLANES = 8
VMEM_LIMIT_BYTES = 56 * 1024 * 1024
NEG_BIG = -1e30
TINY = 1e-30
LOG2_E = math.log2(math.e)

F32 = jnp.float32
BF16 = jnp.bfloat16

PROJ_ROWS = 512
PROJ_COLS = 512
LRU_ROWS = 256
ATTN_KV_TILE = 512
ATTN_HEADS_PER_STEP = 2
OUT_ROWS = 1024
OUT_SUB_ROWS = 512


def _sigmoid(x):
    return 0.5 * jnp.tanh(0.5 * x) + 0.5


def _rms_scale(x):
    return lax.rsqrt(jnp.mean(x * x, axis=-1, keepdims=True) + RMS_EPS)


def _segment_permutation(ts):
    p = jnp.arange(ts)
    t = (p % SUBLANES) * (ts // SUBLANES) + p // SUBLANES
    return (t[:, None] == jnp.arange(ts)[None, :]).astype(BF16)


def _in_proj_lru_kernel(x_ref, gain_ref, wqkv_ref, wg_ref, wf_ref, perm_ref, unperm_ref,
                        cw_ref, cb_ref, wri_ref, br_ref, bi_ref, lam_ref, lgain_ref,
                        qkv_ref, gattn_ref, ylru_ref, flt_ref,
                        xlru_scr, glru_scr, xpad_scr, a_scr, u_scr, h_scr, acum_scr,
                        tail_scr, carry_scr,
                        *, d_attn, chunk, q_scale, ts, tiles_per_seq):
    tm, d = x_ref.shape
    n_sub = tm // ts
    seg = ts // SUBLANES
    head = (CONV_WIDTH - 1) * SUBLANES

    @pl.when(pl.program_id(0) % tiles_per_seq == 0)
    def _():
        tail_scr[...] = jnp.zeros(tail_scr.shape, F32)
        carry_scr[...] = jnp.zeros(carry_scr.shape, F32)

    x = x_ref[...]
    xn = (x * _rms_scale(x) * gain_ref[...]).astype(BF16)
    perm = perm_ref[...]
    xn_seg = jnp.concatenate(
        [jnp.dot(perm, xn[s * ts:(s + 1) * ts, :], preferred_element_type=F32).astype(BF16)
         for s in range(n_sub)], axis=0)

    for dst, lo in ((xlru_scr, d_attn), (glru_scr, 2 * d_attn)):
        for c0 in range(0, d_attn, chunk):
            dst[:, c0:c0 + chunk] = jnp.dot(xn_seg, wg_ref[:, lo + c0:lo + c0 + chunk],
                                            preferred_element_type=F32)

    neg_lam = -lam_ref[...]
    k_unit = -LRU_C * (jnp.maximum(neg_lam, 0.0) + jnp.log1p(jnp.exp(-jnp.abs(neg_lam))))
    c1 = (0.5 * LOG2_E) * k_unit
    hb_r = 0.5 * br_ref[...]
    hb_i = 0.5 * bi_ref[...]
    sub = lax.broadcasted_iota(jnp.int32, (SUBLANES, d), 0)

    def lru_gates(s):
        base = s * ts
        x_s = xlru_scr[base:base + ts, :]
        pad0 = s * (ts + head)
        xpad_scr[pad0 + head:pad0 + head + ts, :] = x_s
        for m in range(CONV_WIDTH - 1):
            grp = slice(m * SUBLANES, (m + 1) * SUBLANES)
            cur = x_s[ts - head + m * SUBLANES:ts - head + (m + 1) * SUBLANES, :]
            xpad_scr[pad0 + m * SUBLANES:pad0 + (m + 1) * SUBLANES, :] = jnp.where(
                sub == 0, pltpu.roll(tail_scr[grp, :], 1, axis=0),
                pltpu.roll(cur, 1, axis=0))
        tail_scr[...] = x_s[ts - head:ts, :]
        xc = cb_ref[...] + sum(
            cw_ref[j:j + 1, :]
            * xpad_scr[pad0 + j * SUBLANES:pad0 + j * SUBLANES + ts, :]
            for j in range(CONV_WIDTH))
        xcb = xc.astype(BF16)
        for n in range(N_LRU_BLOCKS):
            cols = slice(n * LRU_BLOCK, (n + 1) * LRU_BLOCK)
            gates = jnp.dot(xcb[:, cols], wri_ref[n], preferred_element_type=F32)
            tr = jnp.tanh(gates[:, :LRU_BLOCK] + hb_r[:, cols])
            ti = jnp.tanh(gates[:, LRU_BLOCK:] + hb_i[:, cols])
            a = jnp.exp2(tr * c1[:, cols] + c1[:, cols])
            y2 = 1.0 - a * a
            a_scr[base:base + ts, cols] = a
            u_scr[base:base + ts, cols] = (y2 * lax.rsqrt(jnp.maximum(y2, TINY))) * (
                (ti + 1.0) * (0.5 * xc[:, cols]))

    def lru_scan(s):
        base = s * ts
        h = jnp.zeros((SUBLANES, d), F32)
        acc = jnp.ones((SUBLANES, d), F32)
        for k in range(seg):
            rows = slice(base + k * SUBLANES, base + (k + 1) * SUBLANES)
            a = a_scr[rows, :]
            h = a * h + u_scr[rows, :]
            acc = a * acc
            h_scr[rows, :] = h
            acum_scr[rows, :] = acc
        c = carry_scr[0:1, :]
        seg_in = [c]
        for r in range(SUBLANES):
            c = acc[r:r + 1, :] * c + h[r:r + 1, :]
            seg_in.append(c)
        carry_scr[...] = jnp.broadcast_to(seg_in[SUBLANES], carry_scr.shape)
        return jnp.concatenate(seg_in[:SUBLANES], axis=0)

    def lru_finish(s, c_in):
        base = s * ts
        gain = lgain_ref[...]
        for k in range(seg):
            rows = slice(base + k * SUBLANES, base + (k + 1) * SUBLANES)
            h = h_scr[rows, :] + acum_scr[rows, :] * c_in
            hg = 0.5 * glru_scr[rows, :]
            a_scr[rows, :] = (h * _rms_scale(h)) * gain * (hg * (jnp.tanh(hg) + 1.0))
        y = jnp.dot(unperm_ref[...], a_scr[base:base + ts, :].astype(BF16),
                    preferred_element_type=F32)
        ylru_ref[base:base + ts, :] = y.astype(ylru_ref.dtype)

    def project(dst, w_ref, w_lo, c0, scale):
        z = jnp.dot(xn, w_ref[:, w_lo + c0:w_lo + c0 + chunk], preferred_element_type=F32)
        if scale is not None:
            z = z * scale
        dst[:, c0:c0 + chunk] = z.astype(dst.dtype)

    chunks = [(qkv_ref, wqkv_ref, 0, c0, q_scale if c0 < d_attn else None)
              for c0 in range(0, 3 * d_attn, chunk)]
    chunks += [(gattn_ref, wg_ref, 0, c0, None) for c0 in range(0, d_attn, chunk)]
    lru_steps = []
    carried = {}
    for s in range(n_sub):
        lru_steps.append(lambda s=s: lru_gates(s))
    for s in range(n_sub):
        lru_steps.append(lambda s=s: carried.__setitem__(s, lru_scan(s)))
        lru_steps.append(lambda s=s: lru_finish(s, carried[s]))
    for n, args in enumerate(chunks):
        project(*args)
        if n < len(lru_steps):
            lru_steps[n]()
    for step in lru_steps[len(chunks):]:
        step()
    fl = jnp.dot(xn, wf_ref[...], preferred_element_type=F32)
    flt_ref[...] = fl.T[:flt_ref.shape[0], :]


def _in_proj_lru(x2d, pre_gain, w_qkv, w_g, w_f, conv_w, conv_b, w_ri, b_r, b_i, lam,
                 lru_gain, *, n_heads, seq, tm, chunk, ts):
    n, d = x2d.shape
    d_attn = n_heads * HEAD_DIM
    n_sub = tm // ts
    kern = functools.partial(_in_proj_lru_kernel, d_attn=d_attn, chunk=chunk,
                             q_scale=HEAD_DIM ** -0.5 * LOG2_E, ts=ts,
                             tiles_per_seq=seq // tm)
    perm = _segment_permutation(ts)
    const = lambda i: (0, 0)
    row = lambda i: (i, 0)
    single = pl.Buffered(1)
    vec = pl.BlockSpec((1, d), const)
    mat = pl.BlockSpec((ts, ts), const)
    head = (CONV_WIDTH - 1) * SUBLANES
    return pl.pallas_call(
        kern,
        grid=(n // tm,),
        in_specs=[
            pl.BlockSpec((tm, d), row),
            vec,
            pl.BlockSpec(w_qkv.shape, const, pipeline_mode=single),
            pl.BlockSpec(w_g.shape, const, pipeline_mode=single),
            pl.BlockSpec(w_f.shape, const, pipeline_mode=single),
            mat, mat,
            pl.BlockSpec((CONV_WIDTH, d), const),
            vec,
            pl.BlockSpec(w_ri.shape, lambda i: (0, 0, 0)),
            vec, vec, vec, vec,
        ],
        out_specs=[
            pl.BlockSpec((tm, 3 * d_attn), row),
            pl.BlockSpec((tm, d_attn), row),
            pl.BlockSpec((tm, d), row),
            pl.BlockSpec((n_heads, tm), lambda i: (0, i)),
        ],
        out_shape=[
            jax.ShapeDtypeStruct((n, 3 * d_attn), BF16),
            jax.ShapeDtypeStruct((n, d_attn), BF16),
            jax.ShapeDtypeStruct((n, d), BF16),
            jax.ShapeDtypeStruct((n_heads, n), F32),
        ],
        scratch_shapes=[pltpu.VMEM((tm, d), F32),
                        pltpu.VMEM((tm, d), F32),
                        pltpu.VMEM((n_sub * (ts + head), d), F32),
                        pltpu.VMEM((tm, d), F32),
                        pltpu.VMEM((tm, d), F32),
                        pltpu.VMEM((tm, d), F32),
                        pltpu.VMEM((tm, d), F32),
                        pltpu.VMEM((head, d), F32),
                        pltpu.VMEM((SUBLANES, d), F32)],
        compiler_params=pltpu.CompilerParams(
            dimension_semantics=("arbitrary",),
            vmem_limit_bytes=VMEM_LIMIT_BYTES),
        name="in_proj_lru",
    )(x2d, pre_gain, w_qkv, w_g, w_f, perm, perm.T, conv_w, conv_b, w_ri, b_r, b_i, lam,
      lru_gain)


def _forget_cumsum_kernel(flt_ref, bf_ref, c_ref):
    z = flt_ref[...] + bf_ref[...]
    ls = jnp.minimum(z, 0.0) - jnp.log1p(jnp.exp(-jnp.abs(z)))
    seq = ls.shape[1]
    lane = lax.broadcasted_iota(jnp.int32, ls.shape, 1)
    shift = 1
    while shift < seq:
        ls = ls + jnp.where(lane >= shift, pltpu.roll(ls, shift, axis=1), 0.0)
        shift *= 2
    c_ref[...] = ls * LOG2_E


def _forget_cumsum(fl_t, b_f, *, seq):
    n_heads, n = fl_t.shape
    return pl.pallas_call(
        _forget_cumsum_kernel,
        grid=(n // seq,),
        in_specs=[pl.BlockSpec((n_heads, seq), lambda b: (0, b)),
                  pl.BlockSpec((n_heads, 1), lambda b: (0, 0))],
        out_specs=pl.BlockSpec((n_heads, seq), lambda b: (0, b)),
        out_shape=jax.ShapeDtypeStruct((n_heads, n), F32),
        compiler_params=pltpu.CompilerParams(dimension_semantics=("arbitrary",)),
        name="forget_cumsum",
    )(fl_t, b_f)


def _fox_attention_kernel(q_ref, k_ref, v_ref, c_ref, o_ref,
                          s0_scr, s1_scr, s2_scr, s3_scr, r0_scr, r1_scr, r2_scr, r3_scr,
                          vext_scr, m_scr, acc_scr, *, tile, heads_per_step):
    hg = pl.program_id(1)
    tq = 2 * tile
    n_super = q_ref.shape[0] // tq
    half_a = slice(0, tile)
    half_b = slice(tile, tq)
    both = slice(0, tq)

    for hh in range(heads_per_step):
        vext_scr[hh, :, :HEAD_DIM] = v_ref[:, hh * HEAD_DIM:(hh + 1) * HEAD_DIM]
        vext_scr[hh, :, HEAD_DIM:] = jnp.ones((vext_scr.shape[1], HEAD_DIM), BF16)

    def scores(hh, i, j, s_scr, r_scr, rows):
        start = pl.multiple_of(j * tile, tile)
        q0 = pl.multiple_of(i * tq + rows.start, tile)
        n_rows = rows.stop - rows.start
        cols = slice(hh * HEAD_DIM, (hh + 1) * HEAD_DIM)
        s = lax.dot_general(q_ref[pl.ds(q0, n_rows), cols],
                            k_ref[pl.ds(start, tile), cols],
                            (((1,), (1,)), ((), ())), preferred_element_type=F32)
        s = s - c_ref[pl.ds(hg * heads_per_step + hh, 1), pl.ds(start, tile)]
        s_scr[rows, :] = s
        r_scr[rows, :] = jnp.broadcast_to(jnp.max(s, axis=1, keepdims=True),
                                          (n_rows, HEAD_DIM))

    def consume(hh, j, s_scr, r_scr, rows, diag):
        start = pl.multiple_of(j * tile, tile)
        if diag:
            half = tile // 2
            parts = [(slice(rows.start, rows.start + half), half),
                     (slice(rows.start + half, rows.stop), tile)]
            row = lax.broadcasted_iota(jnp.int32, (half, HEAD_DIM), 0)
            lane = lax.broadcasted_iota(jnp.int32, (half, HEAD_DIM), 1)
        else:
            parts = [(rows, tile)]
        for prows, n_keys in parts:
            m_prev = m_scr[prows, :]
            s_chunks = [s_scr[prows, c * HEAD_DIM:(c + 1) * HEAD_DIM]
                        for c in range(n_keys // HEAD_DIM)]
            if diag:
                first = (n_keys - half) // HEAD_DIM
                for c in range(first, len(s_chunks)):
                    keep = row >= lane + (c - first) * HEAD_DIM
                    s_chunks[c] = jnp.where(keep, s_chunks[c], NEG_BIG)
                blk_max = functools.reduce(jnp.maximum, s_chunks)
                m_new = jnp.maximum(m_prev, jnp.max(blk_max, axis=1, keepdims=True))
            else:
                m_new = jnp.maximum(m_prev, r_scr[prows, :])
            alpha = jnp.exp2(m_prev - m_new)
            pb = jnp.concatenate([jnp.exp2(sc - m_new).astype(BF16) for sc in s_chunks],
                                 axis=1)
            pv = jnp.dot(pb, vext_scr[hh, pl.ds(start, n_keys), :],
                         preferred_element_type=F32)
            acc_scr[prows, :HEAD_DIM] = (alpha * acc_scr[prows, :HEAD_DIM]
                                         + pv[:, :HEAD_DIM])
            acc_scr[prows, HEAD_DIM:] = (alpha * acc_scr[prows, HEAD_DIM:]
                                         + pv[:, HEAD_DIM:])
            m_scr[prows, :] = m_new

    def reset():
        m_scr[...] = jnp.full(m_scr.shape, NEG_BIG, F32)
        acc_scr[...] = jnp.zeros(acc_scr.shape, F32)

    assert heads_per_step % 2 == 0
    bufs = ((s0_scr, s1_scr, r0_scr, r1_scr), (s2_scr, s3_scr, r2_scr, r3_scr))

    def super_tile(i, carry):
        for hh in range(heads_per_step):
            sa, sb, ra, rb = bufs[hh % 2]
            nxt_s, _, nxt_r, _ = bufs[(hh + 1) % 2]

            def pair(t, c, hh=hh, sa=sa, sb=sb, ra=ra, rb=rb):
                j = 2 * t
                scores(hh, i, j + 1, sb, rb, both)
                consume(hh, j, sa, ra, both, diag=False)
                scores(hh, i, j + 2, sa, ra, both)
                consume(hh, j + 1, sb, rb, both, diag=False)
                return c

            def two_pairs(t, c, pair=pair):
                return pair(2 * t + 1, pair(2 * t, c))

            reset()
            lax.fori_loop(0, i // 2, two_pairs, 0)

            @pl.when(i % 2 == 1)
            def _(pair=pair):
                pair(i - 1, 0)

            scores(hh, i, 2 * i + 1, sb, rb, half_b)
            if hh + 1 < heads_per_step:
                scores(hh + 1, i, 0, nxt_s, nxt_r, both)
            else:
                scores(0, jnp.minimum(i + 1, n_super - 1), 0, nxt_s, nxt_r, both)
            consume(hh, 2 * i, sa, ra, half_a, diag=True)
            consume(hh, 2 * i, sa, ra, half_b, diag=False)
            consume(hh, 2 * i + 1, sb, rb, half_b, diag=True)
            o_ref[pl.ds(pl.multiple_of(i * tq, tq), tq),
                  hh * HEAD_DIM:(hh + 1) * HEAD_DIM] = (
                acc_scr[:, :HEAD_DIM] / acc_scr[:, HEAD_DIM:]).astype(o_ref.dtype)
        return carry

    scores(0, 0, 0, s0_scr, r0_scr, both)
    lax.fori_loop(0, n_super, super_tile, 0)


def _fox_attention(qkv, c, *, batch, seq, n_heads, tile, heads_per_step):
    n = qkv.shape[0]
    tq = 2 * tile
    hp = heads_per_step
    w = hp * HEAD_DIM
    ng = n_heads // hp
    kern = functools.partial(_fox_attention_kernel, tile=tile, heads_per_step=hp)
    return pl.pallas_call(
        kern,
        grid=(batch, ng),
        in_specs=[
            pl.BlockSpec((seq, w), lambda b, g: (b, g)),
            pl.BlockSpec((seq, w), lambda b, g: (b, ng + g)),
            pl.BlockSpec((seq, w), lambda b, g: (b, 2 * ng + g)),
            pl.BlockSpec((n_heads, seq), lambda b, g: (0, b)),
        ],
        out_specs=pl.BlockSpec((seq, w), lambda b, g: (b, g)),
        out_shape=jax.ShapeDtypeStruct((n, n_heads * HEAD_DIM), BF16),
        scratch_shapes=[pltpu.VMEM((tq, tile), F32)] * 4
                       + [pltpu.VMEM((tq, HEAD_DIM), F32)] * 4
                       + [pltpu.VMEM((hp, seq, 2 * HEAD_DIM), BF16),
                        pltpu.VMEM((tq, HEAD_DIM), F32),
                        pltpu.VMEM((tq, 2 * HEAD_DIM), F32)],
        compiler_params=pltpu.CompilerParams(
            dimension_semantics=("arbitrary", "arbitrary"),
            vmem_limit_bytes=VMEM_LIMIT_BYTES),
        name="fox_attention",
    )(qkv, qkv, qkv, c)


def _out_proj_kernel(oa_ref, ga_ref, yl_ref, x_ref, p_ref, again_ref, wout_ref,
                     pgain_ref, wple_ref, plegain_ref, wpg_ref, bpg_ref, out_ref,
                     *, d_attn, sub_rows):
    subs = [slice(r0, r0 + sub_rows) for r0 in range(0, out_ref.shape[0], sub_rows)]

    def gated_attn(rows):
        oa = oa_ref[rows, :].astype(F32)
        hg = 0.5 * ga_ref[rows, :].astype(F32)
        ya = oa * _rms_scale(oa) * again_ref[...] * (hg * (jnp.tanh(hg) + 1.0))
        return ya.astype(BF16)

    def mixed(rows, ya):
        mix = jnp.dot(ya, wout_ref[0:d_attn, :], preferred_element_type=F32)
        return mix + jnp.dot(yl_ref[rows, :], wout_ref[d_attn:, :],
                             preferred_element_type=F32)

    def embed(rows):
        e = jnp.dot(p_ref[rows, :].astype(BF16), wple_ref[...], preferred_element_type=F32)
        return e * _rms_scale(e) * plegain_ref[...]

    def residual(rows, mix):
        return x_ref[rows, :] + mix * _rms_scale(mix) * pgain_ref[...]

    def gate_logits(h1):
        return jnp.dot(h1.astype(BF16), wpg_ref[...], preferred_element_type=F32)

    ya = [gated_attn(r) for r in subs]
    mix = [mixed(r, y) for r, y in zip(subs, ya)]
    e = [embed(r) for r in subs]
    h1 = [residual(r, m) for r, m in zip(subs, mix)]
    z = [gate_logits(h) for h in h1]
    for r, h, zz, ee in zip(subs, h1, z, e):
        out_ref[r, :] = h + _sigmoid(zz + bpg_ref[...]) * ee


def _out_proj(o_attn, g_attn, y_lru, x2d, p2d, attn_gain, w_out, post_gain, w_ple,
              ple_gain, w_pg, b_pg, *, tm, sub_rows):
    n, d = x2d.shape
    d_attn = o_attn.shape[1]
    d_lru = y_lru.shape[1]
    d_ple = p2d.shape[1]
    kern = functools.partial(_out_proj_kernel, d_attn=d_attn, sub_rows=sub_rows)
    row = lambda i: (i, 0)
    const = lambda i: (0, 0)
    single = pl.Buffered(1)
    return pl.pallas_call(
        kern,
        grid=(n // tm,),
        in_specs=[
            pl.BlockSpec((tm, d_attn), row),
            pl.BlockSpec((tm, d_attn), row),
            pl.BlockSpec((tm, d_lru), row),
            pl.BlockSpec((tm, d), row),
            pl.BlockSpec((tm, d_ple), row),
            pl.BlockSpec((1, d_attn), const),
            pl.BlockSpec(w_out.shape, const, pipeline_mode=single),
            pl.BlockSpec((1, d), const),
            pl.BlockSpec(w_ple.shape, const, pipeline_mode=single),
            pl.BlockSpec((1, d), const),
            pl.BlockSpec(w_pg.shape, const, pipeline_mode=single),
            pl.BlockSpec((1, d), const),
        ],
        out_specs=pl.BlockSpec((tm, d), row),
        out_shape=jax.ShapeDtypeStruct((n, d), F32),
        compiler_params=pltpu.CompilerParams(
            dimension_semantics=("arbitrary",),
            vmem_limit_bytes=VMEM_LIMIT_BYTES),
        name="out_proj",
    )(o_attn, g_attn, y_lru, x2d, p2d, attn_gain, w_out, post_gain, w_ple, ple_gain,
      w_pg, b_pg)


def _layer(h2d, p2d, w_in, b_f, pre_gain, post_gain, conv_w, conv_b, w_rgate, b_rgate,
           w_igate, b_igate, lru_lambda, attn_out_gain, lru_out_gain, w_out, w_ple,
           ple_gain, w_ple_gate, b_ple_gate, *, batch, seq):
    d = h2d.shape[1]
    n_heads = b_f.shape[0]
    d_attn = n_heads * HEAD_DIM
    assert w_in.shape[1] == 4 * d_attn + n_heads + 2 * d and d == d_attn
    fl_lo = 3 * d_attn
    w_qkv = w_in[:, :fl_lo].astype(BF16)
    w_g = w_in[:, fl_lo + n_heads:].astype(BF16)
    w_f = jnp.pad(w_in[:, fl_lo:fl_lo + n_heads],
                  ((0, 0), (0, N_FORGET_PAD - n_heads))).astype(BF16)
    vec = lambda v: v.reshape(1, -1)

    w_ri = (0.5 * jnp.concatenate([w_rgate, w_igate], axis=-1)).astype(BF16)
    qkv, g_attn, y_lru, fl_t = _in_proj_lru(
        h2d, vec(pre_gain), w_qkv, w_g, w_f, conv_w, vec(conv_b), w_ri, vec(b_rgate),
        vec(b_igate), vec(lru_lambda), vec(lru_out_gain), n_heads=n_heads, seq=seq,
        tm=PROJ_ROWS, chunk=PROJ_COLS, ts=LRU_ROWS)
    c = _forget_cumsum(fl_t, b_f.reshape(n_heads, 1), seq=seq)
    o_attn = _fox_attention(qkv, c, batch=batch, seq=seq, n_heads=n_heads,
                            tile=ATTN_KV_TILE, heads_per_step=ATTN_HEADS_PER_STEP)
    return _out_proj(o_attn, g_attn, y_lru, h2d, p2d, vec(attn_out_gain),
                     w_out.astype(BF16), vec(post_gain), w_ple.astype(BF16),
                     vec(ple_gain), w_ple_gate.astype(BF16), vec(b_ple_gate), tm=OUT_ROWS,
                     sub_rows=OUT_SUB_ROWS)


def kernel(x, p, w_in, b_f, pre_gain, post_gain, conv_w, conv_b, w_rgate, b_rgate,
           w_igate, b_igate, lru_lambda, attn_out_gain, lru_out_gain, w_out, w_ple,
           ple_gain, w_ple_gate, b_ple_gate):
    batch, seq, d = x.shape
    h = x.reshape(batch * seq, d)
    for i in range(w_in.shape[0]):
        h = _layer(h, p[i].reshape(batch * seq, -1), w_in[i], b_f[i], pre_gain[i],
                   post_gain[i], conv_w[i], conv_b[i], w_rgate[i], b_rgate[i],
                   w_igate[i], b_igate[i], lru_lambda[i], attn_out_gain[i],
                   lru_out_gain[i], w_out[i], w_ple[i], ple_gain[i], w_ple_gate[i],
                   b_ple_gate[i], batch=batch, seq=seq)
    return h.reshape(batch, seq, d)
```

```python
import functools
import math

import jax
import jax.numpy as jnp
from jax import lax
from jax.experimental import pallas as pl
from jax.experimental.pallas import tpu as pltpu

HEAD_DIM = 128
N_LRU_BLOCKS = 8
LRU_BLOCK = 128
CONV_WIDTH = 4
LRU_C = 8.0
RMS_EPS = 1e-6
N_FORGET_PAD = 128
SUBLANES = 8
VMEM_LIMIT_BYTES = 56 * 1024 * 1024
NEG_BIG = -1e30
TINY = 1e-30
LOG2_E = math.log2(math.e)

F32 = jnp.float32
BF16 = jnp.bfloat16

PROJ_ROWS = 512
PROJ_COLS = 512
LRU_ROWS = 256
ATTN_KV_TILE = 512
ATTN_HEADS_PER_STEP = 2
OUT_ROWS = 1024
OUT_SUB_ROWS = 512


def _sigmoid(x):
    return 0.5 * jnp.tanh(0.5 * x) + 0.5


def _rms_scale(x):
    return lax.rsqrt(jnp.mean(x * x, axis=-1, keepdims=True) + RMS_EPS)


def _segment_permutation(ts):
    p = jnp.arange(ts)
    t = (p % SUBLANES) * (ts // SUBLANES) + p // SUBLANES
    return (t[:, None] == jnp.arange(ts)[None, :]).astype(BF16)


def _in_proj_lru_kernel(x_ref, gain_ref, wqkv_ref, wg_ref, wf_ref, perm_ref, unperm_ref,
                        cw_ref, cb_ref, wri_ref, br_ref, bi_ref, lam_ref, lgain_ref,
                        qkv_ref, gattn_ref, ylru_ref, flt_ref,
                        xlru_scr, glru_scr, xpad_scr, a_scr, u_scr, h_scr, acum_scr,
                        tail_scr, carry_scr,
                        *, d_attn, chunk, q_scale, ts, tiles_per_seq):
    tm, d = x_ref.shape
    n_sub = tm // ts
    seg = ts // SUBLANES
    head = (CONV_WIDTH - 1) * SUBLANES

    @pl.when(pl.program_id(0) % tiles_per_seq == 0)
    def _():
        tail_scr[...] = jnp.zeros(tail_scr.shape, F32)
        carry_scr[...] = jnp.zeros(carry_scr.shape, F32)

    x = x_ref[...]
    xn = (x * _rms_scale(x) * gain_ref[...]).astype(BF16)
    perm = perm_ref[...]
    xn_seg = jnp.concatenate(
        [jnp.dot(perm, xn[s * ts:(s + 1) * ts, :], preferred_element_type=F32).astype(BF16)
         for s in range(n_sub)], axis=0)

    for dst, lo in ((xlru_scr, d_attn), (glru_scr, 2 * d_attn)):
        for c0 in range(0, d_attn, chunk):
            dst[:, c0:c0 + chunk] = jnp.dot(xn_seg, wg_ref[:, lo + c0:lo + c0 + chunk],
                                            preferred_element_type=F32)

    neg_lam = -lam_ref[...]
    k_unit = -LRU_C * (jnp.maximum(neg_lam, 0.0) + jnp.log1p(jnp.exp(-jnp.abs(neg_lam))))
    c1 = (0.5 * LOG2_E) * k_unit
    hb_r = 0.5 * br_ref[...]
    hb_i = 0.5 * bi_ref[...]
    sub = lax.broadcasted_iota(jnp.int32, (SUBLANES, d), 0)

    def lru_gates(s):
        base = s * ts
        x_s = xlru_scr[base:base + ts, :]
        pad0 = s * (ts + head)
        xpad_scr[pad0 + head:pad0 + head + ts, :] = x_s
        for m in range(CONV_WIDTH - 1):
            grp = slice(m * SUBLANES, (m + 1) * SUBLANES)
            cur = x_s[ts - head + m * SUBLANES:ts - head + (m + 1) * SUBLANES, :]
            xpad_scr[pad0 + m * SUBLANES:pad0 + (m + 1) * SUBLANES, :] = jnp.where(
                sub == 0, pltpu.roll(tail_scr[grp, :], 1, axis=0),
                pltpu.roll(cur, 1, axis=0))
        tail_scr[...] = x_s[ts - head:ts, :]
        xc = cb_ref[...] + sum(
            cw_ref[j:j + 1, :]
            * xpad_scr[pad0 + j * SUBLANES:pad0 + j * SUBLANES + ts, :]
            for j in range(CONV_WIDTH))
        xcb = xc.astype(BF16)
        for n in range(N_LRU_BLOCKS):
            cols = slice(n * LRU_BLOCK, (n + 1) * LRU_BLOCK)
            gates = jnp.dot(xcb[:, cols], wri_ref[n], preferred_element_type=F32)
            tr = jnp.tanh(gates[:, :LRU_BLOCK] + hb_r[:, cols])
            ti = jnp.tanh(gates[:, LRU_BLOCK:] + hb_i[:, cols])
            a = jnp.exp2(tr * c1[:, cols] + c1[:, cols])
            y2 = 1.0 - a * a
            a_scr[base:base + ts, cols] = a
            u_scr[base:base + ts, cols] = (y2 * lax.rsqrt(jnp.maximum(y2, TINY))) * (
                (ti + 1.0) * (0.5 * xc[:, cols]))

    def lru_scan(s):
        base = s * ts
        h = jnp.zeros((SUBLANES, d), F32)
        acc = jnp.ones((SUBLANES, d), F32)
        for k in range(seg):
            rows = slice(base + k * SUBLANES, base + (k + 1) * SUBLANES)
            a = a_scr[rows, :]
            h = a * h + u_scr[rows, :]
            acc = a * acc
            h_scr[rows, :] = h
            acum_scr[rows, :] = acc
        c = carry_scr[0:1, :]
        seg_in = [c]
        for r in range(SUBLANES):
            c = acc[r:r + 1, :] * c + h[r:r + 1, :]
            seg_in.append(c)
        carry_scr[...] = jnp.broadcast_to(seg_in[SUBLANES], carry_scr.shape)
        return jnp.concatenate(seg_in[:SUBLANES], axis=0)

    def lru_finish(s, c_in):
        base = s * ts
        gain = lgain_ref[...]
        for k in range(seg):
            rows = slice(base + k * SUBLANES, base + (k + 1) * SUBLANES)
            h = h_scr[rows, :] + acum_scr[rows, :] * c_in
            hg = 0.5 * glru_scr[rows, :]
            a_scr[rows, :] = (h * _rms_scale(h)) * gain * (hg * (jnp.tanh(hg) + 1.0))
        y = jnp.dot(unperm_ref[...], a_scr[base:base + ts, :].astype(BF16),
                    preferred_element_type=F32)
        ylru_ref[base:base + ts, :] = y.astype(ylru_ref.dtype)

    def project(dst, w_ref, w_lo, c0, scale):
        z = jnp.dot(xn, w_ref[:, w_lo + c0:w_lo + c0 + chunk], preferred_element_type=F32)
        if scale is not None:
            z = z * scale
        dst[:, c0:c0 + chunk] = z.astype(dst.dtype)

    chunks = [(qkv_ref, wqkv_ref, 0, c0, q_scale if c0 < d_attn else None)
              for c0 in range(0, 3 * d_attn, chunk)]
    chunks += [(gattn_ref, wg_ref, 0, c0, None) for c0 in range(0, d_attn, chunk)]
    lru_steps = []
    carried = {}
    for s in range(n_sub):
        lru_steps.append(lambda s=s: lru_gates(s))
    for s in range(n_sub):
        lru_steps.append(lambda s=s: carried.__setitem__(s, lru_scan(s)))
        lru_steps.append(lambda s=s: lru_finish(s, carried[s]))
    for n, args in enumerate(chunks):
        project(*args)
        if n < len(lru_steps):
            lru_steps[n]()
    for step in lru_steps[len(chunks):]:
        step()
    fl = jnp.dot(xn, wf_ref[...], preferred_element_type=F32)
    flt_ref[...] = fl.T[:flt_ref.shape[0], :]


def _in_proj_lru(x2d, pre_gain, w_qkv, w_g, w_f, conv_w, conv_b, w_ri, b_r, b_i, lam,
                 lru_gain, *, n_heads, seq, tm, chunk, ts):
    n, d = x2d.shape
    d_attn = n_heads * HEAD_DIM
    n_sub = tm // ts
    kern = functools.partial(_in_proj_lru_kernel, d_attn=d_attn, chunk=chunk,
                             q_scale=HEAD_DIM ** -0.5 * LOG2_E, ts=ts,
                             tiles_per_seq=seq // tm)
    perm = _segment_permutation(ts)
    const = lambda i: (0, 0)
    row = lambda i: (i, 0)
    single = pl.Buffered(1)
    vec = pl.BlockSpec((1, d), const)
    mat = pl.BlockSpec((ts, ts), const)
    head = (CONV_WIDTH - 1) * SUBLANES
    return pl.pallas_call(
        kern,
        grid=(n // tm,),
        in_specs=[
            pl.BlockSpec((tm, d), row),
            vec,
            pl.BlockSpec(w_qkv.shape, const, pipeline_mode=single),
            pl.BlockSpec(w_g.shape, const, pipeline_mode=single),
            pl.BlockSpec(w_f.shape, const, pipeline_mode=single),
            mat, mat,
            pl.BlockSpec((CONV_WIDTH, d), const),
            vec,
            pl.BlockSpec(w_ri.shape, lambda i: (0, 0, 0)),
            vec, vec, vec, vec,
        ],
        out_specs=[
            pl.BlockSpec((tm, 3 * d_attn), row),
            pl.BlockSpec((tm, d_attn), row),
            pl.BlockSpec((tm, d), row),
            pl.BlockSpec((n_heads, tm), lambda i: (0, i)),
        ],
        out_shape=[
            jax.ShapeDtypeStruct((n, 3 * d_attn), BF16),
            jax.ShapeDtypeStruct((n, d_attn), BF16),
            jax.ShapeDtypeStruct((n, d), BF16),
            jax.ShapeDtypeStruct((n_heads, n), F32),
        ],
        scratch_shapes=[pltpu.VMEM((tm, d), F32),
                        pltpu.VMEM((tm, d), F32),
                        pltpu.VMEM((n_sub * (ts + head), d), F32),
                        pltpu.VMEM((tm, d), F32),
                        pltpu.VMEM((tm, d), F32),
                        pltpu.VMEM((tm, d), F32),
                        pltpu.VMEM((tm, d), F32),
                        pltpu.VMEM((head, d), F32),
                        pltpu.VMEM((SUBLANES, d), F32)],
        compiler_params=pltpu.CompilerParams(
            dimension_semantics=("arbitrary",),
            vmem_limit_bytes=VMEM_LIMIT_BYTES),
        name="in_proj_lru",
    )(x2d, pre_gain, w_qkv, w_g, w_f, perm, perm.T, conv_w, conv_b, w_ri, b_r, b_i, lam,
      lru_gain)


def _forget_cumsum_kernel(flt_ref, bf_ref, c_ref):
    z = flt_ref[...] + bf_ref[...]
    ls = jnp.minimum(z, 0.0) - jnp.log1p(jnp.exp(-jnp.abs(z)))
    seq = ls.shape[1]
    lane = lax.broadcasted_iota(jnp.int32, ls.shape, 1)
    shift = 1
    while shift < seq:
        ls = ls + jnp.where(lane >= shift, pltpu.roll(ls, shift, axis=1), 0.0)
        shift *= 2
    c_ref[...] = ls * LOG2_E


def _forget_cumsum(fl_t, b_f, *, seq):
    n_heads, n = fl_t.shape
    return pl.pallas_call(
        _forget_cumsum_kernel,
        grid=(n // seq,),
        in_specs=[pl.BlockSpec((n_heads, seq), lambda b: (0, b)),
                  pl.BlockSpec((n_heads, 1), lambda b: (0, 0))],
        out_specs=pl.BlockSpec((n_heads, seq), lambda b: (0, b)),
        out_shape=jax.ShapeDtypeStruct((n_heads, n), F32),
        compiler_params=pltpu.CompilerParams(dimension_semantics=("arbitrary",)),
        name="forget_cumsum",
    )(fl_t, b_f)


def _fox_attention_kernel(q_ref, k_ref, v_ref, c_ref, o_ref,
                          s0_scr, s1_scr, s2_scr, s3_scr, r0_scr, r1_scr, r2_scr, r3_scr,
                          vext_scr, m_scr, acc_scr, *, tile, heads_per_step):
    hg = pl.program_id(1)
    tq = 2 * tile
    n_super = q_ref.shape[0] // tq
    half_a = slice(0, tile)
    half_b = slice(tile, tq)
    both = slice(0, tq)

    for hh in range(heads_per_step):
        vext_scr[hh, :, :HEAD_DIM] = v_ref[:, hh * HEAD_DIM:(hh + 1) * HEAD_DIM]
        vext_scr[hh, :, HEAD_DIM:] = jnp.ones((vext_scr.shape[1], HEAD_DIM), BF16)

    def scores(hh, i, j, s_scr, r_scr, rows):
        start = j * tile
        q0 = i * tq + rows.start
        n_rows = rows.stop - rows.start
        cols = slice(hh * HEAD_DIM, (hh + 1) * HEAD_DIM)
        s = lax.dot_general(q_ref[pl.ds(q0, n_rows), cols],
                            k_ref[pl.ds(start, tile), cols],
                            (((1,), (1,)), ((), ())), preferred_element_type=F32)
        s = s - c_ref[pl.ds(hg * heads_per_step + hh, 1), pl.ds(start, tile)]
        s_scr[rows, :] = s
        r_scr[rows, :] = jnp.broadcast_to(jnp.max(s, axis=1, keepdims=True),
                                          (n_rows, HEAD_DIM))

    def consume(hh, j, s_scr, r_scr, rows, diag):
        start = j * tile
        if diag:
            half = tile // 2
            parts = [(slice(rows.start, rows.start + half), half),
                     (slice(rows.start + half, rows.stop), tile)]
            row = lax.broadcasted_iota(jnp.int32, (half, HEAD_DIM), 0)
            lane = lax.broadcasted_iota(jnp.int32, (half, HEAD_DIM), 1)
        else:
            parts = [(rows, tile)]
        for prows, n_keys in parts:
            m_prev = m_scr[prows, :]
            s_chunks = [s_scr[prows, c * HEAD_DIM:(c + 1) * HEAD_DIM]
                        for c in range(n_keys // HEAD_DIM)]
            if diag:
                first = (n_keys - half) // HEAD_DIM
                for c in range(first, len(s_chunks)):
                    keep = row >= lane + (c - first) * HEAD_DIM
                    s_chunks[c] = jnp.where(keep, s_chunks[c], NEG_BIG)
                blk_max = functools.reduce(jnp.maximum, s_chunks)
                m_new = jnp.maximum(m_prev, jnp.max(blk_max, axis=1, keepdims=True))
            else:
                m_new = jnp.maximum(m_prev, r_scr[prows, :])
            alpha = jnp.exp2(m_prev - m_new)
            pb = jnp.concatenate([jnp.exp2(sc - m_new).astype(BF16) for sc in s_chunks],
                                 axis=1)
            pv = jnp.dot(pb, vext_scr[hh, pl.ds(start, n_keys), :],
                         preferred_element_type=F32)
            acc_scr[prows, :HEAD_DIM] = (alpha * acc_scr[prows, :HEAD_DIM]
                                         + pv[:, :HEAD_DIM])
            acc_scr[prows, HEAD_DIM:] = (alpha * acc_scr[prows, HEAD_DIM:]
                                         + pv[:, HEAD_DIM:])
            m_scr[prows, :] = m_new

    def reset():
        m_scr[...] = jnp.full(m_scr.shape, NEG_BIG, F32)
        acc_scr[...] = jnp.zeros(acc_scr.shape, F32)

    assert heads_per_step % 2 == 0
    bufs = ((s0_scr, s1_scr, r0_scr, r1_scr), (s2_scr, s3_scr, r2_scr, r3_scr))

    def super_tile(i, carry):
        for hh in range(heads_per_step):
            sa, sb, ra, rb = bufs[hh % 2]
            nxt_s, _, nxt_r, _ = bufs[(hh + 1) % 2]

            def pair(t, c, hh=hh, sa=sa, sb=sb, ra=ra, rb=rb):
                j = 2 * t
                scores(hh, i, j + 1, sb, rb, both)
                consume(hh, j, sa, ra, both, diag=False)
                scores(hh, i, j + 2, sa, ra, both)
                consume(hh, j + 1, sb, rb, both, diag=False)
                return c

            reset()
            for t in range(i):
                pair(t, 0)
            scores(hh, i, 2 * i + 1, sb, rb, half_b)
            if hh + 1 < heads_per_step:
                scores(hh + 1, i, 0, nxt_s, nxt_r, both)
            elif i + 1 < n_super:
                scores(0, i + 1, 0, nxt_s, nxt_r, both)
            consume(hh, 2 * i, sa, ra, half_a, diag=True)
            consume(hh, 2 * i, sa, ra, half_b, diag=False)
            consume(hh, 2 * i + 1, sb, rb, half_b, diag=True)
            o_ref[pl.ds(i * tq, tq),
                  hh * HEAD_DIM:(hh + 1) * HEAD_DIM] = (
                acc_scr[:, :HEAD_DIM] / acc_scr[:, HEAD_DIM:]).astype(o_ref.dtype)
        return carry

    scores(0, 0, 0, s0_scr, r0_scr, both)
    for i in range(n_super):
        super_tile(i, 0)


def _fox_attention(qkv, c, *, batch, seq, n_heads, tile, heads_per_step):
    n = qkv.shape[0]
    tq = 2 * tile
    hp = heads_per_step
    w = hp * HEAD_DIM
    ng = n_heads // hp
    kern = functools.partial(_fox_attention_kernel, tile=tile, heads_per_step=hp)
    return pl.pallas_call(
        kern,
        grid=(batch, ng),
        in_specs=[
            pl.BlockSpec((seq, w), lambda b, g: (b, g)),
            pl.BlockSpec((seq, w), lambda b, g: (b, ng + g)),
            pl.BlockSpec((seq, w), lambda b, g: (b, 2 * ng + g)),
            pl.BlockSpec((n_heads, seq), lambda b, g: (0, b)),
        ],
        out_specs=pl.BlockSpec((seq, w), lambda b, g: (b, g)),
        out_shape=jax.ShapeDtypeStruct((n, n_heads * HEAD_DIM), BF16),
        scratch_shapes=[pltpu.VMEM((tq, tile), F32)] * 4
                       + [pltpu.VMEM((tq, HEAD_DIM), F32)] * 4
                       + [pltpu.VMEM((hp, seq, 2 * HEAD_DIM), BF16),
                        pltpu.VMEM((tq, HEAD_DIM), F32),
                        pltpu.VMEM((tq, 2 * HEAD_DIM), F32)],
        compiler_params=pltpu.CompilerParams(
            dimension_semantics=("arbitrary", "arbitrary"),
            vmem_limit_bytes=VMEM_LIMIT_BYTES),
        name="fox_attention",
    )(qkv, qkv, qkv, c)


def _out_proj_kernel(oa_ref, ga_ref, yl_ref, x_ref, p_ref, again_ref, wout_ref,
                     pgain_ref, wple_ref, plegain_ref, wpg_ref, bpg_ref, out_ref,
                     *, d_attn, sub_rows):
    subs = [slice(r0, r0 + sub_rows) for r0 in range(0, out_ref.shape[0], sub_rows)]

    def gated_attn(rows):
        oa = oa_ref[rows, :].astype(F32)
        hg = 0.5 * ga_ref[rows, :].astype(F32)
        ya = oa * _rms_scale(oa) * again_ref[...] * (hg * (jnp.tanh(hg) + 1.0))
        return ya.astype(BF16)

    def mixed(rows, ya):
        mix = jnp.dot(ya, wout_ref[0:d_attn, :], preferred_element_type=F32)
        return mix + jnp.dot(yl_ref[rows, :], wout_ref[d_attn:, :],
                             preferred_element_type=F32)

    def embed(rows):
        e = jnp.dot(p_ref[rows, :].astype(BF16), wple_ref[...], preferred_element_type=F32)
        return e * _rms_scale(e) * plegain_ref[...]

    def residual(rows, mix):
        return x_ref[rows, :] + mix * _rms_scale(mix) * pgain_ref[...]

    def gate_logits(h1):
        return jnp.dot(h1.astype(BF16), wpg_ref[...], preferred_element_type=F32)

    ya = [gated_attn(r) for r in subs]
    mix = [mixed(r, y) for r, y in zip(subs, ya)]
    e = [embed(r) for r in subs]
    h1 = [residual(r, m) for r, m in zip(subs, mix)]
    z = [gate_logits(h) for h in h1]
    for r, h, zz, ee in zip(subs, h1, z, e):
        out_ref[r, :] = h + _sigmoid(zz + bpg_ref[...]) * ee


def _out_proj(o_attn, g_attn, y_lru, x2d, p2d, attn_gain, w_out, post_gain, w_ple,
              ple_gain, w_pg, b_pg, *, tm, sub_rows):
    n, d = x2d.shape
    d_attn = o_attn.shape[1]
    d_lru = y_lru.shape[1]
    d_ple = p2d.shape[1]
    kern = functools.partial(_out_proj_kernel, d_attn=d_attn, sub_rows=sub_rows)
    row = lambda i: (i, 0)
    const = lambda i: (0, 0)
    single = pl.Buffered(1)
    return pl.pallas_call(
        kern,
        grid=(n // tm,),
        in_specs=[
            pl.BlockSpec((tm, d_attn), row),
            pl.BlockSpec((tm, d_attn), row),
            pl.BlockSpec((tm, d_lru), row),
            pl.BlockSpec((tm, d), row),
            pl.BlockSpec((tm, d_ple), row),
            pl.BlockSpec((1, d_attn), const),
            pl.BlockSpec(w_out.shape, const, pipeline_mode=single),
            pl.BlockSpec((1, d), const),
            pl.BlockSpec(w_ple.shape, const, pipeline_mode=single),
            pl.BlockSpec((1, d), const),
            pl.BlockSpec(w_pg.shape, const, pipeline_mode=single),
            pl.BlockSpec((1, d), const),
        ],
        out_specs=pl.BlockSpec((tm, d), row),
        out_shape=jax.ShapeDtypeStruct((n, d), F32),
        compiler_params=pltpu.CompilerParams(
            dimension_semantics=("arbitrary",),
            vmem_limit_bytes=VMEM_LIMIT_BYTES),
        name="out_proj",
    )(o_attn, g_attn, y_lru, x2d, p2d, attn_gain, w_out, post_gain, w_ple, ple_gain,
      w_pg, b_pg)


def _layer(h2d, p2d, w_in, b_f, pre_gain, post_gain, conv_w, conv_b, w_rgate, b_rgate,
           w_igate, b_igate, lru_lambda, attn_out_gain, lru_out_gain, w_out, w_ple,
           ple_gain, w_ple_gate, b_ple_gate, *, batch, seq):
    d = h2d.shape[1]
    n_heads = b_f.shape[0]
    d_attn = n_heads * HEAD_DIM
    assert w_in.shape[1] == 4 * d_attn + n_heads + 2 * d and d == d_attn
    fl_lo = 3 * d_attn
    w_qkv = w_in[:, :fl_lo].astype(BF16)
    w_g = w_in[:, fl_lo + n_heads:].astype(BF16)
    w_f = jnp.pad(w_in[:, fl_lo:fl_lo + n_heads],
                  ((0, 0), (0, N_FORGET_PAD - n_heads))).astype(BF16)
    vec = lambda v: v.reshape(1, -1)

    w_ri = (0.5 * jnp.concatenate([w_rgate, w_igate], axis=-1)).astype(BF16)
    qkv, g_attn, y_lru, fl_t = _in_proj_lru(
        h2d, vec(pre_gain), w_qkv, w_g, w_f, conv_w, vec(conv_b), w_ri, vec(b_rgate),
        vec(b_igate), vec(lru_lambda), vec(lru_out_gain), n_heads=n_heads, seq=seq,
        tm=PROJ_ROWS, chunk=PROJ_COLS, ts=LRU_ROWS)
    c = _forget_cumsum(fl_t, b_f.reshape(n_heads, 1), seq=seq)
    o_attn = _fox_attention(qkv, c, batch=batch, seq=seq, n_heads=n_heads,
                            tile=ATTN_KV_TILE, heads_per_step=ATTN_HEADS_PER_STEP)
    return _out_proj(o_attn, g_attn, y_lru, h2d, p2d, vec(attn_out_gain),
                     w_out.astype(BF16), vec(post_gain), w_ple.astype(BF16),
                     vec(ple_gain), w_ple_gate.astype(BF16), vec(b_ple_gate), tm=OUT_ROWS,
                     sub_rows=OUT_SUB_ROWS)


def kernel(x, p, w_in, b_f, pre_gain, post_gain, conv_w, conv_b, w_rgate, b_rgate,
           w_igate, b_igate, lru_lambda, attn_out_gain, lru_out_gain, w_out, w_ple,
           ple_gain, w_ple_gate, b_ple_gate):
    batch, seq, d = x.shape
    h = x.reshape(batch * seq, d)
    for i in range(w_in.shape[0]):
        h = _layer(h, p[i].reshape(batch * seq, -1), w_in[i], b_f[i], pre_gain[i],
                   post_gain[i], conv_w[i], conv_b[i], w_rgate[i], b_rgate[i],
                   w_igate[i], b_igate[i], lru_lambda[i], attn_out_gain[i],
                   lru_out_gain[i], w_out[i], w_ple[i], ple_gain[i], w_ple_gate[i],
                   b_ple_gate[i], batch=batch, seq=seq)
    return h.reshape(batch, seq, d)
```

```python
import functools
import math

import jax
import jax.numpy as jnp
from jax import lax
from jax.experimental import pallas as pl
from jax.experimental.pallas import tpu as pltpu

HEAD_DIM = 128
N_LRU_BLOCKS = 8
LRU_BLOCK = 128
CONV_WIDTH = 4
LRU_C = 8.0
RMS_EPS = 1e-6
N_FORGET_PAD = 128
SUBLANES = 8
VMEM_LIMIT_BYTES = 56 * 1024 * 1024
NEG_BIG = -1e30
TINY = 1e-30
LOG2_E = math.log2(math.e)

F32 = jnp.float32
BF16 = jnp.bfloat16

PROJ_ROWS = 1024
PROJ_COLS = 512
LRU_ROWS = 256
ATTN_KV_TILE = 512
ATTN_HEADS_PER_STEP = 2
OUT_ROWS = 1024
OUT_SUB_ROWS = 512


def _sigmoid(x):
    return 0.5 * jnp.tanh(0.5 * x) + 0.5


def _rms_scale(x):
    return lax.rsqrt(jnp.mean(x * x, axis=-1, keepdims=True) + RMS_EPS)


def _segment_permutation(ts):
    p = jnp.arange(ts)
    t = (p % SUBLANES) * (ts // SUBLANES) + p // SUBLANES
    return (t[:, None] == jnp.arange(ts)[None, :]).astype(BF16)


def _in_proj_lru_kernel(x_ref, gain_ref, wqkv_ref, wg_ref, wf_ref, perm_ref, unperm_ref,
                        cw_ref, cb_ref, wri_ref, br_ref, bi_ref, lam_ref, lgain_ref,
                        qkv_ref, gattn_ref, ylru_ref, flt_ref,
                        xlru_scr, glru_scr, xpad_scr, a_scr, u_scr, h_scr, acum_scr,
                        tail_scr, carry_scr,
                        *, d_attn, chunk, q_scale, ts, tiles_per_seq):
    tm, d = x_ref.shape
    n_sub = tm // ts
    seg = ts // SUBLANES
    head = (CONV_WIDTH - 1) * SUBLANES

    @pl.when(pl.program_id(0) % tiles_per_seq == 0)
    def _():
        tail_scr[...] = jnp.zeros(tail_scr.shape, F32)
        carry_scr[...] = jnp.zeros(carry_scr.shape, F32)

    x = x_ref[...]
    xn = (x * _rms_scale(x) * gain_ref[...]).astype(BF16)
    perm = perm_ref[...]
    xn_seg = jnp.concatenate(
        [jnp.dot(perm, xn[s * ts:(s + 1) * ts, :], preferred_element_type=F32).astype(BF16)
         for s in range(n_sub)], axis=0)

    for dst, lo in ((xlru_scr, d_attn), (glru_scr, 2 * d_attn)):
        for c0 in range(0, d_attn, chunk):
            dst[:, c0:c0 + chunk] = jnp.dot(xn_seg, wg_ref[:, lo + c0:lo + c0 + chunk],
                                            preferred_element_type=F32)

    neg_lam = -lam_ref[...]
    k_unit = -LRU_C * (jnp.maximum(neg_lam, 0.0) + jnp.log1p(jnp.exp(-jnp.abs(neg_lam))))
    c1 = (0.5 * LOG2_E) * k_unit
    hb_r = 0.5 * br_ref[...]
    hb_i = 0.5 * bi_ref[...]
    sub = lax.broadcasted_iota(jnp.int32, (SUBLANES, d), 0)

    def lru_gates(s):
        base = s * ts
        x_s = xlru_scr[base:base + ts, :]
        pad0 = s * (ts + head)
        xpad_scr[pad0 + head:pad0 + head + ts, :] = x_s
        for m in range(CONV_WIDTH - 1):
            grp = slice(m * SUBLANES, (m + 1) * SUBLANES)
            cur = x_s[ts - head + m * SUBLANES:ts - head + (m + 1) * SUBLANES, :]
            xpad_scr[pad0 + m * SUBLANES:pad0 + (m + 1) * SUBLANES, :] = jnp.where(
                sub == 0, pltpu.roll(tail_scr[grp, :], 1, axis=0),
                pltpu.roll(cur, 1, axis=0))
        tail_scr[...] = x_s[ts - head:ts, :]
        xc = cb_ref[...] + sum(
            cw_ref[j:j + 1, :]
            * xpad_scr[pad0 + j * SUBLANES:pad0 + j * SUBLANES + ts, :]
            for j in range(CONV_WIDTH))
        xcb = xc.astype(BF16)
        for n in range(N_LRU_BLOCKS):
            cols = slice(n * LRU_BLOCK, (n + 1) * LRU_BLOCK)
            gates = jnp.dot(xcb[:, cols], wri_ref[n], preferred_element_type=F32)
            tr = jnp.tanh(gates[:, :LRU_BLOCK] + hb_r[:, cols])
            ti = jnp.tanh(gates[:, LRU_BLOCK:] + hb_i[:, cols])
            a = jnp.exp2(tr * c1[:, cols] + c1[:, cols])
            y2 = 1.0 - a * a
            a_scr[base:base + ts, cols] = a
            u_scr[base:base + ts, cols] = (y2 * lax.rsqrt(jnp.maximum(y2, TINY))) * (
                (ti + 1.0) * (0.5 * xc[:, cols]))

    def lru_scan(s):
        base = s * ts
        h = jnp.zeros((SUBLANES, d), F32)
        acc = jnp.ones((SUBLANES, d), F32)
        for k in range(seg):
            rows = slice(base + k * SUBLANES, base + (k + 1) * SUBLANES)
            a = a_scr[rows, :]
            h = a * h + u_scr[rows, :]
            acc = a * acc
            h_scr[rows, :] = h
            acum_scr[rows, :] = acc
        c = carry_scr[0:1, :]
        seg_in = [c]
        for r in range(SUBLANES):
            c = acc[r:r + 1, :] * c + h[r:r + 1, :]
            seg_in.append(c)
        carry_scr[...] = jnp.broadcast_to(seg_in[SUBLANES], carry_scr.shape)
        return jnp.concatenate(seg_in[:SUBLANES], axis=0)

    def lru_finish(s, c_in):
        base = s * ts
        gain = lgain_ref[...]
        for k in range(seg):
            rows = slice(base + k * SUBLANES, base + (k + 1) * SUBLANES)
            h = h_scr[rows, :] + acum_scr[rows, :] * c_in
            hg = 0.5 * glru_scr[rows, :]
            a_scr[rows, :] = (h * _rms_scale(h)) * gain * (hg * (jnp.tanh(hg) + 1.0))
        y = jnp.dot(unperm_ref[...], a_scr[base:base + ts, :].astype(BF16),
                    preferred_element_type=F32)
        ylru_ref[base:base + ts, :] = y.astype(ylru_ref.dtype)

    def project(dst, w_ref, w_lo, c0, scale):
        z = jnp.dot(xn, w_ref[:, w_lo + c0:w_lo + c0 + chunk], preferred_element_type=F32)
        if scale is not None:
            z = z * scale
        dst[:, c0:c0 + chunk] = z.astype(dst.dtype)

    chunks = [(qkv_ref, wqkv_ref, 0, c0, q_scale if c0 < d_attn else None)
              for c0 in range(0, 3 * d_attn, chunk)]
    chunks += [(gattn_ref, wg_ref, 0, c0, None) for c0 in range(0, d_attn, chunk)]
    lru_steps = []
    carried = {}
    for s in range(n_sub):
        lru_steps.append(lambda s=s: lru_gates(s))
    for s in range(n_sub):
        lru_steps.append(lambda s=s: carried.__setitem__(s, lru_scan(s)))
        lru_steps.append(lambda s=s: lru_finish(s, carried[s]))
    for n, args in enumerate(chunks):
        project(*args)
        if n < len(lru_steps):
            lru_steps[n]()
    for step in lru_steps[len(chunks):]:
        step()
    fl = jnp.dot(xn, wf_ref[...], preferred_element_type=F32)
    flt_ref[...] = fl.T[:flt_ref.shape[0], :]


def _in_proj_lru(x2d, pre_gain, w_qkv, w_g, w_f, conv_w, conv_b, w_ri, b_r, b_i, lam,
                 lru_gain, *, n_heads, seq, tm, chunk, ts):
    n, d = x2d.shape
    d_attn = n_heads * HEAD_DIM
    n_sub = tm // ts
    kern = functools.partial(_in_proj_lru_kernel, d_attn=d_attn, chunk=chunk,
                             q_scale=HEAD_DIM ** -0.5 * LOG2_E, ts=ts,
                             tiles_per_seq=seq // tm)
    perm = _segment_permutation(ts)
    const = lambda i: (0, 0)
    row = lambda i: (i, 0)
    single = pl.Buffered(1)
    vec = pl.BlockSpec((1, d), const)
    mat = pl.BlockSpec((ts, ts), const)
    head = (CONV_WIDTH - 1) * SUBLANES
    return pl.pallas_call(
        kern,
        grid=(n // tm,),
        in_specs=[
            pl.BlockSpec((tm, d), row),
            vec,
            pl.BlockSpec(w_qkv.shape, const, pipeline_mode=single),
            pl.BlockSpec(w_g.shape, const, pipeline_mode=single),
            pl.BlockSpec(w_f.shape, const, pipeline_mode=single),
            mat, mat,
            pl.BlockSpec((CONV_WIDTH, d), const),
            vec,
            pl.BlockSpec(w_ri.shape, lambda i: (0, 0, 0)),
            vec, vec, vec, vec,
        ],
        out_specs=[
            pl.BlockSpec((tm, 3 * d_attn), row),
            pl.BlockSpec((tm, d_attn), row),
            pl.BlockSpec((tm, d), row),
            pl.BlockSpec((n_heads, tm), lambda i: (0, i)),
        ],
        out_shape=[
            jax.ShapeDtypeStruct((n, 3 * d_attn), BF16),
            jax.ShapeDtypeStruct((n, d_attn), BF16),
            jax.ShapeDtypeStruct((n, d), BF16),
            jax.ShapeDtypeStruct((n_heads, n), F32),
        ],
        scratch_shapes=[pltpu.VMEM((tm, d), F32),
                        pltpu.VMEM((tm, d), F32),
                        pltpu.VMEM((n_sub * (ts + head), d), F32),
                        pltpu.VMEM((tm, d), F32),
                        pltpu.VMEM((tm, d), F32),
                        pltpu.VMEM((tm, d), F32),
                        pltpu.VMEM((tm, d), F32),
                        pltpu.VMEM((head, d), F32),
                        pltpu.VMEM((SUBLANES, d), F32)],
        compiler_params=pltpu.CompilerParams(
            dimension_semantics=("arbitrary",),
            vmem_limit_bytes=VMEM_LIMIT_BYTES),
        name="in_proj_lru",
    )(x2d, pre_gain, w_qkv, w_g, w_f, perm, perm.T, conv_w, conv_b, w_ri, b_r, b_i, lam,
      lru_gain)


def _forget_cumsum_kernel(flt_ref, bf_ref, c_ref):
    z = flt_ref[...] + bf_ref[...]
    ls = jnp.minimum(z, 0.0) - jnp.log1p(jnp.exp(-jnp.abs(z)))
    seq = ls.shape[1]
    lane = lax.broadcasted_iota(jnp.int32, ls.shape, 1)
    shift = 1
    while shift < seq:
        ls = ls + jnp.where(lane >= shift, pltpu.roll(ls, shift, axis=1), 0.0)
        shift *= 2
    c_ref[...] = ls * LOG2_E


def _forget_cumsum(fl_t, b_f, *, seq):
    n_heads, n = fl_t.shape
    return pl.pallas_call(
        _forget_cumsum_kernel,
        grid=(n // seq,),
        in_specs=[pl.BlockSpec((n_heads, seq), lambda b: (0, b)),
                  pl.BlockSpec((n_heads, 1), lambda b: (0, 0))],
        out_specs=pl.BlockSpec((n_heads, seq), lambda b: (0, b)),
        out_shape=jax.ShapeDtypeStruct((n_heads, n), F32),
        compiler_params=pltpu.CompilerParams(dimension_semantics=("arbitrary",)),
        name="forget_cumsum",
    )(fl_t, b_f)


def _fox_attention_kernel(q_ref, k_ref, v_ref, c_ref, o_ref,
                          s0_scr, s1_scr, s2_scr, s3_scr, r0_scr, r1_scr, r2_scr, r3_scr,
                          vext_scr, m_scr, acc_scr, *, tile, heads_per_step):
    hg = pl.program_id(1)
    tq = 2 * tile
    n_super = q_ref.shape[0] // tq
    half_a = slice(0, tile)
    half_b = slice(tile, tq)
    both = slice(0, tq)

    for hh in range(heads_per_step):
        vext_scr[hh, :, :HEAD_DIM] = v_ref[:, hh * HEAD_DIM:(hh + 1) * HEAD_DIM]
        vext_scr[hh, :, HEAD_DIM:] = jnp.ones((vext_scr.shape[1], HEAD_DIM), BF16)

    def scores(hh, i, j, s_scr, r_scr, rows):
        start = j * tile
        q0 = i * tq + rows.start
        n_rows = rows.stop - rows.start
        cols = slice(hh * HEAD_DIM, (hh + 1) * HEAD_DIM)
        s = lax.dot_general(q_ref[pl.ds(q0, n_rows), cols],
                            k_ref[pl.ds(start, tile), cols],
                            (((1,), (1,)), ((), ())), preferred_element_type=F32)
        s = s - c_ref[pl.ds(hg * heads_per_step + hh, 1), pl.ds(start, tile)]
        s_scr[rows, :] = s
        r_scr[rows, :] = jnp.broadcast_to(jnp.max(s, axis=1, keepdims=True),
                                          (n_rows, HEAD_DIM))

    def consume(hh, j, s_scr, r_scr, rows, diag):
        start = j * tile
        if diag:
            half = tile // 2
            parts = [(slice(rows.start, rows.start + half), half),
                     (slice(rows.start + half, rows.stop), tile)]
            row = lax.broadcasted_iota(jnp.int32, (half, HEAD_DIM), 0)
            lane = lax.broadcasted_iota(jnp.int32, (half, HEAD_DIM), 1)
        else:
            parts = [(rows, tile)]
        for prows, n_keys in parts:
            m_prev = m_scr[prows, :]
            s_chunks = [s_scr[prows, c * HEAD_DIM:(c + 1) * HEAD_DIM]
                        for c in range(n_keys // HEAD_DIM)]
            if diag:
                first = (n_keys - half) // HEAD_DIM
                for c in range(first, len(s_chunks)):
                    keep = row >= lane + (c - first) * HEAD_DIM
                    s_chunks[c] = jnp.where(keep, s_chunks[c], NEG_BIG)
                blk_max = functools.reduce(jnp.maximum, s_chunks)
                m_new = jnp.maximum(m_prev, jnp.max(blk_max, axis=1, keepdims=True))
            else:
                m_new = jnp.maximum(m_prev, r_scr[prows, :])
            alpha = jnp.exp2(m_prev - m_new)
            pb = jnp.concatenate([jnp.exp2(sc - m_new).astype(BF16) for sc in s_chunks],
                                 axis=1)
            pv = jnp.dot(pb, vext_scr[hh, pl.ds(start, n_keys), :],
                         preferred_element_type=F32)
            acc_scr[prows, :HEAD_DIM] = (alpha * acc_scr[prows, :HEAD_DIM]
                                         + pv[:, :HEAD_DIM])
            acc_scr[prows, HEAD_DIM:] = (alpha * acc_scr[prows, HEAD_DIM:]
                                         + pv[:, HEAD_DIM:])
            m_scr[prows, :] = m_new

    def reset():
        m_scr[...] = jnp.full(m_scr.shape, NEG_BIG, F32)
        acc_scr[...] = jnp.zeros(acc_scr.shape, F32)

    assert heads_per_step % 2 == 0
    bufs = ((s0_scr, s1_scr, r0_scr, r1_scr), (s2_scr, s3_scr, r2_scr, r3_scr))

    def super_tile(i, carry):
        for hh in range(heads_per_step):
            sa, sb, ra, rb = bufs[hh % 2]
            nxt_s, _, nxt_r, _ = bufs[(hh + 1) % 2]

            def pair(t, c, hh=hh, sa=sa, sb=sb, ra=ra, rb=rb):
                j = 2 * t
                scores(hh, i, j + 1, sb, rb, both)
                consume(hh, j, sa, ra, both, diag=False)
                scores(hh, i, j + 2, sa, ra, both)
                consume(hh, j + 1, sb, rb, both, diag=False)
                return c

            reset()
            for t in range(i):
                pair(t, 0)
            scores(hh, i, 2 * i + 1, sb, rb, half_b)
            if hh + 1 < heads_per_step:
                scores(hh + 1, i, 0, nxt_s, nxt_r, both)
            elif i + 1 < n_super:
                scores(0, i + 1, 0, nxt_s, nxt_r, both)
            consume(hh, 2 * i, sa, ra, half_a, diag=True)
            consume(hh, 2 * i, sa, ra, half_b, diag=False)
            consume(hh, 2 * i + 1, sb, rb, half_b, diag=True)
            o_ref[pl.ds(i * tq, tq),
                  hh * HEAD_DIM:(hh + 1) * HEAD_DIM] = (
                acc_scr[:, :HEAD_DIM] / acc_scr[:, HEAD_DIM:]).astype(o_ref.dtype)
        return carry

    scores(0, 0, 0, s0_scr, r0_scr, both)
    for i in range(n_super):
        super_tile(i, 0)


def _fox_attention(qkv, c, *, batch, seq, n_heads, tile, heads_per_step):
    n = qkv.shape[0]
    tq = 2 * tile
    hp = heads_per_step
    w = hp * HEAD_DIM
    ng = n_heads // hp
    kern = functools.partial(_fox_attention_kernel, tile=tile, heads_per_step=hp)
    return pl.pallas_call(
        kern,
        grid=(batch, ng),
        in_specs=[
            pl.BlockSpec((seq, w), lambda b, g: (b, g)),
            pl.BlockSpec((seq, w), lambda b, g: (b, ng + g)),
            pl.BlockSpec((seq, w), lambda b, g: (b, 2 * ng + g)),
            pl.BlockSpec((n_heads, seq), lambda b, g: (0, b)),
        ],
        out_specs=pl.BlockSpec((seq, w), lambda b, g: (b, g)),
        out_shape=jax.ShapeDtypeStruct((n, n_heads * HEAD_DIM), BF16),
        scratch_shapes=[pltpu.VMEM((tq, tile), F32)] * 4
                       + [pltpu.VMEM((tq, HEAD_DIM), F32)] * 4
                       + [pltpu.VMEM((hp, seq, 2 * HEAD_DIM), BF16),
                        pltpu.VMEM((tq, HEAD_DIM), F32),
                        pltpu.VMEM((tq, 2 * HEAD_DIM), F32)],
        compiler_params=pltpu.CompilerParams(
            dimension_semantics=("arbitrary", "arbitrary"),
            vmem_limit_bytes=VMEM_LIMIT_BYTES),
        name="fox_attention",
    )(qkv, qkv, qkv, c)


def _out_proj_kernel(oa_ref, ga_ref, yl_ref, x_ref, p_ref, again_ref, wout_ref,
                     pgain_ref, wple_ref, plegain_ref, wpg_ref, bpg_ref, out_ref,
                     *, d_attn, sub_rows):
    subs = [slice(r0, r0 + sub_rows) for r0 in range(0, out_ref.shape[0], sub_rows)]

    def gated_attn(rows):
        oa = oa_ref[rows, :].astype(F32)
        hg = 0.5 * ga_ref[rows, :].astype(F32)
        ya = oa * _rms_scale(oa) * again_ref[...] * (hg * (jnp.tanh(hg) + 1.0))
        return ya.astype(BF16)

    def mixed(rows, ya):
        mix = jnp.dot(ya, wout_ref[0:d_attn, :], preferred_element_type=F32)
        return mix + jnp.dot(yl_ref[rows, :], wout_ref[d_attn:, :],
                             preferred_element_type=F32)

    def embed(rows):
        e = jnp.dot(p_ref[rows, :].astype(BF16), wple_ref[...], preferred_element_type=F32)
        return e * _rms_scale(e) * plegain_ref[...]

    def residual(rows, mix):
        return x_ref[rows, :] + mix * _rms_scale(mix) * pgain_ref[...]

    def gate_logits(h1):
        return jnp.dot(h1.astype(BF16), wpg_ref[...], preferred_element_type=F32)

    ya = [gated_attn(r) for r in subs]
    mix = [mixed(r, y) for r, y in zip(subs, ya)]
    e = [embed(r) for r in subs]
    h1 = [residual(r, m) for r, m in zip(subs, mix)]
    z = [gate_logits(h) for h in h1]
    for r, h, zz, ee in zip(subs, h1, z, e):
        out_ref[r, :] = h + _sigmoid(zz + bpg_ref[...]) * ee


def _out_proj(o_attn, g_attn, y_lru, x2d, p2d, attn_gain, w_out, post_gain, w_ple,
              ple_gain, w_pg, b_pg, *, tm, sub_rows):
    n, d = x2d.shape
    d_attn = o_attn.shape[1]
    d_lru = y_lru.shape[1]
    d_ple = p2d.shape[1]
    kern = functools.partial(_out_proj_kernel, d_attn=d_attn, sub_rows=sub_rows)
    row = lambda i: (i, 0)
    const = lambda i: (0, 0)
    single = pl.Buffered(1)
    return pl.pallas_call(
        kern,
        grid=(n // tm,),
        in_specs=[
            pl.BlockSpec((tm, d_attn), row),
            pl.BlockSpec((tm, d_attn), row),
            pl.BlockSpec((tm, d_lru), row),
            pl.BlockSpec((tm, d), row),
            pl.BlockSpec((tm, d_ple), row),
            pl.BlockSpec((1, d_attn), const),
            pl.BlockSpec(w_out.shape, const, pipeline_mode=single),
            pl.BlockSpec((1, d), const),
            pl.BlockSpec(w_ple.shape, const, pipeline_mode=single),
            pl.BlockSpec((1, d), const),
            pl.BlockSpec(w_pg.shape, const, pipeline_mode=single),
            pl.BlockSpec((1, d), const),
        ],
        out_specs=pl.BlockSpec((tm, d), row),
        out_shape=jax.ShapeDtypeStruct((n, d), F32),
        compiler_params=pltpu.CompilerParams(
            dimension_semantics=("arbitrary",),
            vmem_limit_bytes=VMEM_LIMIT_BYTES),
        name="out_proj",
    )(o_attn, g_attn, y_lru, x2d, p2d, attn_gain, w_out, post_gain, w_ple, ple_gain,
      w_pg, b_pg)


def _layer(h2d, p2d, w_in, b_f, pre_gain, post_gain, conv_w, conv_b, w_rgate, b_rgate,
           w_igate, b_igate, lru_lambda, attn_out_gain, lru_out_gain, w_out, w_ple,
           ple_gain, w_ple_gate, b_ple_gate, *, batch, seq):
    d = h2d.shape[1]
    n_heads = b_f.shape[0]
    d_attn = n_heads * HEAD_DIM
    assert w_in.shape[1] == 4 * d_attn + n_heads + 2 * d and d == d_attn
    fl_lo = 3 * d_attn
    w_qkv = w_in[:, :fl_lo].astype(BF16)
    w_g = w_in[:, fl_lo + n_heads:].astype(BF16)
    w_f = jnp.pad(w_in[:, fl_lo:fl_lo + n_heads],
                  ((0, 0), (0, N_FORGET_PAD - n_heads))).astype(BF16)
    vec = lambda v: v.reshape(1, -1)

    w_ri = (0.5 * jnp.concatenate([w_rgate, w_igate], axis=-1)).astype(BF16)
    qkv, g_attn, y_lru, fl_t = _in_proj_lru(
        h2d, vec(pre_gain), w_qkv, w_g, w_f, conv_w, vec(conv_b), w_ri, vec(b_rgate),
        vec(b_igate), vec(lru_lambda), vec(lru_out_gain), n_heads=n_heads, seq=seq,
        tm=PROJ_ROWS, chunk=PROJ_COLS, ts=LRU_ROWS)
    c = _forget_cumsum(fl_t, b_f.reshape(n_heads, 1), seq=seq)
    o_attn = _fox_attention(qkv, c, batch=batch, seq=seq, n_heads=n_heads,
                            tile=ATTN_KV_TILE, heads_per_step=ATTN_HEADS_PER_STEP)
    return _out_proj(o_attn, g_attn, y_lru, h2d, p2d, vec(attn_out_gain),
                     w_out.astype(BF16), vec(post_gain), w_ple.astype(BF16),
                     vec(ple_gain), w_ple_gate.astype(BF16), vec(b_ple_gate), tm=OUT_ROWS,
                     sub_rows=OUT_SUB_ROWS)


def kernel(x, p, w_in, b_f, pre_gain, post_gain, conv_w, conv_b, w_rgate, b_rgate,
           w_igate, b_igate, lru_lambda, attn_out_gain, lru_out_gain, w_out, w_ple,
           ple_gain, w_ple_gate, b_ple_gate):
    batch, seq, d = x.shape
    h = x.reshape(batch * seq, d)
    for i in range(w_in.shape[0]):
        h = _layer(h, p[i].reshape(batch * seq, -1), w_in[i], b_f[i], pre_gain[i],
                   post_gain[i], conv_w[i], conv_b[i], w_rgate[i], b_rgate[i],
                   w_igate[i], b_igate[i], lru_lambda[i], attn_out_gain[i],
                   lru_out_gain[i], w_out[i], w_ple[i], ple_gain[i], w_ple_gate[i],
                   b_ple_gate[i], batch=batch, seq=seq)
    return h.reshape(batch, seq, d)
```

```python
import functools
import math

import jax
import jax.numpy as jnp
from jax import lax
from jax.experimental import pallas as pl
from jax.experimental.pallas import tpu as pltpu

HEAD_DIM = 128
N_LRU_BLOCKS = 8
LRU_BLOCK = 128
CONV_WIDTH = 4
LRU_C = 8.0
RMS_EPS = 1e-6
N_FORGET_PAD = 128
SUBLANES = 8
VMEM_LIMIT_BYTES = 56 * 1024 * 1024
NEG_BIG = -1e30
TINY = 1e-30
LOG2_E = math.log2(math.e)

F32 = jnp.float32
BF16 = jnp.bfloat16

PROJ_ROWS = 1024
PROJ_COLS = 256
LRU_ROWS = 256
ATTN_KV_TILE = 512
ATTN_HEADS_PER_STEP = 2
OUT_ROWS = 1024
OUT_SUB_ROWS = 512


def _sigmoid(x):
    return 0.5 * jnp.tanh(0.5 * x) + 0.5


def _rms_scale(x):
    return lax.rsqrt(jnp.mean(x * x, axis=-1, keepdims=True) + RMS_EPS)


def _segment_permutation(ts):
    p = jnp.arange(ts)
    t = (p % SUBLANES) * (ts // SUBLANES) + p // SUBLANES
    return (t[:, None] == jnp.arange(ts)[None, :]).astype(BF16)


def _in_proj_lru_kernel(x_ref, gain_ref, wqkv_ref, wg_ref, wf_ref, perm_ref, unperm_ref,
                        cw_ref, cb_ref, wri_ref, br_ref, bi_ref, lam_ref, lgain_ref,
                        qkv_ref, gattn_ref, ylru_ref, flt_ref,
                        xlru_scr, glru_scr, xpad_scr, a_scr, u_scr, h_scr, acum_scr,
                        tail_scr, carry_scr,
                        *, d_attn, chunk, q_scale, ts, tiles_per_seq):
    tm, d = x_ref.shape
    n_sub = tm // ts
    seg = ts // SUBLANES
    head = (CONV_WIDTH - 1) * SUBLANES

    @pl.when(pl.program_id(0) % tiles_per_seq == 0)
    def _():
        tail_scr[...] = jnp.zeros(tail_scr.shape, F32)
        carry_scr[...] = jnp.zeros(carry_scr.shape, F32)

    x = x_ref[...]
    xn = (x * _rms_scale(x) * gain_ref[...]).astype(BF16)
    perm = perm_ref[...]
    xn_seg = jnp.concatenate(
        [jnp.dot(perm, xn[s * ts:(s + 1) * ts, :], preferred_element_type=F32).astype(BF16)
         for s in range(n_sub)], axis=0)

    for dst, lo in ((xlru_scr, d_attn), (glru_scr, 2 * d_attn)):
        for c0 in range(0, d_attn, chunk):
            dst[:, c0:c0 + chunk] = jnp.dot(xn_seg, wg_ref[:, lo + c0:lo + c0 + chunk],
                                            preferred_element_type=F32)

    neg_lam = -lam_ref[...]
    k_unit = -LRU_C * (jnp.maximum(neg_lam, 0.0) + jnp.log1p(jnp.exp(-jnp.abs(neg_lam))))
    c1 = (0.5 * LOG2_E) * k_unit
    hb_r = 0.5 * br_ref[...]
    hb_i = 0.5 * bi_ref[...]
    sub = lax.broadcasted_iota(jnp.int32, (SUBLANES, d), 0)

    def lru_gates(s):
        base = s * ts
        x_s = xlru_scr[base:base + ts, :]
        pad0 = s * (ts + head)
        xpad_scr[pad0 + head:pad0 + head + ts, :] = x_s
        for m in range(CONV_WIDTH - 1):
            grp = slice(m * SUBLANES, (m + 1) * SUBLANES)
            cur = x_s[ts - head + m * SUBLANES:ts - head + (m + 1) * SUBLANES, :]
            xpad_scr[pad0 + m * SUBLANES:pad0 + (m + 1) * SUBLANES, :] = jnp.where(
                sub == 0, pltpu.roll(tail_scr[grp, :], 1, axis=0),
                pltpu.roll(cur, 1, axis=0))
        tail_scr[...] = x_s[ts - head:ts, :]
        xc = cb_ref[...] + sum(
            cw_ref[j:j + 1, :]
            * xpad_scr[pad0 + j * SUBLANES:pad0 + j * SUBLANES + ts, :]
            for j in range(CONV_WIDTH))
        xcb = xc.astype(BF16)
        for n in range(N_LRU_BLOCKS):
            cols = slice(n * LRU_BLOCK, (n + 1) * LRU_BLOCK)
            gates = jnp.dot(xcb[:, cols], wri_ref[n], preferred_element_type=F32)
            tr = jnp.tanh(gates[:, :LRU_BLOCK] + hb_r[:, cols])
            ti = jnp.tanh(gates[:, LRU_BLOCK:] + hb_i[:, cols])
            a = jnp.exp2(tr * c1[:, cols] + c1[:, cols])
            y2 = 1.0 - a * a
            a_scr[base:base + ts, cols] = a
            u_scr[base:base + ts, cols] = (y2 * lax.rsqrt(jnp.maximum(y2, TINY))) * (
                (ti + 1.0) * (0.5 * xc[:, cols]))

    def lru_scan(s):
        base = s * ts
        h = jnp.zeros((SUBLANES, d), F32)
        acc = jnp.ones((SUBLANES, d), F32)
        for k in range(seg):
            rows = slice(base + k * SUBLANES, base + (k + 1) * SUBLANES)
            a = a_scr[rows, :]
            h = a * h + u_scr[rows, :]
            acc = a * acc
            h_scr[rows, :] = h
            acum_scr[rows, :] = acc
        c = carry_scr[0:1, :]
        seg_in = [c]
        for r in range(SUBLANES):
            c = acc[r:r + 1, :] * c + h[r:r + 1, :]
            seg_in.append(c)
        carry_scr[...] = jnp.broadcast_to(seg_in[SUBLANES], carry_scr.shape)
        return jnp.concatenate(seg_in[:SUBLANES], axis=0)

    def lru_finish(s, c_in):
        base = s * ts
        gain = lgain_ref[...]
        for k in range(seg):
            rows = slice(base + k * SUBLANES, base + (k + 1) * SUBLANES)
            h = h_scr[rows, :] + acum_scr[rows, :] * c_in
            hg = 0.5 * glru_scr[rows, :]
            a_scr[rows, :] = (h * _rms_scale(h)) * gain * (hg * (jnp.tanh(hg) + 1.0))
        y = jnp.dot(unperm_ref[...], a_scr[base:base + ts, :].astype(BF16),
                    preferred_element_type=F32)
        ylru_ref[base:base + ts, :] = y.astype(ylru_ref.dtype)

    def project(dst, w_ref, w_lo, c0, scale):
        z = jnp.dot(xn, w_ref[:, w_lo + c0:w_lo + c0 + chunk], preferred_element_type=F32)
        if scale is not None:
            z = z * scale
        dst[:, c0:c0 + chunk] = z.astype(dst.dtype)

    chunks = [(qkv_ref, wqkv_ref, 0, c0, q_scale if c0 < d_attn else None)
              for c0 in range(0, 3 * d_attn, chunk)]
    chunks += [(gattn_ref, wg_ref, 0, c0, None) for c0 in range(0, d_attn, chunk)]
    lru_steps = []
    carried = {}
    for s in range(n_sub):
        lru_steps.append(lambda s=s: lru_gates(s))
    for s in range(n_sub):
        lru_steps.append(lambda s=s: carried.__setitem__(s, lru_scan(s)))
        lru_steps.append(lambda s=s: lru_finish(s, carried[s]))
    for n, args in enumerate(chunks):
        project(*args)
        if n < len(lru_steps):
            lru_steps[n]()
    for step in lru_steps[len(chunks):]:
        step()
    fl = jnp.dot(xn, wf_ref[...], preferred_element_type=F32)
    flt_ref[...] = fl.T[:flt_ref.shape[0], :]


def _in_proj_lru(x2d, pre_gain, w_qkv, w_g, w_f, conv_w, conv_b, w_ri, b_r, b_i, lam,
                 lru_gain, *, n_heads, seq, tm, chunk, ts):
    n, d = x2d.shape
    d_attn = n_heads * HEAD_DIM
    n_sub = tm // ts
    kern = functools.partial(_in_proj_lru_kernel, d_attn=d_attn, chunk=chunk,
                             q_scale=HEAD_DIM ** -0.5 * LOG2_E, ts=ts,
                             tiles_per_seq=seq // tm)
    perm = _segment_permutation(ts)
    const = lambda i: (0, 0)
    row = lambda i: (i, 0)
    single = pl.Buffered(1)
    vec = pl.BlockSpec((1, d), const)
    mat = pl.BlockSpec((ts, ts), const)
    head = (CONV_WIDTH - 1) * SUBLANES
    return pl.pallas_call(
        kern,
        grid=(n // tm,),
        in_specs=[
            pl.BlockSpec((tm, d), row),
            vec,
            pl.BlockSpec((d, 3 * d_attn), const, pipeline_mode=single),
            pl.BlockSpec(w_g.shape, const, pipeline_mode=single),
            pl.BlockSpec(w_f.shape, const, pipeline_mode=single),
            mat, mat,
            pl.BlockSpec((CONV_WIDTH, d), const),
            vec,
            pl.BlockSpec(w_ri.shape, lambda i: (0, 0, 0)),
            vec, vec, vec, vec,
        ],
        out_specs=[
            pl.BlockSpec((tm, 3 * d_attn), row),
            pl.BlockSpec((tm, d_attn), row),
            pl.BlockSpec((tm, d), row),
            pl.BlockSpec((n_heads, tm), lambda i: (0, i)),
        ],
        out_shape=[
            jax.ShapeDtypeStruct((n, 3 * d_attn), BF16),
            jax.ShapeDtypeStruct((n, d_attn), BF16),
            jax.ShapeDtypeStruct((n, d), BF16),
            jax.ShapeDtypeStruct((n_heads, n), F32),
        ],
        scratch_shapes=[pltpu.VMEM((tm, d), F32),
                        pltpu.VMEM((tm, d), F32),
                        pltpu.VMEM((n_sub * (ts + head), d), F32),
                        pltpu.VMEM((tm, d), F32),
                        pltpu.VMEM((tm, d), F32),
                        pltpu.VMEM((tm, d), F32),
                        pltpu.VMEM((tm, d), F32),
                        pltpu.VMEM((head, d), F32),
                        pltpu.VMEM((SUBLANES, d), F32)],
        compiler_params=pltpu.CompilerParams(
            dimension_semantics=("arbitrary",),
            vmem_limit_bytes=VMEM_LIMIT_BYTES),
        name="in_proj_lru",
    )(x2d, pre_gain, w_qkv, w_g, w_f, perm, perm.T, conv_w, conv_b, w_ri, b_r, b_i, lam,
      lru_gain)


def _forget_cumsum_kernel(flt_ref, bf_ref, c_ref):
    z = flt_ref[...] + bf_ref[...]
    ls = jnp.minimum(z, 0.0) - jnp.log1p(jnp.exp(-jnp.abs(z)))
    seq = ls.shape[1]
    lane = lax.broadcasted_iota(jnp.int32, ls.shape, 1)
    shift = 1
    while shift < seq:
        ls = ls + jnp.where(lane >= shift, pltpu.roll(ls, shift, axis=1), 0.0)
        shift *= 2
    c_ref[...] = ls * LOG2_E


def _forget_cumsum(fl_t, b_f, *, seq):
    n_heads, n = fl_t.shape
    return pl.pallas_call(
        _forget_cumsum_kernel,
        grid=(n // seq,),
        in_specs=[pl.BlockSpec((n_heads, seq), lambda b: (0, b)),
                  pl.BlockSpec((n_heads, 1), lambda b: (0, 0))],
        out_specs=pl.BlockSpec((n_heads, seq), lambda b: (0, b)),
        out_shape=jax.ShapeDtypeStruct((n_heads, n), F32),
        compiler_params=pltpu.CompilerParams(dimension_semantics=("arbitrary",)),
        name="forget_cumsum",
    )(fl_t, b_f)


def _fox_attention_kernel(q_ref, k_ref, v_ref, c_ref, o_ref,
                          s0_scr, s1_scr, s2_scr, s3_scr, r0_scr, r1_scr, r2_scr, r3_scr,
                          vext_scr, m_scr, acc_scr, *, tile, heads_per_step):
    hg = pl.program_id(1)
    tq = 2 * tile
    n_super = q_ref.shape[0] // tq
    half_a = slice(0, tile)
    half_b = slice(tile, tq)
    both = slice(0, tq)

    for hh in range(heads_per_step):
        vext_scr[hh, :, :HEAD_DIM] = v_ref[:, hh * HEAD_DIM:(hh + 1) * HEAD_DIM]
        vext_scr[hh, :, HEAD_DIM:] = jnp.ones((vext_scr.shape[1], HEAD_DIM), BF16)

    def scores(hh, i, j, s_scr, r_scr, rows):
        start = j * tile
        q0 = i * tq + rows.start
        n_rows = rows.stop - rows.start
        cols = slice(hh * HEAD_DIM, (hh + 1) * HEAD_DIM)
        s = lax.dot_general(q_ref[pl.ds(q0, n_rows), cols],
                            k_ref[pl.ds(start, tile), cols],
                            (((1,), (1,)), ((), ())), preferred_element_type=F32)
        s = s - c_ref[pl.ds(hg * heads_per_step + hh, 1), pl.ds(start, tile)]
        s_scr[rows, :] = s
        r_scr[rows, :] = jnp.broadcast_to(jnp.max(s, axis=1, keepdims=True),
                                          (n_rows, HEAD_DIM))

    def consume(hh, j, s_scr, r_scr, rows, diag):
        start = j * tile
        if diag:
            half = tile // 2
            parts = [(slice(rows.start, rows.start + half), half),
                     (slice(rows.start + half, rows.stop), tile)]
            row = lax.broadcasted_iota(jnp.int32, (half, HEAD_DIM), 0)
            lane = lax.broadcasted_iota(jnp.int32, (half, HEAD_DIM), 1)
        else:
            parts = [(rows, tile)]
        for prows, n_keys in parts:
            m_prev = m_scr[prows, :]
            s_chunks = [s_scr[prows, c * HEAD_DIM:(c + 1) * HEAD_DIM]
                        for c in range(n_keys // HEAD_DIM)]
            if diag:
                first = (n_keys - half) // HEAD_DIM
                for c in range(first, len(s_chunks)):
                    keep = row >= lane + (c - first) * HEAD_DIM
                    s_chunks[c] = jnp.where(keep, s_chunks[c], NEG_BIG)
                blk_max = functools.reduce(jnp.maximum, s_chunks)
                m_new = jnp.maximum(m_prev, jnp.max(blk_max, axis=1, keepdims=True))
            else:
                m_new = jnp.maximum(m_prev, r_scr[prows, :])
            alpha = jnp.exp2(m_prev - m_new)
            pb = jnp.concatenate([jnp.exp2(sc - m_new).astype(BF16) for sc in s_chunks],
                                 axis=1)
            pv = jnp.dot(pb, vext_scr[hh, pl.ds(start, n_keys), :],
                         preferred_element_type=F32)
            acc_scr[prows, :HEAD_DIM] = (alpha * acc_scr[prows, :HEAD_DIM]
                                         + pv[:, :HEAD_DIM])
            acc_scr[prows, HEAD_DIM:] = (alpha * acc_scr[prows, HEAD_DIM:]
                                         + pv[:, HEAD_DIM:])
            m_scr[prows, :] = m_new

    def reset():
        m_scr[...] = jnp.full(m_scr.shape, NEG_BIG, F32)
        acc_scr[...] = jnp.zeros(acc_scr.shape, F32)

    assert heads_per_step % 2 == 0
    bufs = ((s0_scr, s1_scr, r0_scr, r1_scr), (s2_scr, s3_scr, r2_scr, r3_scr))

    def super_tile(i, carry):
        for hh in range(heads_per_step):
            sa, sb, ra, rb = bufs[hh % 2]
            nxt_s, _, nxt_r, _ = bufs[(hh + 1) % 2]

            def pair(t, c, hh=hh, sa=sa, sb=sb, ra=ra, rb=rb):
                j = 2 * t
                scores(hh, i, j + 1, sb, rb, both)
                consume(hh, j, sa, ra, both, diag=False)
                scores(hh, i, j + 2, sa, ra, both)
                consume(hh, j + 1, sb, rb, both, diag=False)
                return c

            reset()
            for t in range(i):
                pair(t, 0)
            scores(hh, i, 2 * i + 1, sb, rb, half_b)
            if hh + 1 < heads_per_step:
                scores(hh + 1, i, 0, nxt_s, nxt_r, both)
            elif i + 1 < n_super:
                scores(0, i + 1, 0, nxt_s, nxt_r, both)
            consume(hh, 2 * i, sa, ra, half_a, diag=True)
            consume(hh, 2 * i, sa, ra, half_b, diag=False)
            consume(hh, 2 * i + 1, sb, rb, half_b, diag=True)
            o_ref[pl.ds(i * tq, tq),
                  hh * HEAD_DIM:(hh + 1) * HEAD_DIM] = (
                acc_scr[:, :HEAD_DIM] / acc_scr[:, HEAD_DIM:]).astype(o_ref.dtype)
        return carry

    scores(0, 0, 0, s0_scr, r0_scr, both)
    for i in range(n_super):
        super_tile(i, 0)


def _fox_attention(qkv, c, *, batch, seq, n_heads, tile, heads_per_step):
    n = qkv.shape[0]
    tq = 2 * tile
    hp = heads_per_step
    w = hp * HEAD_DIM
    ng = n_heads // hp
    kern = functools.partial(_fox_attention_kernel, tile=tile, heads_per_step=hp)
    return pl.pallas_call(
        kern,
        grid=(batch, ng),
        in_specs=[
            pl.BlockSpec((seq, w), lambda b, g: (b, g)),
            pl.BlockSpec((seq, w), lambda b, g: (b, ng + g)),
            pl.BlockSpec((seq, w), lambda b, g: (b, 2 * ng + g)),
            pl.BlockSpec((n_heads, seq), lambda b, g: (0, b)),
        ],
        out_specs=pl.BlockSpec((seq, w), lambda b, g: (b, g)),
        out_shape=jax.ShapeDtypeStruct((n, n_heads * HEAD_DIM), BF16),
        scratch_shapes=[pltpu.VMEM((tq, tile), F32)] * 4
                       + [pltpu.VMEM((tq, HEAD_DIM), F32)] * 4
                       + [pltpu.VMEM((hp, seq, 2 * HEAD_DIM), BF16),
                        pltpu.VMEM((tq, HEAD_DIM), F32),
                        pltpu.VMEM((tq, 2 * HEAD_DIM), F32)],
        compiler_params=pltpu.CompilerParams(
            dimension_semantics=("arbitrary", "arbitrary"),
            vmem_limit_bytes=VMEM_LIMIT_BYTES),
        name="fox_attention",
    )(qkv, qkv, qkv, c)


def _out_proj_kernel(oa_ref, ga_ref, yl_ref, x_ref, p_ref, again_ref, wout_ref,
                     pgain_ref, wple_ref, plegain_ref, wpg_ref, bpg_ref, out_ref,
                     *, d_attn, sub_rows):
    subs = [slice(r0, r0 + sub_rows) for r0 in range(0, out_ref.shape[0], sub_rows)]

    def gated_attn(rows):
        oa = oa_ref[rows, :].astype(F32)
        hg = 0.5 * ga_ref[rows, :].astype(F32)
        ya = oa * _rms_scale(oa) * again_ref[...] * (hg * (jnp.tanh(hg) + 1.0))
        return ya.astype(BF16)

    def mixed(rows, ya):
        mix = jnp.dot(ya, wout_ref[0:d_attn, :], preferred_element_type=F32)
        return mix + jnp.dot(yl_ref[rows, :], wout_ref[d_attn:, :],
                             preferred_element_type=F32)

    def embed(rows):
        e = jnp.dot(p_ref[rows, :].astype(BF16), wple_ref[...], preferred_element_type=F32)
        return e * _rms_scale(e) * plegain_ref[...]

    def residual(rows, mix):
        return x_ref[rows, :] + mix * _rms_scale(mix) * pgain_ref[...]

    def gate_logits(h1):
        return jnp.dot(h1.astype(BF16), wpg_ref[...], preferred_element_type=F32)

    ya = [gated_attn(r) for r in subs]
    mix = [mixed(r, y) for r, y in zip(subs, ya)]
    e = [embed(r) for r in subs]
    h1 = [residual(r, m) for r, m in zip(subs, mix)]
    z = [gate_logits(h) for h in h1]
    for r, h, zz, ee in zip(subs, h1, z, e):
        out_ref[r, :] = h + _sigmoid(zz + bpg_ref[...]) * ee


def _out_proj(o_attn, g_attn, y_lru, x2d, p2d, attn_gain, w_out, post_gain, w_ple,
              ple_gain, w_pg, b_pg, *, tm, sub_rows):
    n, d = x2d.shape
    d_attn = o_attn.shape[1]
    d_lru = y_lru.shape[1]
    d_ple = p2d.shape[1]
    kern = functools.partial(_out_proj_kernel, d_attn=d_attn, sub_rows=sub_rows)
    row = lambda i: (i, 0)
    const = lambda i: (0, 0)
    single = pl.Buffered(1)
    return pl.pallas_call(
        kern,
        grid=(n // tm,),
        in_specs=[
            pl.BlockSpec((tm, d_attn), row),
            pl.BlockSpec((tm, d_attn), row),
            pl.BlockSpec((tm, d_lru), row),
            pl.BlockSpec((tm, d), row),
            pl.BlockSpec((tm, d_ple), row),
            pl.BlockSpec((1, d_attn), const),
            pl.BlockSpec(w_out.shape, const, pipeline_mode=single),
            pl.BlockSpec((1, d), const),
            pl.BlockSpec(w_ple.shape, const, pipeline_mode=single),
            pl.BlockSpec((1, d), const),
            pl.BlockSpec(w_pg.shape, const, pipeline_mode=single),
            pl.BlockSpec((1, d), const),
        ],
        out_specs=pl.BlockSpec((tm, d), row),
        out_shape=jax.ShapeDtypeStruct((n, d), F32),
        compiler_params=pltpu.CompilerParams(
            dimension_semantics=("arbitrary",),
            vmem_limit_bytes=VMEM_LIMIT_BYTES),
        name="out_proj",
    )(o_attn, g_attn, y_lru, x2d, p2d, attn_gain, w_out, post_gain, w_ple, ple_gain,
      w_pg, b_pg)


def _layer(h2d, p2d, w_in, b_f, pre_gain, post_gain, conv_w, conv_b, w_rgate, b_rgate,
           w_igate, b_igate, lru_lambda, attn_out_gain, lru_out_gain, w_out, w_ple,
           ple_gain, w_ple_gate, b_ple_gate, *, batch, seq):
    d = h2d.shape[1]
    n_heads = b_f.shape[0]
    d_attn = n_heads * HEAD_DIM
    assert w_in.shape[1] == 4 * d_attn + n_heads + 2 * d and d == d_attn
    fl_lo = 3 * d_attn
    w_bf = w_in.astype(BF16)
    w_g = w_bf[:, fl_lo + n_heads:]
    w_f = jnp.pad(w_bf[:, fl_lo:fl_lo + n_heads], ((0, 0), (0, N_FORGET_PAD - n_heads)))
    vec = lambda v: v.reshape(1, -1)

    w_ri = (0.5 * jnp.concatenate([w_rgate, w_igate], axis=-1)).astype(BF16)
    qkv, g_attn, y_lru, fl_t = _in_proj_lru(
        h2d, vec(pre_gain), w_bf, w_g, w_f, conv_w, vec(conv_b), w_ri, vec(b_rgate),
        vec(b_igate), vec(lru_lambda), vec(lru_out_gain), n_heads=n_heads, seq=seq,
        tm=PROJ_ROWS, chunk=PROJ_COLS, ts=LRU_ROWS)
    c = _forget_cumsum(fl_t, b_f.reshape(n_heads, 1), seq=seq)
    o_attn = _fox_attention(qkv, c, batch=batch, seq=seq, n_heads=n_heads,
                            tile=ATTN_KV_TILE, heads_per_step=ATTN_HEADS_PER_STEP)
    return _out_proj(o_attn, g_attn, y_lru, h2d, p2d, vec(attn_out_gain),
                     w_out.astype(BF16), vec(post_gain), w_ple.astype(BF16),
                     vec(ple_gain), w_ple_gate.astype(BF16), vec(b_ple_gate), tm=OUT_ROWS,
                     sub_rows=OUT_SUB_ROWS)


def kernel(x, p, w_in, b_f, pre_gain, post_gain, conv_w, conv_b, w_rgate, b_rgate,
           w_igate, b_igate, lru_lambda, attn_out_gain, lru_out_gain, w_out, w_ple,
           ple_gain, w_ple_gate, b_ple_gate):
    batch, seq, d = x.shape
    h = x.reshape(batch * seq, d)
    for i in range(w_in.shape[0]):
        h = _layer(h, p[i].reshape(batch * seq, -1), w_in[i], b_f[i], pre_gain[i],
                   post_gain[i], conv_w[i], conv_b[i], w_rgate[i], b_rgate[i],
                   w_igate[i], b_igate[i], lru_lambda[i], attn_out_gain[i],
                   lru_out_gain[i], w_out[i], w_ple[i], ple_gain[i], w_ple_gate[i],
                   b_ple_gate[i], batch=batch, seq=seq)
    return h.reshape(batch, seq, d)
```

```python
import functools
import math

import jax
import jax.numpy as jnp
from jax import lax
from jax.experimental import pallas as pl
from jax.experimental.pallas import tpu as pltpu

HEAD_DIM = 128
N_LRU_BLOCKS = 8
LRU_BLOCK = 128
CONV_WIDTH = 4
LRU_C = 8.0
RMS_EPS = 1e-6
N_FORGET_PAD = 128
SUBLANES = 8
VMEM_LIMIT_BYTES = 56 * 1024 * 1024
NEG_BIG = -1e30
TINY = 1e-30
LOG2_E = math.log2(math.e)

F32 = jnp.float32
BF16 = jnp.bfloat16

PROJ_ROWS = 1024
PROJ_COLS = 1024
LRU_ROWS = 256
ATTN_KV_TILE = 512
ATTN_HEADS_PER_STEP = 2
OUT_ROWS = 1024
OUT_SUB_ROWS = 512


def _sigmoid(x):
    return 0.5 * jnp.tanh(0.5 * x) + 0.5


def _rms_scale(x):
    return lax.rsqrt(jnp.mean(x * x, axis=-1, keepdims=True) + RMS_EPS)


def _segment_permutation(ts):
    p = jnp.arange(ts)
    t = (p % SUBLANES) * (ts // SUBLANES) + p // SUBLANES
    return (t[:, None] == jnp.arange(ts)[None, :]).astype(BF16)


def _in_proj_lru_kernel(x_ref, gain_ref, wqkv_ref, wg_ref, wf_ref, perm_ref, unperm_ref,
                        cw_ref, cb_ref, wri_ref, br_ref, bi_ref, lam_ref, lgain_ref,
                        qkv_ref, gattn_ref, ylru_ref, flt_ref,
                        xlru_scr, glru_scr, xpad_scr, a_scr, u_scr, h_scr, acum_scr,
                        tail_scr, carry_scr,
                        *, d_attn, chunk, q_scale, ts, tiles_per_seq):
    tm, d = x_ref.shape
    n_sub = tm // ts
    seg = ts // SUBLANES
    head = (CONV_WIDTH - 1) * SUBLANES

    @pl.when(pl.program_id(0) % tiles_per_seq == 0)
    def _():
        tail_scr[...] = jnp.zeros(tail_scr.shape, F32)
        carry_scr[...] = jnp.zeros(carry_scr.shape, F32)

    x = x_ref[...]
    xn = (x * _rms_scale(x) * gain_ref[...]).astype(BF16)
    perm = perm_ref[...]
    xn_seg = jnp.concatenate(
        [jnp.dot(perm, xn[s * ts:(s + 1) * ts, :], preferred_element_type=F32).astype(BF16)
         for s in range(n_sub)], axis=0)

    for dst, lo in ((xlru_scr, d_attn), (glru_scr, 2 * d_attn)):
        for c0 in range(0, d_attn, chunk):
            dst[:, c0:c0 + chunk] = jnp.dot(xn_seg, wg_ref[:, lo + c0:lo + c0 + chunk],
                                            preferred_element_type=F32)

    neg_lam = -lam_ref[...]
    k_unit = -LRU_C * (jnp.maximum(neg_lam, 0.0) + jnp.log1p(jnp.exp(-jnp.abs(neg_lam))))
    c1 = (0.5 * LOG2_E) * k_unit
    hb_r = 0.5 * br_ref[...]
    hb_i = 0.5 * bi_ref[...]
    sub = lax.broadcasted_iota(jnp.int32, (SUBLANES, d), 0)

    def lru_gates(s):
        base = s * ts
        x_s = xlru_scr[base:base + ts, :]
        pad0 = s * (ts + head)
        xpad_scr[pad0 + head:pad0 + head + ts, :] = x_s
        for m in range(CONV_WIDTH - 1):
            grp = slice(m * SUBLANES, (m + 1) * SUBLANES)
            cur = x_s[ts - head + m * SUBLANES:ts - head + (m + 1) * SUBLANES, :]
            xpad_scr[pad0 + m * SUBLANES:pad0 + (m + 1) * SUBLANES, :] = jnp.where(
                sub == 0, pltpu.roll(tail_scr[grp, :], 1, axis=0),
                pltpu.roll(cur, 1, axis=0))
        tail_scr[...] = x_s[ts - head:ts, :]
        xc = cb_ref[...] + sum(
            cw_ref[j:j + 1, :]
            * xpad_scr[pad0 + j * SUBLANES:pad0 + j * SUBLANES + ts, :]
            for j in range(CONV_WIDTH))
        xcb = xc.astype(BF16)
        for n in range(N_LRU_BLOCKS):
            cols = slice(n * LRU_BLOCK, (n + 1) * LRU_BLOCK)
            gates = jnp.dot(xcb[:, cols], wri_ref[n], preferred_element_type=F32)
            tr = jnp.tanh(gates[:, :LRU_BLOCK] + hb_r[:, cols])
            ti = jnp.tanh(gates[:, LRU_BLOCK:] + hb_i[:, cols])
            a = jnp.exp2(tr * c1[:, cols] + c1[:, cols])
            y2 = 1.0 - a * a
            a_scr[base:base + ts, cols] = a
            u_scr[base:base + ts, cols] = (y2 * lax.rsqrt(jnp.maximum(y2, TINY))) * (
                (ti + 1.0) * (0.5 * xc[:, cols]))

    def lru_scan(s):
        base = s * ts
        h = jnp.zeros((SUBLANES, d), F32)
        acc = jnp.ones((SUBLANES, d), F32)
        for k in range(seg):
            rows = slice(base + k * SUBLANES, base + (k + 1) * SUBLANES)
            a = a_scr[rows, :]
            h = a * h + u_scr[rows, :]
            acc = a * acc
            h_scr[rows, :] = h
            acum_scr[rows, :] = acc
        c = carry_scr[0:1, :]
        seg_in = [c]
        for r in range(SUBLANES):
            c = acc[r:r + 1, :] * c + h[r:r + 1, :]
            seg_in.append(c)
        carry_scr[...] = jnp.broadcast_to(seg_in[SUBLANES], carry_scr.shape)
        return jnp.concatenate(seg_in[:SUBLANES], axis=0)

    def lru_finish(s, c_in):
        base = s * ts
        gain = lgain_ref[...]
        for k in range(seg):
            rows = slice(base + k * SUBLANES, base + (k + 1) * SUBLANES)
            h = h_scr[rows, :] + acum_scr[rows, :] * c_in
            hg = 0.5 * glru_scr[rows, :]
            a_scr[rows, :] = (h * _rms_scale(h)) * gain * (hg * (jnp.tanh(hg) + 1.0))
        y = jnp.dot(unperm_ref[...], a_scr[base:base + ts, :].astype(BF16),
                    preferred_element_type=F32)
        ylru_ref[base:base + ts, :] = y.astype(ylru_ref.dtype)

    def project(dst, w_ref, w_lo, c0, scale):
        z = jnp.dot(xn, w_ref[:, w_lo + c0:w_lo + c0 + chunk], preferred_element_type=F32)
        if scale is not None:
            z = z * scale
        dst[:, c0:c0 + chunk] = z.astype(dst.dtype)

    chunks = [(qkv_ref, wqkv_ref, 0, c0, q_scale if c0 < d_attn else None)
              for c0 in range(0, 3 * d_attn, chunk)]
    chunks += [(gattn_ref, wg_ref, 0, c0, None) for c0 in range(0, d_attn, chunk)]
    lru_steps = []
    carried = {}
    for s in range(n_sub):
        lru_steps.append(lambda s=s: lru_gates(s))
    for s in range(n_sub):
        lru_steps.append(lambda s=s: carried.__setitem__(s, lru_scan(s)))
        lru_steps.append(lambda s=s: lru_finish(s, carried[s]))
    for n, args in enumerate(chunks):
        project(*args)
        if n < len(lru_steps):
            lru_steps[n]()
    for step in lru_steps[len(chunks):]:
        step()
    fl = jnp.dot(xn, wf_ref[...], preferred_element_type=F32)
    flt_ref[...] = fl.T[:flt_ref.shape[0], :]


def _in_proj_lru(x2d, pre_gain, w_qkv, w_g, w_f, conv_w, conv_b, w_ri, b_r, b_i, lam,
                 lru_gain, *, n_heads, seq, tm, chunk, ts):
    n, d = x2d.shape
    d_attn = n_heads * HEAD_DIM
    n_sub = tm // ts
    kern = functools.partial(_in_proj_lru_kernel, d_attn=d_attn, chunk=chunk,
                             q_scale=HEAD_DIM ** -0.5 * LOG2_E, ts=ts,
                             tiles_per_seq=seq // tm)
    perm = _segment_permutation(ts)
    const = lambda i: (0, 0)
    row = lambda i: (i, 0)
    single = pl.Buffered(1)
    vec = pl.BlockSpec((1, d), const)
    mat = pl.BlockSpec((ts, ts), const)
    head = (CONV_WIDTH - 1) * SUBLANES
    return pl.pallas_call(
        kern,
        grid=(n // tm,),
        in_specs=[
            pl.BlockSpec((tm, d), row),
            vec,
            pl.BlockSpec((d, 3 * d_attn), const, pipeline_mode=single),
            pl.BlockSpec(w_g.shape, const, pipeline_mode=single),
            pl.BlockSpec(w_f.shape, const, pipeline_mode=single),
            mat, mat,
            pl.BlockSpec((CONV_WIDTH, d), const),
            vec,
            pl.BlockSpec(w_ri.shape, lambda i: (0, 0, 0)),
            vec, vec, vec, vec,
        ],
        out_specs=[
            pl.BlockSpec((tm, 3 * d_attn), row),
            pl.BlockSpec((tm, d_attn), row),
            pl.BlockSpec((tm, d), row),
            pl.BlockSpec((n_heads, tm), lambda i: (0, i)),
        ],
        out_shape=[
            jax.ShapeDtypeStruct((n, 3 * d_attn), BF16),
            jax.ShapeDtypeStruct((n, d_attn), BF16),
            jax.ShapeDtypeStruct((n, d), BF16),
            jax.ShapeDtypeStruct((n_heads, n), F32),
        ],
        scratch_shapes=[pltpu.VMEM((tm, d), F32),
                        pltpu.VMEM((tm, d), F32),
                        pltpu.VMEM((n_sub * (ts + head), d), F32),
                        pltpu.VMEM((tm, d), F32),
                        pltpu.VMEM((tm, d), F32),
                        pltpu.VMEM((tm, d), F32),
                        pltpu.VMEM((tm, d), F32),
                        pltpu.VMEM((head, d), F32),
                        pltpu.VMEM((SUBLANES, d), F32)],
        compiler_params=pltpu.CompilerParams(
            dimension_semantics=("arbitrary",),
            vmem_limit_bytes=VMEM_LIMIT_BYTES),
        name="in_proj_lru",
    )(x2d, pre_gain, w_qkv, w_g, w_f, perm, perm.T, conv_w, conv_b, w_ri, b_r, b_i, lam,
      lru_gain)


def _forget_cumsum_kernel(flt_ref, bf_ref, c_ref):
    z = flt_ref[...] + bf_ref[...]
    ls = jnp.minimum(z, 0.0) - jnp.log1p(jnp.exp(-jnp.abs(z)))
    seq = ls.shape[1]
    lane = lax.broadcasted_iota(jnp.int32, ls.shape, 1)
    shift = 1
    while shift < seq:
        ls = ls + jnp.where(lane >= shift, pltpu.roll(ls, shift, axis=1), 0.0)
        shift *= 2
    c_ref[...] = ls * LOG2_E


def _forget_cumsum(fl_t, b_f, *, seq):
    n_heads, n = fl_t.shape
    return pl.pallas_call(
        _forget_cumsum_kernel,
        grid=(n // seq,),
        in_specs=[pl.BlockSpec((n_heads, seq), lambda b: (0, b)),
                  pl.BlockSpec((n_heads, 1), lambda b: (0, 0))],
        out_specs=pl.BlockSpec((n_heads, seq), lambda b: (0, b)),
        out_shape=jax.ShapeDtypeStruct((n_heads, n), F32),
        compiler_params=pltpu.CompilerParams(dimension_semantics=("arbitrary",)),
        name="forget_cumsum",
    )(fl_t, b_f)


def _fox_attention_kernel(q_ref, k_ref, v_ref, c_ref, o_ref,
                          s0_scr, s1_scr, s2_scr, s3_scr, r0_scr, r1_scr, r2_scr, r3_scr,
                          vext_scr, m_scr, acc_scr, *, tile, heads_per_step):
    hg = pl.program_id(1)
    tq = 2 * tile
    n_super = q_ref.shape[0] // tq
    half_a = slice(0, tile)
    half_b = slice(tile, tq)
    both = slice(0, tq)

    for hh in range(heads_per_step):
        vext_scr[hh, :, :HEAD_DIM] = v_ref[:, hh * HEAD_DIM:(hh + 1) * HEAD_DIM]
        vext_scr[hh, :, HEAD_DIM:] = jnp.ones((vext_scr.shape[1], HEAD_DIM), BF16)

    def scores(hh, i, j, s_scr, r_scr, rows):
        start = j * tile
        q0 = i * tq + rows.start
        n_rows = rows.stop - rows.start
        cols = slice(hh * HEAD_DIM, (hh + 1) * HEAD_DIM)
        s = lax.dot_general(q_ref[pl.ds(q0, n_rows), cols],
                            k_ref[pl.ds(start, tile), cols],
                            (((1,), (1,)), ((), ())), preferred_element_type=F32)
        s = s - c_ref[pl.ds(hg * heads_per_step + hh, 1), pl.ds(start, tile)]
        s_scr[rows, :] = s
        r_scr[rows, :] = jnp.broadcast_to(jnp.max(s, axis=1, keepdims=True),
                                          (n_rows, HEAD_DIM))

    def consume(hh, j, s_scr, r_scr, rows, diag):
        start = j * tile
        if diag:
            half = tile // 2
            parts = [(slice(rows.start, rows.start + half), half),
                     (slice(rows.start + half, rows.stop), tile)]
            row = lax.broadcasted_iota(jnp.int32, (half, HEAD_DIM), 0)
            lane = lax.broadcasted_iota(jnp.int32, (half, HEAD_DIM), 1)
        else:
            parts = [(rows, tile)]
        for prows, n_keys in parts:
            m_prev = m_scr[prows, :]
            s_chunks = [s_scr[prows, c * HEAD_DIM:(c + 1) * HEAD_DIM]
                        for c in range(n_keys // HEAD_DIM)]
            if diag:
                first = (n_keys - half) // HEAD_DIM
                for c in range(first, len(s_chunks)):
                    keep = row >= lane + (c - first) * HEAD_DIM
                    s_chunks[c] = jnp.where(keep, s_chunks[c], NEG_BIG)
                blk_max = functools.reduce(jnp.maximum, s_chunks)
                m_new = jnp.maximum(m_prev, jnp.max(blk_max, axis=1, keepdims=True))
            else:
                m_new = jnp.maximum(m_prev, r_scr[prows, :])
            alpha = jnp.exp2(m_prev - m_new)
            pb = jnp.concatenate([jnp.exp2(sc - m_new).astype(BF16) for sc in s_chunks],
                                 axis=1)
            pv = jnp.dot(pb, vext_scr[hh, pl.ds(start, n_keys), :],
                         preferred_element_type=F32)
            acc_scr[prows, :HEAD_DIM] = (alpha * acc_scr[prows, :HEAD_DIM]
                                         + pv[:, :HEAD_DIM])
            acc_scr[prows, HEAD_DIM:] = (alpha * acc_scr[prows, HEAD_DIM:]
                                         + pv[:, HEAD_DIM:])
            m_scr[prows, :] = m_new

    def reset():
        m_scr[...] = jnp.full(m_scr.shape, NEG_BIG, F32)
        acc_scr[...] = jnp.zeros(acc_scr.shape, F32)

    assert heads_per_step % 2 == 0
    bufs = ((s0_scr, s1_scr, r0_scr, r1_scr), (s2_scr, s3_scr, r2_scr, r3_scr))

    def super_tile(i, carry):
        for hh in range(heads_per_step):
            sa, sb, ra, rb = bufs[hh % 2]
            nxt_s, _, nxt_r, _ = bufs[(hh + 1) % 2]

            def pair(t, c, hh=hh, sa=sa, sb=sb, ra=ra, rb=rb):
                j = 2 * t
                scores(hh, i, j + 1, sb, rb, both)
                consume(hh, j, sa, ra, both, diag=False)
                scores(hh, i, j + 2, sa, ra, both)
                consume(hh, j + 1, sb, rb, both, diag=False)
                return c

            reset()
            for t in range(i):
                pair(t, 0)
            scores(hh, i, 2 * i + 1, sb, rb, half_b)
            if hh + 1 < heads_per_step:
                scores(hh + 1, i, 0, nxt_s, nxt_r, both)
            elif i + 1 < n_super:
                scores(0, i + 1, 0, nxt_s, nxt_r, both)
            consume(hh, 2 * i, sa, ra, half_a, diag=True)
            consume(hh, 2 * i, sa, ra, half_b, diag=False)
            consume(hh, 2 * i + 1, sb, rb, half_b, diag=True)
            o_ref[pl.ds(i * tq, tq),
                  hh * HEAD_DIM:(hh + 1) * HEAD_DIM] = (
                acc_scr[:, :HEAD_DIM] / acc_scr[:, HEAD_DIM:]).astype(o_ref.dtype)
        return carry

    scores(0, 0, 0, s0_scr, r0_scr, both)
    for i in range(n_super):
        super_tile(i, 0)


def _fox_attention(qkv, c, *, batch, seq, n_heads, tile, heads_per_step):
    n = qkv.shape[0]
    tq = 2 * tile
    hp = heads_per_step
    w = hp * HEAD_DIM
    ng = n_heads // hp
    kern = functools.partial(_fox_attention_kernel, tile=tile, heads_per_step=hp)
    return pl.pallas_call(
        kern,
        grid=(batch, ng),
        in_specs=[
            pl.BlockSpec((seq, w), lambda b, g: (b, g)),
            pl.BlockSpec((seq, w), lambda b, g: (b, ng + g)),
            pl.BlockSpec((seq, w), lambda b, g: (b, 2 * ng + g)),
            pl.BlockSpec((n_heads, seq), lambda b, g: (0, b)),
        ],
        out_specs=pl.BlockSpec((seq, w), lambda b, g: (b, g)),
        out_shape=jax.ShapeDtypeStruct((n, n_heads * HEAD_DIM), BF16),
        scratch_shapes=[pltpu.VMEM((tq, tile), F32)] * 4
                       + [pltpu.VMEM((tq, HEAD_DIM), F32)] * 4
                       + [pltpu.VMEM((hp, seq, 2 * HEAD_DIM), BF16),
                        pltpu.VMEM((tq, HEAD_DIM), F32),
                        pltpu.VMEM((tq, 2 * HEAD_DIM), F32)],
        compiler_params=pltpu.CompilerParams(
            dimension_semantics=("arbitrary", "arbitrary"),
            vmem_limit_bytes=VMEM_LIMIT_BYTES),
        name="fox_attention",
    )(qkv, qkv, qkv, c)


def _out_proj_kernel(oa_ref, ga_ref, yl_ref, x_ref, p_ref, again_ref, wout_ref,
                     pgain_ref, wple_ref, plegain_ref, wpg_ref, bpg_ref, out_ref,
                     *, d_attn, sub_rows):
    subs = [slice(r0, r0 + sub_rows) for r0 in range(0, out_ref.shape[0], sub_rows)]

    def gated_attn(rows):
        oa = oa_ref[rows, :].astype(F32)
        hg = 0.5 * ga_ref[rows, :].astype(F32)
        ya = oa * _rms_scale(oa) * again_ref[...] * (hg * (jnp.tanh(hg) + 1.0))
        return ya.astype(BF16)

    def mixed(rows, ya):
        mix = jnp.dot(ya, wout_ref[0:d_attn, :], preferred_element_type=F32)
        return mix + jnp.dot(yl_ref[rows, :], wout_ref[d_attn:, :],
                             preferred_element_type=F32)

    def embed(rows):
        e = jnp.dot(p_ref[rows, :].astype(BF16), wple_ref[...], preferred_element_type=F32)
        return e * _rms_scale(e) * plegain_ref[...]

    def residual(rows, mix):
        return x_ref[rows, :] + mix * _rms_scale(mix) * pgain_ref[...]

    def gate_logits(h1):
        return jnp.dot(h1.astype(BF16), wpg_ref[...], preferred_element_type=F32)

    ya = [gated_attn(r) for r in subs]
    mix = [mixed(r, y) for r, y in zip(subs, ya)]
    e = [embed(r) for r in subs]
    h1 = [residual(r, m) for r, m in zip(subs, mix)]
    z = [gate_logits(h) for h in h1]
    for r, h, zz, ee in zip(subs, h1, z, e):
        out_ref[r, :] = h + _sigmoid(zz + bpg_ref[...]) * ee


def _out_proj(o_attn, g_attn, y_lru, x2d, p2d, attn_gain, w_out, post_gain, w_ple,
              ple_gain, w_pg, b_pg, *, tm, sub_rows):
    n, d = x2d.shape
    d_attn = o_attn.shape[1]
    d_lru = y_lru.shape[1]
    d_ple = p2d.shape[1]
    kern = functools.partial(_out_proj_kernel, d_attn=d_attn, sub_rows=sub_rows)
    row = lambda i: (i, 0)
    const = lambda i: (0, 0)
    single = pl.Buffered(1)
    return pl.pallas_call(
        kern,
        grid=(n // tm,),
        in_specs=[
            pl.BlockSpec((tm, d_attn), row),
            pl.BlockSpec((tm, d_attn), row),
            pl.BlockSpec((tm, d_lru), row),
            pl.BlockSpec((tm, d), row),
            pl.BlockSpec((tm, d_ple), row),
            pl.BlockSpec((1, d_attn), const),
            pl.BlockSpec(w_out.shape, const, pipeline_mode=single),
            pl.BlockSpec((1, d), const),
            pl.BlockSpec(w_ple.shape, const, pipeline_mode=single),
            pl.BlockSpec((1, d), const),
            pl.BlockSpec(w_pg.shape, const, pipeline_mode=single),
            pl.BlockSpec((1, d), const),
        ],
        out_specs=pl.BlockSpec((tm, d), row),
        out_shape=jax.ShapeDtypeStruct((n, d), F32),
        compiler_params=pltpu.CompilerParams(
            dimension_semantics=("arbitrary",),
            vmem_limit_bytes=VMEM_LIMIT_BYTES),
        name="out_proj",
    )(o_attn, g_attn, y_lru, x2d, p2d, attn_gain, w_out, post_gain, w_ple, ple_gain,
      w_pg, b_pg)


def _layer(h2d, p2d, w_in, b_f, pre_gain, post_gain, conv_w, conv_b, w_rgate, b_rgate,
           w_igate, b_igate, lru_lambda, attn_out_gain, lru_out_gain, w_out, w_ple,
           ple_gain, w_ple_gate, b_ple_gate, *, batch, seq):
    d = h2d.shape[1]
    n_heads = b_f.shape[0]
    d_attn = n_heads * HEAD_DIM
    assert w_in.shape[1] == 4 * d_attn + n_heads + 2 * d and d == d_attn
    fl_lo = 3 * d_attn
    w_bf = w_in.astype(BF16)
    w_g = w_bf[:, fl_lo + n_heads:]
    w_f = jnp.pad(w_bf[:, fl_lo:fl_lo + n_heads], ((0, 0), (0, N_FORGET_PAD - n_heads)))
    vec = lambda v: v.reshape(1, -1)

    w_ri = (0.5 * jnp.concatenate([w_rgate, w_igate], axis=-1)).astype(BF16)
    qkv, g_attn, y_lru, fl_t = _in_proj_lru(
        h2d, vec(pre_gain), w_bf, w_g, w_f, conv_w, vec(conv_b), w_ri, vec(b_rgate),
        vec(b_igate), vec(lru_lambda), vec(lru_out_gain), n_heads=n_heads, seq=seq,
        tm=PROJ_ROWS, chunk=PROJ_COLS, ts=LRU_ROWS)
    c = _forget_cumsum(fl_t, b_f.reshape(n_heads, 1), seq=seq)
    o_attn = _fox_attention(qkv, c, batch=batch, seq=seq, n_heads=n_heads,
                            tile=ATTN_KV_TILE, heads_per_step=ATTN_HEADS_PER_STEP)
    return _out_proj(o_attn, g_attn, y_lru, h2d, p2d, vec(attn_out_gain),
                     w_out.astype(BF16), vec(post_gain), w_ple.astype(BF16),
                     vec(ple_gain), w_ple_gate.astype(BF16), vec(b_ple_gate), tm=OUT_ROWS,
                     sub_rows=OUT_SUB_ROWS)


def kernel(x, p, w_in, b_f, pre_gain, post_gain, conv_w, conv_b, w_rgate, b_rgate,
           w_igate, b_igate, lru_lambda, attn_out_gain, lru_out_gain, w_out, w_ple,
           ple_gain, w_ple_gate, b_ple_gate):
    batch, seq, d = x.shape
    h = x.reshape(batch * seq, d)
    for i in range(w_in.shape[0]):
        h = _layer(h, p[i].reshape(batch * seq, -1), w_in[i], b_f[i], pre_gain[i],
                   post_gain[i], conv_w[i], conv_b[i], w_rgate[i], b_rgate[i],
                   w_igate[i], b_igate[i], lru_lambda[i], attn_out_gain[i],
                   lru_out_gain[i], w_out[i], w_ple[i], ple_gain[i], w_ple_gate[i],
                   b_ple_gate[i], batch=batch, seq=seq)
    return h.reshape(batch, seq, d)
```

```python
import functools
import math

import jax
import jax.numpy as jnp
from jax import lax
from jax.experimental import pallas as pl
from jax.experimental.pallas import tpu as pltpu

HEAD_DIM = 128
N_LRU_BLOCKS = 8
LRU_BLOCK = 128
CONV_WIDTH = 4
LRU_C = 8.0
RMS_EPS = 1e-6
N_FORGET_PAD = 128
SUBLANES = 8
VMEM_LIMIT_BYTES = 56 * 1024 * 1024
NEG_BIG = -1e30
TINY = 1e-30
LOG2_E = math.log2(math.e)

F32 = jnp.float32
BF16 = jnp.bfloat16

PROJ_ROWS = 1024
PROJ_COLS = 512
LRU_ROWS = 256
ATTN_KV_TILE = 512
ATTN_HEADS_PER_STEP = 2
OUT_ROWS = 1024
OUT_SUB_ROWS = 512
SPLIT_ROWS = 128


def _sigmoid(x):
    return 0.5 * jnp.tanh(0.5 * x) + 0.5


def _rms_scale(x):
    return lax.rsqrt(jnp.mean(x * x, axis=-1, keepdims=True) + RMS_EPS)


def _split_w_in_kernel(w_ref, wqkv_ref, wg_ref, wf_ref, *, fl_lo, n_forget):
    wqkv_ref[...] = w_ref[:, :fl_lo].astype(BF16)
    wg_ref[...] = w_ref[:, fl_lo + n_forget:].astype(BF16)
    f = w_ref[:, fl_lo:fl_lo + N_FORGET_PAD]
    lane = lax.broadcasted_iota(jnp.int32, f.shape, 1)
    wf_ref[...] = jnp.where(lane < n_forget, f, 0.0).astype(BF16)


def _split_w_in(w_in_all, layer, *, fl_lo, n_forget, rows):
    _, d, n = w_in_all.shape
    n_g = n - fl_lo - n_forget
    kern = functools.partial(_split_w_in_kernel, fl_lo=fl_lo, n_forget=n_forget)
    row = lambda r: (r, 0)
    return pl.pallas_call(
        kern,
        grid=(d // rows,),
        in_specs=[pl.BlockSpec((None, rows, n), lambda r: (layer, r, 0))],
        out_specs=[pl.BlockSpec((rows, fl_lo), row),
                   pl.BlockSpec((rows, n_g), row),
                   pl.BlockSpec((rows, N_FORGET_PAD), row)],
        out_shape=[jax.ShapeDtypeStruct((d, fl_lo), BF16),
                   jax.ShapeDtypeStruct((d, n_g), BF16),
                   jax.ShapeDtypeStruct((d, N_FORGET_PAD), BF16)],
        compiler_params=pltpu.CompilerParams(dimension_semantics=("arbitrary",)),
        name="split_w_in",
    )(w_in_all)


def _segment_permutation(ts):
    p = jnp.arange(ts)
    t = (p % SUBLANES) * (ts // SUBLANES) + p // SUBLANES
    return (t[:, None] == jnp.arange(ts)[None, :]).astype(BF16)


def _in_proj_lru_kernel(x_ref, gain_ref, wqkv_ref, wg_ref, wf_ref, perm_ref, unperm_ref,
                        cw_ref, cb_ref, wri_ref, br_ref, bi_ref, lam_ref, lgain_ref,
                        qkv_ref, gattn_ref, ylru_ref, flt_ref,
                        xlru_scr, glru_scr, xpad_scr, a_scr, u_scr, h_scr, acum_scr,
                        tail_scr, carry_scr,
                        *, d_attn, chunk, q_scale, ts, tiles_per_seq):
    tm, d = x_ref.shape
    n_sub = tm // ts
    seg = ts // SUBLANES
    head = (CONV_WIDTH - 1) * SUBLANES

    @pl.when(pl.program_id(0) % tiles_per_seq == 0)
    def _():
        tail_scr[...] = jnp.zeros(tail_scr.shape, F32)
        carry_scr[...] = jnp.zeros(carry_scr.shape, F32)

    x = x_ref[...]
    xn = (x * _rms_scale(x) * gain_ref[...]).astype(BF16)
    perm = perm_ref[...]
    xn_seg = jnp.concatenate(
        [jnp.dot(perm, xn[s * ts:(s + 1) * ts, :], preferred_element_type=F32).astype(BF16)
         for s in range(n_sub)], axis=0)

    for dst, lo in ((xlru_scr, d_attn), (glru_scr, 2 * d_attn)):
        for c0 in range(0, d_attn, chunk):
            dst[:, c0:c0 + chunk] = jnp.dot(xn_seg, wg_ref[:, lo + c0:lo + c0 + chunk],
                                            preferred_element_type=F32)

    neg_lam = -lam_ref[...]
    k_unit = -LRU_C * (jnp.maximum(neg_lam, 0.0) + jnp.log1p(jnp.exp(-jnp.abs(neg_lam))))
    c1 = (0.5 * LOG2_E) * k_unit
    hb_r = 0.5 * br_ref[...]
    hb_i = 0.5 * bi_ref[...]
    sub = lax.broadcasted_iota(jnp.int32, (SUBLANES, d), 0)

    def lru_gates(s):
        base = s * ts
        x_s = xlru_scr[base:base + ts, :]
        pad0 = s * (ts + head)
        xpad_scr[pad0 + head:pad0 + head + ts, :] = x_s
        for m in range(CONV_WIDTH - 1):
            grp = slice(m * SUBLANES, (m + 1) * SUBLANES)
            cur = x_s[ts - head + m * SUBLANES:ts - head + (m + 1) * SUBLANES, :]
            xpad_scr[pad0 + m * SUBLANES:pad0 + (m + 1) * SUBLANES, :] = jnp.where(
                sub == 0, pltpu.roll(tail_scr[grp, :], 1, axis=0),
                pltpu.roll(cur, 1, axis=0))
        tail_scr[...] = x_s[ts - head:ts, :]
        xc = cb_ref[...] + sum(
            cw_ref[j:j + 1, :]
            * xpad_scr[pad0 + j * SUBLANES:pad0 + j * SUBLANES + ts, :]
            for j in range(CONV_WIDTH))
        xcb = xc.astype(BF16)
        for n in range(N_LRU_BLOCKS):
            cols = slice(n * LRU_BLOCK, (n + 1) * LRU_BLOCK)
            gates = jnp.dot(xcb[:, cols], wri_ref[n], preferred_element_type=F32)
            tr = jnp.tanh(gates[:, :LRU_BLOCK] + hb_r[:, cols])
            ti = jnp.tanh(gates[:, LRU_BLOCK:] + hb_i[:, cols])
            a = jnp.exp2(tr * c1[:, cols] + c1[:, cols])
            y2 = 1.0 - a * a
            a_scr[base:base + ts, cols] = a
            u_scr[base:base + ts, cols] = (y2 * lax.rsqrt(jnp.maximum(y2, TINY))) * (
                (ti + 1.0) * (0.5 * xc[:, cols]))

    def lru_scan(s):
        base = s * ts
        h = jnp.zeros((SUBLANES, d), F32)
        acc = jnp.ones((SUBLANES, d), F32)
        for k in range(seg):
            rows = slice(base + k * SUBLANES, base + (k + 1) * SUBLANES)
            a = a_scr[rows, :]
            h = a * h + u_scr[rows, :]
            acc = a * acc
            h_scr[rows, :] = h
            acum_scr[rows, :] = acc
        c = carry_scr[0:1, :]
        seg_in = [c]
        for r in range(SUBLANES):
            c = acc[r:r + 1, :] * c + h[r:r + 1, :]
            seg_in.append(c)
        carry_scr[...] = jnp.broadcast_to(seg_in[SUBLANES], carry_scr.shape)
        return jnp.concatenate(seg_in[:SUBLANES], axis=0)

    def lru_finish(s, c_in):
        base = s * ts
        gain = lgain_ref[...]
        for k in range(seg):
            rows = slice(base + k * SUBLANES, base + (k + 1) * SUBLANES)
            h = h_scr[rows, :] + acum_scr[rows, :] * c_in
            hg = 0.5 * glru_scr[rows, :]
            a_scr[rows, :] = (h * _rms_scale(h)) * gain * (hg * (jnp.tanh(hg) + 1.0))
        y = jnp.dot(unperm_ref[...], a_scr[base:base + ts, :].astype(BF16),
                    preferred_element_type=F32)
        ylru_ref[base:base + ts, :] = y.astype(ylru_ref.dtype)

    def project(dst, w_ref, w_lo, c0, scale):
        z = jnp.dot(xn, w_ref[:, w_lo + c0:w_lo + c0 + chunk], preferred_element_type=F32)
        if scale is not None:
            z = z * scale
        dst[:, c0:c0 + chunk] = z.astype(dst.dtype)

    chunks = [(qkv_ref, wqkv_ref, 0, c0, q_scale if c0 < d_attn else None)
              for c0 in range(0, 3 * d_attn, chunk)]
    chunks += [(gattn_ref, wg_ref, 0, c0, None) for c0 in range(0, d_attn, chunk)]
    lru_steps = []
    carried = {}
    for s in range(n_sub):
        lru_steps.append(lambda s=s: lru_gates(s))
    for s in range(n_sub):
        lru_steps.append(lambda s=s: carried.__setitem__(s, lru_scan(s)))
        lru_steps.append(lambda s=s: lru_finish(s, carried[s]))
    for n, args in enumerate(chunks):
        project(*args)
        if n < len(lru_steps):
            lru_steps[n]()
    for step in lru_steps[len(chunks):]:
        step()
    fl = jnp.dot(xn, wf_ref[...], preferred_element_type=F32)
    flt_ref[...] = fl.T[:flt_ref.shape[0], :]


def _in_proj_lru(x2d, pre_gain, w_qkv, w_g, w_f, conv_w, conv_b, w_ri, b_r, b_i, lam,
                 lru_gain, *, n_heads, seq, tm, chunk, ts):
    n, d = x2d.shape
    d_attn = n_heads * HEAD_DIM
    n_sub = tm // ts
    kern = functools.partial(_in_proj_lru_kernel, d_attn=d_attn, chunk=chunk,
                             q_scale=HEAD_DIM ** -0.5 * LOG2_E, ts=ts,
                             tiles_per_seq=seq // tm)
    perm = _segment_permutation(ts)
    const = lambda i: (0, 0)
    row = lambda i: (i, 0)
    single = pl.Buffered(1)
    vec = pl.BlockSpec((1, d), const)
    mat = pl.BlockSpec((ts, ts), const)
    head = (CONV_WIDTH - 1) * SUBLANES
    return pl.pallas_call(
        kern,
        grid=(n // tm,),
        in_specs=[
            pl.BlockSpec((tm, d), row),
            vec,
            pl.BlockSpec(w_qkv.shape, const, pipeline_mode=single),
            pl.BlockSpec(w_g.shape, const, pipeline_mode=single),
            pl.BlockSpec(w_f.shape, const, pipeline_mode=single),
            mat, mat,
            pl.BlockSpec((CONV_WIDTH, d), const),
            vec,
            pl.BlockSpec(w_ri.shape, lambda i: (0, 0, 0)),
            vec, vec, vec, vec,
        ],
        out_specs=[
            pl.BlockSpec((tm, 3 * d_attn), row),
            pl.BlockSpec((tm, d_attn), row),
            pl.BlockSpec((tm, d), row),
            pl.BlockSpec((n_heads, tm), lambda i: (0, i)),
        ],
        out_shape=[
            jax.ShapeDtypeStruct((n, 3 * d_attn), BF16),
            jax.ShapeDtypeStruct((n, d_attn), BF16),
            jax.ShapeDtypeStruct((n, d), BF16),
            jax.ShapeDtypeStruct((n_heads, n), F32),
        ],
        scratch_shapes=[pltpu.VMEM((tm, d), F32),
                        pltpu.VMEM((tm, d), F32),
                        pltpu.VMEM((n_sub * (ts + head), d), F32),
                        pltpu.VMEM((tm, d), F32),
                        pltpu.VMEM((tm, d), F32),
                        pltpu.VMEM((tm, d), F32),
                        pltpu.VMEM((tm, d), F32),
                        pltpu.VMEM((head, d), F32),
                        pltpu.VMEM((SUBLANES, d), F32)],
        compiler_params=pltpu.CompilerParams(
            dimension_semantics=("arbitrary",),
            vmem_limit_bytes=VMEM_LIMIT_BYTES),
        name="in_proj_lru",
    )(x2d, pre_gain, w_qkv, w_g, w_f, perm, perm.T, conv_w, conv_b, w_ri, b_r, b_i, lam,
      lru_gain)


def _forget_cumsum_kernel(flt_ref, bf_ref, c_ref):
    z = flt_ref[...] + bf_ref[...]
    ls = jnp.minimum(z, 0.0) - jnp.log1p(jnp.exp(-jnp.abs(z)))
    seq = ls.shape[1]
    lane = lax.broadcasted_iota(jnp.int32, ls.shape, 1)
    shift = 1
    while shift < seq:
        ls = ls + jnp.where(lane >= shift, pltpu.roll(ls, shift, axis=1), 0.0)
        shift *= 2
    c_ref[...] = ls * LOG2_E


def _forget_cumsum(fl_t, b_f, *, seq):
    n_heads, n = fl_t.shape
    return pl.pallas_call(
        _forget_cumsum_kernel,
        grid=(n // seq,),
        in_specs=[pl.BlockSpec((n_heads, seq), lambda b: (0, b)),
                  pl.BlockSpec((n_heads, 1), lambda b: (0, 0))],
        out_specs=pl.BlockSpec((n_heads, seq), lambda b: (0, b)),
        out_shape=jax.ShapeDtypeStruct((n_heads, n), F32),
        compiler_params=pltpu.CompilerParams(dimension_semantics=("arbitrary",)),
        name="forget_cumsum",
    )(fl_t, b_f)


def _fox_attention_kernel(q_ref, k_ref, v_ref, c_ref, o_ref,
                          s0_scr, s1_scr, s2_scr, s3_scr, r0_scr, r1_scr, r2_scr, r3_scr,
                          vext_scr, m_scr, acc_scr, *, tile, heads_per_step):
    hg = pl.program_id(1)
    tq = 2 * tile
    n_super = q_ref.shape[0] // tq
    half_a = slice(0, tile)
    half_b = slice(tile, tq)
    both = slice(0, tq)

    for hh in range(heads_per_step):
        vext_scr[hh, :, :HEAD_DIM] = v_ref[:, hh * HEAD_DIM:(hh + 1) * HEAD_DIM]
        vext_scr[hh, :, HEAD_DIM:] = jnp.ones((vext_scr.shape[1], HEAD_DIM), BF16)

    def scores(hh, i, j, s_scr, r_scr, rows):
        start = j * tile
        q0 = i * tq + rows.start
        n_rows = rows.stop - rows.start
        cols = slice(hh * HEAD_DIM, (hh + 1) * HEAD_DIM)
        s = lax.dot_general(q_ref[pl.ds(q0, n_rows), cols],
                            k_ref[pl.ds(start, tile), cols],
                            (((1,), (1,)), ((), ())), preferred_element_type=F32)
        s = s - c_ref[pl.ds(hg * heads_per_step + hh, 1), pl.ds(start, tile)]
        s_scr[rows, :] = s
        r_scr[rows, :] = jnp.broadcast_to(jnp.max(s, axis=1, keepdims=True),
                                          (n_rows, HEAD_DIM))

    def consume(hh, j, s_scr, r_scr, rows, diag):
        start = j * tile
        if diag:
            half = tile // 2
            parts = [(slice(rows.start, rows.start + half), half),
                     (slice(rows.start + half, rows.stop), tile)]
            row = lax.broadcasted_iota(jnp.int32, (half, HEAD_DIM), 0)
            lane = lax.broadcasted_iota(jnp.int32, (half, HEAD_DIM), 1)
        else:
            parts = [(rows, tile)]
        for prows, n_keys in parts:
            m_prev = m_scr[prows, :]
            s_chunks = [s_scr[prows, c * HEAD_DIM:(c + 1) * HEAD_DIM]
                        for c in range(n_keys // HEAD_DIM)]
            if diag:
                first = (n_keys - half) // HEAD_DIM
                for c in range(first, len(s_chunks)):
                    keep = row >= lane + (c - first) * HEAD_DIM
                    s_chunks[c] = jnp.where(keep, s_chunks[c], NEG_BIG)
                blk_max = functools.reduce(jnp.maximum, s_chunks)
                m_new = jnp.maximum(m_prev, jnp.max(blk_max, axis=1, keepdims=True))
            else:
                m_new = jnp.maximum(m_prev, r_scr[prows, :])
            alpha = jnp.exp2(m_prev - m_new)
            pb = jnp.concatenate([jnp.exp2(sc - m_new).astype(BF16) for sc in s_chunks],
                                 axis=1)
            pv = jnp.dot(pb, vext_scr[hh, pl.ds(start, n_keys), :],
                         preferred_element_type=F32)
            acc_scr[prows, :HEAD_DIM] = (alpha * acc_scr[prows, :HEAD_DIM]
                                         + pv[:, :HEAD_DIM])
            acc_scr[prows, HEAD_DIM:] = (alpha * acc_scr[prows, HEAD_DIM:]
                                         + pv[:, HEAD_DIM:])
            m_scr[prows, :] = m_new

    def reset():
        m_scr[...] = jnp.full(m_scr.shape, NEG_BIG, F32)
        acc_scr[...] = jnp.zeros(acc_scr.shape, F32)

    assert heads_per_step % 2 == 0
    bufs = ((s0_scr, s1_scr, r0_scr, r1_scr), (s2_scr, s3_scr, r2_scr, r3_scr))

    def super_tile(i, carry):
        for hh in range(heads_per_step):
            sa, sb, ra, rb = bufs[hh % 2]
            nxt_s, _, nxt_r, _ = bufs[(hh + 1) % 2]

            def pair(t, c, hh=hh, sa=sa, sb=sb, ra=ra, rb=rb):
                j = 2 * t
                scores(hh, i, j + 1, sb, rb, both)
                consume(hh, j, sa, ra, both, diag=False)
                scores(hh, i, j + 2, sa, ra, both)
                consume(hh, j + 1, sb, rb, both, diag=False)
                return c

            reset()
            for t in range(i):
                pair(t, 0)
            scores(hh, i, 2 * i + 1, sb, rb, half_b)
            if hh + 1 < heads_per_step:
                scores(hh + 1, i, 0, nxt_s, nxt_r, both)
            elif i + 1 < n_super:
                scores(0, i + 1, 0, nxt_s, nxt_r, both)
            consume(hh, 2 * i, sa, ra, half_a, diag=True)
            consume(hh, 2 * i, sa, ra, half_b, diag=False)
            consume(hh, 2 * i + 1, sb, rb, half_b, diag=True)
            o_ref[pl.ds(i * tq, tq),
                  hh * HEAD_DIM:(hh + 1) * HEAD_DIM] = (
                acc_scr[:, :HEAD_DIM] / acc_scr[:, HEAD_DIM:]).astype(o_ref.dtype)
        return carry

    scores(0, 0, 0, s0_scr, r0_scr, both)
    for i in range(n_super):
        super_tile(i, 0)


def _fox_attention(qkv, c, *, batch, seq, n_heads, tile, heads_per_step):
    n = qkv.shape[0]
    tq = 2 * tile
    hp = heads_per_step
    w = hp * HEAD_DIM
    ng = n_heads // hp
    kern = functools.partial(_fox_attention_kernel, tile=tile, heads_per_step=hp)
    return pl.pallas_call(
        kern,
        grid=(batch, ng),
        in_specs=[
            pl.BlockSpec((seq, w), lambda b, g: (b, g)),
            pl.BlockSpec((seq, w), lambda b, g: (b, ng + g)),
            pl.BlockSpec((seq, w), lambda b, g: (b, 2 * ng + g)),
            pl.BlockSpec((n_heads, seq), lambda b, g: (0, b)),
        ],
        out_specs=pl.BlockSpec((seq, w), lambda b, g: (b, g)),
        out_shape=jax.ShapeDtypeStruct((n, n_heads * HEAD_DIM), BF16),
        scratch_shapes=[pltpu.VMEM((tq, tile), F32)] * 4
                       + [pltpu.VMEM((tq, HEAD_DIM), F32)] * 4
                       + [pltpu.VMEM((hp, seq, 2 * HEAD_DIM), BF16),
                        pltpu.VMEM((tq, HEAD_DIM), F32),
                        pltpu.VMEM((tq, 2 * HEAD_DIM), F32)],
        compiler_params=pltpu.CompilerParams(
            dimension_semantics=("arbitrary", "arbitrary"),
            vmem_limit_bytes=VMEM_LIMIT_BYTES),
        name="fox_attention",
    )(qkv, qkv, qkv, c)


def _out_proj_kernel(oa_ref, ga_ref, yl_ref, x_ref, p_ref, again_ref, wout_ref,
                     pgain_ref, wple_ref, plegain_ref, wpg_ref, bpg_ref, out_ref,
                     *, d_attn, sub_rows):
    subs = [slice(r0, r0 + sub_rows) for r0 in range(0, out_ref.shape[0], sub_rows)]

    def gated_attn(rows):
        oa = oa_ref[rows, :].astype(F32)
        hg = 0.5 * ga_ref[rows, :].astype(F32)
        ya = oa * _rms_scale(oa) * again_ref[...] * (hg * (jnp.tanh(hg) + 1.0))
        return ya.astype(BF16)

    def mixed(rows, ya):
        mix = jnp.dot(ya, wout_ref[0:d_attn, :], preferred_element_type=F32)
        return mix + jnp.dot(yl_ref[rows, :], wout_ref[d_attn:, :],
                             preferred_element_type=F32)

    def embed(rows):
        e = jnp.dot(p_ref[rows, :].astype(BF16), wple_ref[...], preferred_element_type=F32)
        return e * _rms_scale(e) * plegain_ref[...]

    def residual(rows, mix):
        return x_ref[rows, :] + mix * _rms_scale(mix) * pgain_ref[...]

    def gate_logits(h1):
        return jnp.dot(h1.astype(BF16), wpg_ref[...], preferred_element_type=F32)

    ya = [gated_attn(r) for r in subs]
    mix = [mixed(r, y) for r, y in zip(subs, ya)]
    e = [embed(r) for r in subs]
    h1 = [residual(r, m) for r, m in zip(subs, mix)]
    z = [gate_logits(h) for h in h1]
    for r, h, zz, ee in zip(subs, h1, z, e):
        out_ref[r, :] = h + _sigmoid(zz + bpg_ref[...]) * ee


def _out_proj(o_attn, g_attn, y_lru, x2d, p2d, attn_gain, w_out, post_gain, w_ple,
              ple_gain, w_pg, b_pg, *, tm, sub_rows):
    n, d = x2d.shape
    d_attn = o_attn.shape[1]
    d_lru = y_lru.shape[1]
    d_ple = p2d.shape[1]
    kern = functools.partial(_out_proj_kernel, d_attn=d_attn, sub_rows=sub_rows)
    row = lambda i: (i, 0)
    const = lambda i: (0, 0)
    single = pl.Buffered(1)
    return pl.pallas_call(
        kern,
        grid=(n // tm,),
        in_specs=[
            pl.BlockSpec((tm, d_attn), row),
            pl.BlockSpec((tm, d_attn), row),
            pl.BlockSpec((tm, d_lru), row),
            pl.BlockSpec((tm, d), row),
            pl.BlockSpec((tm, d_ple), row),
            pl.BlockSpec((1, d_attn), const),
            pl.BlockSpec(w_out.shape, const, pipeline_mode=single),
            pl.BlockSpec((1, d), const),
            pl.BlockSpec(w_ple.shape, const, pipeline_mode=single),
            pl.BlockSpec((1, d), const),
            pl.BlockSpec(w_pg.shape, const, pipeline_mode=single),
            pl.BlockSpec((1, d), const),
        ],
        out_specs=pl.BlockSpec((tm, d), row),
        out_shape=jax.ShapeDtypeStruct((n, d), F32),
        compiler_params=pltpu.CompilerParams(
            dimension_semantics=("arbitrary",),
            vmem_limit_bytes=VMEM_LIMIT_BYTES),
        name="out_proj",
    )(o_attn, g_attn, y_lru, x2d, p2d, attn_gain, w_out, post_gain, w_ple, ple_gain,
      w_pg, b_pg)


def _layer(h2d, p2d, w_in_all, layer, b_f, pre_gain, post_gain, conv_w, conv_b, w_rgate,
           b_rgate, w_igate, b_igate, lru_lambda, attn_out_gain, lru_out_gain, w_out,
           w_ple, ple_gain, w_ple_gate, b_ple_gate, *, batch, seq):
    d = h2d.shape[1]
    n_heads = b_f.shape[0]
    d_attn = n_heads * HEAD_DIM
    assert w_in_all.shape[2] == 4 * d_attn + n_heads + 2 * d and d == d_attn
    fl_lo = 3 * d_attn
    w_qkv, w_g, w_f = _split_w_in(w_in_all, layer, fl_lo=fl_lo, n_forget=n_heads,
                                  rows=SPLIT_ROWS)
    vec = lambda v: v.reshape(1, -1)

    w_ri = (0.5 * jnp.concatenate([w_rgate, w_igate], axis=-1)).astype(BF16)
    qkv, g_attn, y_lru, fl_t = _in_proj_lru(
        h2d, vec(pre_gain), w_qkv, w_g, w_f, conv_w, vec(conv_b), w_ri, vec(b_rgate),
        vec(b_igate), vec(lru_lambda), vec(lru_out_gain), n_heads=n_heads, seq=seq,
        tm=PROJ_ROWS, chunk=PROJ_COLS, ts=LRU_ROWS)
    c = _forget_cumsum(fl_t, b_f.reshape(n_heads, 1), seq=seq)
    o_attn = _fox_attention(qkv, c, batch=batch, seq=seq, n_heads=n_heads,
                            tile=ATTN_KV_TILE, heads_per_step=ATTN_HEADS_PER_STEP)
    return _out_proj(o_attn, g_attn, y_lru, h2d, p2d, vec(attn_out_gain),
                     w_out.astype(BF16), vec(post_gain), w_ple.astype(BF16),
                     vec(ple_gain), w_ple_gate.astype(BF16), vec(b_ple_gate), tm=OUT_ROWS,
                     sub_rows=OUT_SUB_ROWS)


def kernel(x, p, w_in, b_f, pre_gain, post_gain, conv_w, conv_b, w_rgate, b_rgate,
           w_igate, b_igate, lru_lambda, attn_out_gain, lru_out_gain, w_out, w_ple,
           ple_gain, w_ple_gate, b_ple_gate):
    batch, seq, d = x.shape
    h = x.reshape(batch * seq, d)
    for i in range(w_in.shape[0]):
        h = _layer(h, p[i].reshape(batch * seq, -1), w_in, i, b_f[i], pre_gain[i],
                   post_gain[i], conv_w[i], conv_b[i], w_rgate[i], b_rgate[i],
                   w_igate[i], b_igate[i], lru_lambda[i], attn_out_gain[i],
                   lru_out_gain[i], w_out[i], w_ple[i], ple_gain[i], w_ple_gate[i],
                   b_ple_gate[i], batch=batch, seq=seq)
    return h.reshape(batch, seq, d)
```

```python
import functools
import math

import jax
import jax.numpy as jnp
from jax import lax
from jax.experimental import pallas as pl
from jax.experimental.pallas import tpu as pltpu

HEAD_DIM = 128
N_LRU_BLOCKS = 8
LRU_BLOCK = 128
CONV_WIDTH = 4
LRU_C = 8.0
RMS_EPS = 1e-6
N_FORGET_PAD = 128
SUBLANES = 8
VMEM_LIMIT_BYTES = 56 * 1024 * 1024
NEG_BIG = -1e30
TINY = 1e-30
LOG2_E = math.log2(math.e)

F32 = jnp.float32
BF16 = jnp.bfloat16

PROJ_ROWS = 1024
PROJ_COLS = 512
LRU_ROWS = 256
ATTN_KV_TILE = 512
ATTN_HEADS_PER_STEP = 2
OUT_ROWS = 1024
OUT_SUB_ROWS = 512


def _sigmoid(x):
    return 0.5 * jnp.tanh(0.5 * x) + 0.5


def _rms_scale(x):
    return lax.rsqrt(jnp.mean(x * x, axis=-1, keepdims=True) + RMS_EPS)


def _segment_permutation(ts):
    p = jnp.arange(ts)
    t = (p % SUBLANES) * (ts // SUBLANES) + p // SUBLANES
    return (t[:, None] == jnp.arange(ts)[None, :]).astype(BF16)


def _in_proj_lru_kernel(x_ref, gain_ref, wqkv_ref, wg_ref, wf_ref, perm_ref, unperm_ref,
                        cw_ref, cb_ref, wri_ref, br_ref, bi_ref, lam_ref, lgain_ref,
                        qkv_ref, gattn_ref, ylru_ref, flt_ref,
                        xlru_scr, glru_scr, xpad_scr, a_scr, u_scr, h_scr, acum_scr,
                        tail_scr, carry_scr,
                        *, d_attn, chunk, q_scale, ts, tiles_per_seq):
    tm, d = x_ref.shape
    n_sub = tm // ts
    seg = ts // SUBLANES
    head = (CONV_WIDTH - 1) * SUBLANES

    @pl.when(pl.program_id(0) % tiles_per_seq == 0)
    def _():
        tail_scr[...] = jnp.zeros(tail_scr.shape, F32)
        carry_scr[...] = jnp.zeros(carry_scr.shape, F32)

    x = x_ref[...]
    xn = (x * _rms_scale(x) * gain_ref[...]).astype(BF16)
    perm = perm_ref[...]
    xn_seg = jnp.concatenate(
        [jnp.dot(perm, xn[s * ts:(s + 1) * ts, :], preferred_element_type=F32).astype(BF16)
         for s in range(n_sub)], axis=0)

    for dst, lo in ((xlru_scr, d_attn), (glru_scr, 2 * d_attn)):
        for c0 in range(0, d_attn, chunk):
            dst[:, c0:c0 + chunk] = jnp.dot(xn_seg, wg_ref[:, lo + c0:lo + c0 + chunk],
                                            preferred_element_type=F32)

    neg_lam = -lam_ref[...]
    k_unit = -LRU_C * (jnp.maximum(neg_lam, 0.0) + jnp.log1p(jnp.exp(-jnp.abs(neg_lam))))
    c1 = (0.5 * LOG2_E) * k_unit
    hb_r = 0.5 * br_ref[...]
    hb_i = 0.5 * bi_ref[...]
    sub = lax.broadcasted_iota(jnp.int32, (SUBLANES, d), 0)

    def lru_gates(s):
        base = s * ts
        x_s = xlru_scr[base:base + ts, :]
        pad0 = s * (ts + head)
        xpad_scr[pad0 + head:pad0 + head + ts, :] = x_s
        for m in range(CONV_WIDTH - 1):
            grp = slice(m * SUBLANES, (m + 1) * SUBLANES)
            cur = x_s[ts - head + m * SUBLANES:ts - head + (m + 1) * SUBLANES, :]
            xpad_scr[pad0 + m * SUBLANES:pad0 + (m + 1) * SUBLANES, :] = jnp.where(
                sub == 0, pltpu.roll(tail_scr[grp, :], 1, axis=0),
                pltpu.roll(cur, 1, axis=0))
        tail_scr[...] = x_s[ts - head:ts, :]
        xc = cb_ref[...] + sum(
            cw_ref[j:j + 1, :]
            * xpad_scr[pad0 + j * SUBLANES:pad0 + j * SUBLANES + ts, :]
            for j in range(CONV_WIDTH))
        xcb = xc.astype(BF16)
        for n in range(N_LRU_BLOCKS):
            cols = slice(n * LRU_BLOCK, (n + 1) * LRU_BLOCK)
            gates = jnp.dot(xcb[:, cols], wri_ref[n], preferred_element_type=F32)
            tr = jnp.tanh(gates[:, :LRU_BLOCK] + hb_r[:, cols])
            ti = jnp.tanh(gates[:, LRU_BLOCK:] + hb_i[:, cols])
            a = jnp.exp2(tr * c1[:, cols] + c1[:, cols])
            y2 = 1.0 - a * a
            a_scr[base:base + ts, cols] = a
            u_scr[base:base + ts, cols] = (y2 * lax.rsqrt(jnp.maximum(y2, TINY))) * (
                (ti + 1.0) * (0.5 * xc[:, cols]))

    def lru_scan(s):
        base = s * ts
        h = jnp.zeros((SUBLANES, d), F32)
        acc = jnp.ones((SUBLANES, d), F32)
        for k in range(seg):
            rows = slice(base + k * SUBLANES, base + (k + 1) * SUBLANES)
            a = a_scr[rows, :]
            h = a * h + u_scr[rows, :]
            acc = a * acc
            h_scr[rows, :] = h
            acum_scr[rows, :] = acc
        c = carry_scr[0:1, :]
        seg_in = [c]
        for r in range(SUBLANES):
            c = acc[r:r + 1, :] * c + h[r:r + 1, :]
            seg_in.append(c)
        carry_scr[...] = jnp.broadcast_to(seg_in[SUBLANES], carry_scr.shape)
        return jnp.concatenate(seg_in[:SUBLANES], axis=0)

    def lru_finish(s, c_in):
        base = s * ts
        gain = lgain_ref[...]
        for k in range(seg):
            rows = slice(base + k * SUBLANES, base + (k + 1) * SUBLANES)
            h = h_scr[rows, :] + acum_scr[rows, :] * c_in
            hg = 0.5 * glru_scr[rows, :]
            a_scr[rows, :] = (h * _rms_scale(h)) * gain * (hg * (jnp.tanh(hg) + 1.0))
        y = jnp.dot(unperm_ref[...], a_scr[base:base + ts, :].astype(BF16),
                    preferred_element_type=F32)
        ylru_ref[base:base + ts, :] = y.astype(ylru_ref.dtype)

    def project(dst, w_ref, w_lo, c0, scale):
        z = jnp.dot(xn, w_ref[:, w_lo + c0:w_lo + c0 + chunk], preferred_element_type=F32)
        if scale is not None:
            z = z * scale
        dst[:, c0:c0 + chunk] = z.astype(dst.dtype)

    chunks = [(qkv_ref, wqkv_ref, 0, c0, q_scale if c0 < d_attn else None)
              for c0 in range(0, 3 * d_attn, chunk)]
    chunks += [(gattn_ref, wg_ref, 0, c0, None) for c0 in range(0, d_attn, chunk)]
    lru_steps = []
    carried = {}
    for s in range(n_sub):
        lru_steps.append(lambda s=s: lru_gates(s))
    for s in range(n_sub):
        lru_steps.append(lambda s=s: carried.__setitem__(s, lru_scan(s)))
        lru_steps.append(lambda s=s: lru_finish(s, carried[s]))
    for n, args in enumerate(chunks):
        project(*args)
        if n < len(lru_steps):
            lru_steps[n]()
    for step in lru_steps[len(chunks):]:
        step()
    fl = jnp.dot(xn, wf_ref[...], preferred_element_type=F32)
    flt_ref[...] = fl.T[:flt_ref.shape[0], :]


def _in_proj_lru(x2d, pre_gain, w_qkv, w_g, w_f, conv_w, conv_b, w_ri, b_r, b_i, lam,
                 lru_gain, *, n_heads, seq, tm, chunk, ts):
    n, d = x2d.shape
    d_attn = n_heads * HEAD_DIM
    n_sub = tm // ts
    kern = functools.partial(_in_proj_lru_kernel, d_attn=d_attn, chunk=chunk,
                             q_scale=HEAD_DIM ** -0.5 * LOG2_E, ts=ts,
                             tiles_per_seq=seq // tm)
    perm = _segment_permutation(ts)
    const = lambda i: (0, 0)
    row = lambda i: (i, 0)
    single = pl.Buffered(1)
    vec = pl.BlockSpec((1, d), const)
    mat = pl.BlockSpec((ts, ts), const)
    head = (CONV_WIDTH - 1) * SUBLANES
    return pl.pallas_call(
        kern,
        grid=(n // tm,),
        in_specs=[
            pl.BlockSpec((tm, d), row),
            vec,
            pl.BlockSpec(w_qkv.shape, const, pipeline_mode=single),
            pl.BlockSpec(w_g.shape, const, pipeline_mode=single),
            pl.BlockSpec(w_f.shape, const, pipeline_mode=single),
            mat, mat,
            pl.BlockSpec((CONV_WIDTH, d), const),
            vec,
            pl.BlockSpec(w_ri.shape, lambda i: (0, 0, 0)),
            vec, vec, vec, vec,
        ],
        out_specs=[
            pl.BlockSpec((tm, 3 * d_attn), row),
            pl.BlockSpec((tm, d_attn), row),
            pl.BlockSpec((tm, d), row),
            pl.BlockSpec((n_heads, tm), lambda i: (0, i)),
        ],
        out_shape=[
            jax.ShapeDtypeStruct((n, 3 * d_attn), BF16),
            jax.ShapeDtypeStruct((n, d_attn), BF16),
            jax.ShapeDtypeStruct((n, d), BF16),
            jax.ShapeDtypeStruct((n_heads, n), F32),
        ],
        scratch_shapes=[pltpu.VMEM((tm, d), F32),
                        pltpu.VMEM((tm, d), F32),
                        pltpu.VMEM((n_sub * (ts + head), d), F32),
                        pltpu.VMEM((tm, d), F32),
                        pltpu.VMEM((tm, d), F32),
                        pltpu.VMEM((tm, d), F32),
                        pltpu.VMEM((tm, d), F32),
                        pltpu.VMEM((head, d), F32),
                        pltpu.VMEM((SUBLANES, d), F32)],
        compiler_params=pltpu.CompilerParams(
            dimension_semantics=("arbitrary",),
            vmem_limit_bytes=VMEM_LIMIT_BYTES),
        name="in_proj_lru",
    )(x2d, pre_gain, w_qkv, w_g, w_f, perm, perm.T, conv_w, conv_b, w_ri, b_r, b_i, lam,
      lru_gain)


def _forget_cumsum_kernel(flt_ref, bf_ref, c_ref):
    z = flt_ref[...] + bf_ref[...]
    ls = jnp.minimum(z, 0.0) - jnp.log1p(jnp.exp(-jnp.abs(z)))
    seq = ls.shape[1]
    lane = lax.broadcasted_iota(jnp.int32, ls.shape, 1)
    shift = 1
    while shift < seq:
        ls = ls + jnp.where(lane >= shift, pltpu.roll(ls, shift, axis=1), 0.0)
        shift *= 2
    c_ref[...] = ls * LOG2_E


def _forget_cumsum(fl_t, b_f, *, seq):
    n_heads, n = fl_t.shape
    return pl.pallas_call(
        _forget_cumsum_kernel,
        grid=(n // seq,),
        in_specs=[pl.BlockSpec((n_heads, seq), lambda b: (0, b)),
                  pl.BlockSpec((n_heads, 1), lambda b: (0, 0))],
        out_specs=pl.BlockSpec((n_heads, seq), lambda b: (0, b)),
        out_shape=jax.ShapeDtypeStruct((n_heads, n), F32),
        compiler_params=pltpu.CompilerParams(dimension_semantics=("arbitrary",)),
        name="forget_cumsum",
    )(fl_t, b_f)


def _fox_attention_kernel(q_ref, k_ref, v_ref, c_ref, o_ref,
                          s0_scr, s1_scr, s2_scr, s3_scr, r0_scr, r1_scr, r2_scr, r3_scr,
                          vext_scr, m_scr, acc_scr, *, tile, heads_per_step):
    hg = pl.program_id(1)
    tq = 2 * tile
    n_super = q_ref.shape[0] // tq
    half_a = slice(0, tile)
    half_b = slice(tile, tq)
    both = slice(0, tq)

    for hh in range(heads_per_step):
        vext_scr[hh, :, :HEAD_DIM] = v_ref[:, hh * HEAD_DIM:(hh + 1) * HEAD_DIM]
        vext_scr[hh, :, HEAD_DIM:] = jnp.ones((vext_scr.shape[1], HEAD_DIM), BF16)

    def scores(hh, i, j, s_scr, r_scr, rows):
        start = j * tile
        q0 = i * tq + rows.start
        n_rows = rows.stop - rows.start
        cols = slice(hh * HEAD_DIM, (hh + 1) * HEAD_DIM)
        s = lax.dot_general(q_ref[pl.ds(q0, n_rows), cols],
                            k_ref[pl.ds(start, tile), cols],
                            (((1,), (1,)), ((), ())), preferred_element_type=F32)
        s = s - c_ref[pl.ds(hg * heads_per_step + hh, 1), pl.ds(start, tile)]
        s_scr[rows, :] = s
        r_scr[rows, :] = jnp.broadcast_to(jnp.max(s, axis=1, keepdims=True),
                                          (n_rows, HEAD_DIM))

    def consume(hh, j, s_scr, r_scr, rows, diag):
        start = j * tile
        if diag:
            half = tile // 2
            parts = [(slice(rows.start, rows.start + half), half),
                     (slice(rows.start + half, rows.stop), tile)]
            row = lax.broadcasted_iota(jnp.int32, (half, HEAD_DIM), 0)
            lane = lax.broadcasted_iota(jnp.int32, (half, HEAD_DIM), 1)
        else:
            parts = [(rows, tile)]
        fresh = j == 0
        for prows, n_keys in parts:
            s_chunks = [s_scr[prows, c * HEAD_DIM:(c + 1) * HEAD_DIM]
                        for c in range(n_keys // HEAD_DIM)]
            if diag:
                first = (n_keys - half) // HEAD_DIM
                for c in range(first, len(s_chunks)):
                    keep = row >= lane + (c - first) * HEAD_DIM
                    s_chunks[c] = jnp.where(keep, s_chunks[c], NEG_BIG)
                blk_max = functools.reduce(jnp.maximum, s_chunks)
                m_new = jnp.broadcast_to(jnp.max(blk_max, axis=1, keepdims=True),
                                         blk_max.shape)
            else:
                m_new = r_scr[prows, :]
            if not fresh:
                m_prev = m_scr[prows, :]
                m_new = jnp.maximum(m_prev, m_new)
                alpha = jnp.exp2(m_prev - m_new)
            pb = jnp.concatenate([jnp.exp2(sc - m_new).astype(BF16) for sc in s_chunks],
                                 axis=1)
            pv = jnp.dot(pb, vext_scr[hh, pl.ds(start, n_keys), :],
                         preferred_element_type=F32)
            if fresh:
                acc_scr[prows, :] = pv
            else:
                acc_scr[prows, :HEAD_DIM] = (alpha * acc_scr[prows, :HEAD_DIM]
                                             + pv[:, :HEAD_DIM])
                acc_scr[prows, HEAD_DIM:] = (alpha * acc_scr[prows, HEAD_DIM:]
                                             + pv[:, HEAD_DIM:])
            m_scr[prows, :] = m_new

    assert heads_per_step % 2 == 0
    bufs = ((s0_scr, s1_scr, r0_scr, r1_scr), (s2_scr, s3_scr, r2_scr, r3_scr))

    def super_tile(i, carry):
        for hh in range(heads_per_step):
            sa, sb, ra, rb = bufs[hh % 2]
            nxt_s, _, nxt_r, _ = bufs[(hh + 1) % 2]

            def pair(t, c, hh=hh, sa=sa, sb=sb, ra=ra, rb=rb):
                j = 2 * t
                scores(hh, i, j + 1, sb, rb, both)
                consume(hh, j, sa, ra, both, diag=False)
                scores(hh, i, j + 2, sa, ra, both)
                consume(hh, j + 1, sb, rb, both, diag=False)
                return c

            for t in range(i):
                pair(t, 0)
            scores(hh, i, 2 * i + 1, sb, rb, half_b)
            if hh + 1 < heads_per_step:
                scores(hh + 1, i, 0, nxt_s, nxt_r, both)
            elif i + 1 < n_super:
                scores(0, i + 1, 0, nxt_s, nxt_r, both)
            consume(hh, 2 * i, sa, ra, half_a, diag=True)
            consume(hh, 2 * i, sa, ra, half_b, diag=False)
            consume(hh, 2 * i + 1, sb, rb, half_b, diag=True)
            o_ref[pl.ds(i * tq, tq),
                  hh * HEAD_DIM:(hh + 1) * HEAD_DIM] = (
                acc_scr[:, :HEAD_DIM] / acc_scr[:, HEAD_DIM:]).astype(o_ref.dtype)
        return carry

    scores(0, 0, 0, s0_scr, r0_scr, both)
    for i in range(n_super):
        super_tile(i, 0)


def _fox_attention(qkv, c, *, batch, seq, n_heads, tile, heads_per_step):
    n = qkv.shape[0]
    tq = 2 * tile
    hp = heads_per_step
    w = hp * HEAD_DIM
    ng = n_heads // hp
    kern = functools.partial(_fox_attention_kernel, tile=tile, heads_per_step=hp)
    return pl.pallas_call(
        kern,
        grid=(batch, ng),
        in_specs=[
            pl.BlockSpec((seq, w), lambda b, g: (b, g)),
            pl.BlockSpec((seq, w), lambda b, g: (b, ng + g)),
            pl.BlockSpec((seq, w), lambda b, g: (b, 2 * ng + g)),
            pl.BlockSpec((n_heads, seq), lambda b, g: (0, b)),
        ],
        out_specs=pl.BlockSpec((seq, w), lambda b, g: (b, g)),
        out_shape=jax.ShapeDtypeStruct((n, n_heads * HEAD_DIM), BF16),
        scratch_shapes=[pltpu.VMEM((tq, tile), F32)] * 4
                       + [pltpu.VMEM((tq, HEAD_DIM), F32)] * 4
                       + [pltpu.VMEM((hp, seq, 2 * HEAD_DIM), BF16),
                        pltpu.VMEM((tq, HEAD_DIM), F32),
                        pltpu.VMEM((tq, 2 * HEAD_DIM), F32)],
        compiler_params=pltpu.CompilerParams(
            dimension_semantics=("arbitrary", "arbitrary"),
            vmem_limit_bytes=VMEM_LIMIT_BYTES),
        name="fox_attention",
    )(qkv, qkv, qkv, c)


def _out_proj_kernel(oa_ref, ga_ref, yl_ref, x_ref, p_ref, again_ref, wout_ref,
                     pgain_ref, wple_ref, plegain_ref, wpg_ref, bpg_ref, out_ref,
                     *, d_attn, sub_rows):
    subs = [slice(r0, r0 + sub_rows) for r0 in range(0, out_ref.shape[0], sub_rows)]

    def gated_attn(rows):
        oa = oa_ref[rows, :].astype(F32)
        hg = 0.5 * ga_ref[rows, :].astype(F32)
        ya = oa * _rms_scale(oa) * again_ref[...] * (hg * (jnp.tanh(hg) + 1.0))
        return ya.astype(BF16)

    def mixed(rows, ya):
        mix = jnp.dot(ya, wout_ref[0:d_attn, :], preferred_element_type=F32)
        return mix + jnp.dot(yl_ref[rows, :], wout_ref[d_attn:, :],
                             preferred_element_type=F32)

    def embed(rows):
        e = jnp.dot(p_ref[rows, :].astype(BF16), wple_ref[...], preferred_element_type=F32)
        return e * _rms_scale(e) * plegain_ref[...]

    def residual(rows, mix):
        return x_ref[rows, :] + mix * _rms_scale(mix) * pgain_ref[...]

    def gate_logits(h1):
        return jnp.dot(h1.astype(BF16), wpg_ref[...], preferred_element_type=F32)

    ya = [gated_attn(r) for r in subs]
    mix = [mixed(r, y) for r, y in zip(subs, ya)]
    e = [embed(r) for r in subs]
    h1 = [residual(r, m) for r, m in zip(subs, mix)]
    z = [gate_logits(h) for h in h1]
    for r, h, zz, ee in zip(subs, h1, z, e):
        out_ref[r, :] = h + _sigmoid(zz + bpg_ref[...]) * ee


def _out_proj(o_attn, g_attn, y_lru, x2d, p2d, attn_gain, w_out, post_gain, w_ple,
              ple_gain, w_pg, b_pg, *, tm, sub_rows):
    n, d = x2d.shape
    d_attn = o_attn.shape[1]
    d_lru = y_lru.shape[1]
    d_ple = p2d.shape[1]
    kern = functools.partial(_out_proj_kernel, d_attn=d_attn, sub_rows=sub_rows)
    row = lambda i: (i, 0)
    const = lambda i: (0, 0)
    single = pl.Buffered(1)
    return pl.pallas_call(
        kern,
        grid=(n // tm,),
        in_specs=[
            pl.BlockSpec((tm, d_attn), row),
            pl.BlockSpec((tm, d_attn), row),
            pl.BlockSpec((tm, d_lru), row),
            pl.BlockSpec((tm, d), row),
            pl.BlockSpec((tm, d_ple), row),
            pl.BlockSpec((1, d_attn), const),
            pl.BlockSpec(w_out.shape, const, pipeline_mode=single),
            pl.BlockSpec((1, d), const),
            pl.BlockSpec(w_ple.shape, const, pipeline_mode=single),
            pl.BlockSpec((1, d), const),
            pl.BlockSpec(w_pg.shape, const, pipeline_mode=single),
            pl.BlockSpec((1, d), const),
        ],
        out_specs=pl.BlockSpec((tm, d), row),
        out_shape=jax.ShapeDtypeStruct((n, d), F32),
        compiler_params=pltpu.CompilerParams(
            dimension_semantics=("arbitrary",),
            vmem_limit_bytes=VMEM_LIMIT_BYTES),
        name="out_proj",
    )(o_attn, g_attn, y_lru, x2d, p2d, attn_gain, w_out, post_gain, w_ple, ple_gain,
      w_pg, b_pg)


def _layer(h2d, p2d, w_in, b_f, pre_gain, post_gain, conv_w, conv_b, w_rgate, b_rgate,
           w_igate, b_igate, lru_lambda, attn_out_gain, lru_out_gain, w_out, w_ple,
           ple_gain, w_ple_gate, b_ple_gate, *, batch, seq):
    d = h2d.shape[1]
    n_heads = b_f.shape[0]
    d_attn = n_heads * HEAD_DIM
    assert w_in.shape[1] == 4 * d_attn + n_heads + 2 * d and d == d_attn
    fl_lo = 3 * d_attn
    w_qkv = w_in[:, :fl_lo].astype(BF16)
    w_g = w_in[:, fl_lo + n_heads:].astype(BF16)
    w_f = jnp.pad(w_in[:, fl_lo:fl_lo + n_heads],
                  ((0, 0), (0, N_FORGET_PAD - n_heads))).astype(BF16)
    vec = lambda v: v.reshape(1, -1)

    w_ri = (0.5 * jnp.concatenate([w_rgate, w_igate], axis=-1)).astype(BF16)
    qkv, g_attn, y_lru, fl_t = _in_proj_lru(
        h2d, vec(pre_gain), w_qkv, w_g, w_f, conv_w, vec(conv_b), w_ri, vec(b_rgate),
        vec(b_igate), vec(lru_lambda), vec(lru_out_gain), n_heads=n_heads, seq=seq,
        tm=PROJ_ROWS, chunk=PROJ_COLS, ts=LRU_ROWS)
    c = _forget_cumsum(fl_t, b_f.reshape(n_heads, 1), seq=seq)
    o_attn = _fox_attention(qkv, c, batch=batch, seq=seq, n_heads=n_heads,
                            tile=ATTN_KV_TILE, heads_per_step=ATTN_HEADS_PER_STEP)
    return _out_proj(o_attn, g_attn, y_lru, h2d, p2d, vec(attn_out_gain),
                     w_out.astype(BF16), vec(post_gain), w_ple.astype(BF16),
                     vec(ple_gain), w_ple_gate.astype(BF16), vec(b_ple_gate), tm=OUT_ROWS,
                     sub_rows=OUT_SUB_ROWS)


def kernel(x, p, w_in, b_f, pre_gain, post_gain, conv_w, conv_b, w_rgate, b_rgate,
           w_igate, b_igate, lru_lambda, attn_out_gain, lru_out_gain, w_out, w_ple,
           ple_gain, w_ple_gate, b_ple_gate):
    batch, seq, d = x.shape
    h = x.reshape(batch * seq, d)
    for i in range(w_in.shape[0]):
        h = _layer(h, p[i].reshape(batch * seq, -1), w_in[i], b_f[i], pre_gain[i],
                   post_gain[i], conv_w[i], conv_b[i], w_rgate[i], b_rgate[i],
                   w_igate[i], b_igate[i], lru_lambda[i], attn_out_gain[i],
                   lru_out_gain[i], w_out[i], w_ple[i], ple_gain[i], w_ple_gate[i],
                   b_ple_gate[i], batch=batch, seq=seq)
    return h.reshape(batch, seq, d)
```
